```python
import jax, jax.numpy as jnp
from jax import lax
import numpy as np

D_MODEL = 1024
BATCH = 8
SEQ = 2048
DEPTH = 4

CTX_LEN = 256
GRID_W = 64
EXPAND = 2
D_INNER = EXPAND * D_MODEL
D_FOURIER = D_INNER // 4
FOURIER_GROUP = 64
N_FOURIER_GROUPS = D_FOURIER // FOURIER_GROUP
D_SSD = D_INNER - D_FOURIER
SSD_HEAD_DIM = 64
N_SSD_HEADS = D_SSD // SSD_HEAD_DIM
N_BC_GROUPS = 4
HEADS_PER_GROUP = N_SSD_HEADS // N_BC_GROUPS
D_STATE = 128
CONV_K = 3
CONV_CH = D_SSD + 2 * N_BC_GROUPS * D_STATE
CHUNK = 128
D_PROJ = 2 * D_FOURIER + D_SSD + CONV_CH + 2 * N_SSD_HEADS
PROJ_SPLITS = (D_FOURIER, 2 * D_FOURIER, 2 * D_FOURIER + D_SSD, 2 * D_FOURIER + D_SSD + CONV_CH)
EPS = 1e-6

kernel_name = "hybrid_fourier_ssd_prefix_dit"


def rmsnorm(x, w):
    xf = x.astype(jnp.float32)
    xf = xf * lax.rsqrt(jnp.mean(xf * xf, axis=-1, keepdims=True) + EPS)
    return xf.astype(x.dtype) * w


def segsum(a):
    cum = jnp.cumsum(a, axis=-1)
    diff = cum[..., :, None] - cum[..., None, :]
    t = a.shape[-1]
    mask = jnp.tril(jnp.ones((t, t), dtype=bool))
    return jnp.where(mask, diff, -jnp.inf)


def ssd_chunked(x, dt, A, B, C, h0):
    bsz, length, g, r, p = x.shape
    n = B.shape[-1]
    nc = length // CHUNK
    xc = (x * dt[..., None]).reshape(bsz, nc, CHUNK, g, r, p)
    bc = B.reshape(bsz, nc, CHUNK, g, n)
    cc = C.reshape(bsz, nc, CHUNK, g, n)
    a = (dt * A).reshape(bsz, nc, CHUNK, g, r).transpose(0, 3, 4, 1, 2)
    a_cum = jnp.cumsum(a, axis=-1)
    lmat = jnp.exp(segsum(a))
    cb = jnp.einsum('bclgn,bcsgn->bgcls', cc, bc)
    y_diag = jnp.einsum('bgcls,bgrcls,bcsgrp->bclgrp', cb, lmat, xc)
    decay_states = jnp.exp(a_cum[..., -1:] - a_cum)
    states = jnp.einsum('bcsgn,bgrcs,bcsgrp->bcgrpn', bc, decay_states, xc)
    states = jnp.concatenate([h0[:, None], states], axis=1)
    chunk_a = jnp.pad(a_cum[..., -1], ((0, 0), (0, 0), (0, 0), (1, 0)))
    decay_chunk = jnp.exp(segsum(chunk_a))
    new_states = jnp.einsum('bgrzc,bcgrpn->bzgrpn', decay_chunk, states)
    prev_states, h_final = new_states[:, :-1], new_states[:, -1]
    y_off = jnp.einsum('bclgn,bcgrpn,bgrcl->bclgrp', cc, prev_states, jnp.exp(a_cum))
    y = (y_diag + y_off).reshape(bsz, length, g, r, p)
    return y, h_final


def depthwise_conv_grid(u, w, bias, rows, cols):
    bsz, length, ch = u.shape
    img = u.reshape(bsz, rows, cols, ch)
    out = lax.conv_general_dilated(img, w[:, :, None, :], window_strides=(1, 1), padding='SAME',
                                   dimension_numbers=('NHWC', 'HWIO', 'NHWC'), feature_group_count=ch)
    return jax.nn.silu(out.reshape(bsz, length, ch) + bias)


def fourier_mixer(u, zf, w_f, b_f):
    bsz, length, _ = u.shape
    uf = u.astype(jnp.float32).reshape(bsz, length, N_FOURIER_GROUPS, FOURIER_GROUP)
    mixed = jnp.fft.fft2(uf, axes=(1, 3), norm='ortho').real.reshape(bsz, length, D_FOURIER)
    mixed = mixed.astype(u.dtype) @ w_f + b_f
    return mixed * jax.nn.silu(zf)


def ssd_mixer(xbc, dt_raw, zs, dt_bias, a_log, d_skip, norm_w, h0_f, h0_b):
    bsz, length, _ = xbc.shape
    f32 = jnp.float32
    g, r, p, n = N_BC_GROUPS, HEADS_PER_GROUP, SSD_HEAD_DIM, D_STATE
    xs, bm, cm = jnp.split(xbc.astype(f32), [D_SSD, D_SSD + g * n], axis=-1)
    xs = xs.reshape(bsz, length, g, r, p)
    bm = bm.reshape(bsz, length, g, n)
    cm = cm.reshape(bsz, length, g, n)
    dt = jax.nn.softplus(dt_raw.astype(f32).reshape(bsz, length, 2, N_SSD_HEADS) + dt_bias.astype(f32))
    A = -jnp.exp(a_log.astype(f32)).reshape(2, g, r)
    dt_f = dt[:, :, 0].reshape(bsz, length, g, r)
    dt_b = dt[:, :, 1].reshape(bsz, length, g, r)
    flip = lambda t: jnp.flip(t, axis=1)
    y_f, h_f = ssd_chunked(xs, dt_f, A[0], bm, cm, h0_f)
    y_b, h_b = ssd_chunked(flip(xs), flip(dt_b), A[1], flip(bm), flip(cm), h0_b)
    y = y_f + flip(y_b) + d_skip.astype(f32).reshape(g, r)[:, :, None] * xs
    gated = y.reshape(bsz, length, g, r * p) * jax.nn.silu(zs.astype(f32)).reshape(bsz, length, g, r * p)
    gated = gated * lax.rsqrt(jnp.mean(gated * gated, axis=-1, keepdims=True) + EPS)
    out = gated.reshape(bsz, length, D_SSD).astype(zs.dtype) * norm_w
    return out, h_f, h_b


def token_mixer(h, rows, cols, w_in, conv_w, conv_b, dt_bias, a_log, d_skip, ssd_norm_w,
                w_fourier, b_fourier, h0_f, h0_b):
    u, zf, zs, xbc, dt_raw = jnp.split(h @ w_in, PROJ_SPLITS, axis=-1)
    four = fourier_mixer(u, zf, w_fourier, b_fourier)
    xbc = depthwise_conv_grid(xbc, conv_w, conv_b, rows, cols)
    ssd, h_f, h_b = ssd_mixer(xbc, dt_raw, zs, dt_bias, a_log, d_skip, ssd_norm_w, h0_f, h0_b)
    return jnp.concatenate([four, ssd], axis=-1), h_f, h_b


def setup_inputs(seed: int = 0) -> dict:
    key = jax.random.key(seed)
    ks = jax.random.split(key, 20)
    f32 = jnp.float32
    nrm = lambda k, shape, s: jax.random.normal(k, shape, f32) * s
    dt0 = jnp.exp(jax.random.uniform(ks[9], (DEPTH, 2, N_SSD_HEADS), f32,
                                     minval=np.log(1e-3), maxval=np.log(1e-1)))
    return {
        "x": nrm(ks[0], (BATCH, SEQ, D_MODEL), 1.0),
        "c": nrm(ks[1], (BATCH, D_MODEL), 1.0),
        "ctx": nrm(ks[2], (BATCH, CTX_LEN, D_MODEL), 1.0),
        "c_ctx": nrm(ks[3], (D_MODEL,), 1.0),
        "norm_w": 1.0 + nrm(ks[4], (DEPTH, D_MODEL), 0.05),
        "w_ada": nrm(ks[5], (DEPTH, D_MODEL, 3 * D_MODEL), 0.5 * D_MODEL ** -0.5),
        "b_ada": nrm(ks[6], (DEPTH, 3 * D_MODEL), 0.02),
        "w_in": nrm(ks[7], (DEPTH, D_MODEL, D_PROJ), D_MODEL ** -0.5),
        "conv_w": nrm(ks[8], (DEPTH, CONV_K, CONV_K, CONV_CH), 1.0 / CONV_K),
        "conv_b": nrm(ks[10], (DEPTH, CONV_CH), 0.02),
        "dt_bias": dt0 + jnp.log(-jnp.expm1(-dt0)),
        "a_log": jnp.log(jax.random.uniform(ks[11], (DEPTH, 2, N_SSD_HEADS), f32, minval=1.0, maxval=16.0)),
        "d_skip": 1.0 + nrm(ks[12], (DEPTH, N_SSD_HEADS), 0.1),
        "ssd_norm_w": 1.0 + nrm(ks[13], (DEPTH, D_SSD), 0.05),
        "w_fourier": nrm(ks[14], (DEPTH, D_FOURIER, D_FOURIER), D_FOURIER ** -0.5),
        "b_fourier": nrm(ks[15], (DEPTH, D_FOURIER), 0.02),
        "w_out": nrm(ks[16], (DEPTH, D_INNER, D_MODEL), D_INNER ** -0.5),
        "final_norm_w": 1.0 + nrm(ks[17], (D_MODEL,), 0.05),
    }


def reference(x, c, ctx, c_ctx, norm_w, w_ada, b_ada, w_in, conv_w, conv_b, dt_bias, a_log, d_skip,
              ssd_norm_w, w_fourier, b_fourier, w_out, final_norm_w):
    bsz, seq_len, _ = x.shape
    ctx_len = ctx.shape[1]
    rows = seq_len // GRID_W
    zeros_state = jnp.zeros((bsz, N_BC_GROUPS, HEADS_PER_GROUP, SSD_HEAD_DIM, D_STATE), jnp.float32)
    silu_c = jax.nn.silu(c)
    silu_cc = jax.nn.silu(c_ctx)
    for i in range(DEPTH):
        mod = silu_c @ w_ada[i] + b_ada[i]
        shift, scale, gate = jnp.split(mod, 3, axis=-1)
        mod_c = silu_cc @ w_ada[i] + b_ada[i]
        shift_c, scale_c, gate_c = jnp.split(mod_c, 3, axis=-1)
        layer = (w_in[i], conv_w[i], conv_b[i], dt_bias[i], a_log[i], d_skip[i], ssd_norm_w[i],
                 w_fourier[i], b_fourier[i])
        hc = rmsnorm(ctx, norm_w[i]) * (1.0 + scale_c) + shift_c
        ctx_mix, h_f, h_b = token_mixer(hc, 1, ctx_len, *layer, zeros_state, zeros_state)
        hx = rmsnorm(x, norm_w[i]) * (1.0 + scale[:, None]) + shift[:, None]
        x_mix, _, _ = token_mixer(hx, rows, GRID_W, *layer, h_f, h_b)
        x = x + gate[:, None] * (x_mix @ w_out[i])
        if i < DEPTH - 1:
            ctx = ctx + gate_c * (ctx_mix @ w_out[i])
    return rmsnorm(x, final_norm_w)
```

```python
import functools
import math

import jax
import jax.numpy as jnp
from jax import lax
from jax.experimental import pallas as pl
from jax.experimental.pallas import tpu as pltpu

F32 = jnp.float32
BF16 = jnp.bfloat16

GRID_W = 64
FOURIER_GROUP = 64
HEAD_DIM = 64
N_BC_GROUPS = 4
D_STATE = 128
CHUNK = 128
EPS = 1e-6

LANES = 128
VMEM_LIMIT_BYTES = 56 * 1024 * 1024

HEADS_PER_TILE = LANES // HEAD_DIM


def _cparams(*sem):
    return pltpu.CompilerParams(dimension_semantics=sem, vmem_limit_bytes=VMEM_LIMIT_BYTES)


def _silu(v):
    return v * (1.0 / (1.0 + jnp.exp(-v)))


def _softplus(v):
    return jnp.maximum(v, 0.0) + jnp.log1p(jnp.exp(-jnp.abs(v)))


def _resident(shape):
    zeros = (0,) * len(shape)
    return pl.BlockSpec(shape, lambda *_: zeros, pipeline_mode=pl.Buffered(1))


def _adaln_kernel(c_ref, w_ref, b_ref, o_ref):
    s = _silu(c_ref[...])
    o_ref[...] = jnp.dot(s, w_ref[...], preferred_element_type=F32,
                         precision=lax.Precision.HIGHEST) + b_ref[...]


def _adaln(cc, w_ada, b_ada):
    depth, d, d3 = w_ada.shape
    rows = cc.shape[0]
    tn = 512
    return pl.pallas_call(
        _adaln_kernel,
        out_shape=jax.ShapeDtypeStruct((depth, rows, d3), F32),
        grid=(depth, d3 // tn),
        in_specs=[
            pl.BlockSpec((rows, d), lambda i, j: (0, 0)),
            pl.BlockSpec((None, d, tn), lambda i, j: (i, 0, j)),
            pl.BlockSpec((None, 1, tn), lambda i, j: (i, 0, j)),
        ],
        out_specs=pl.BlockSpec((None, rows, tn), lambda i, j: (i, 0, j)),
        compiler_params=_cparams("parallel", "parallel"),
        name="adaln",
    )(cc, w_ada, b_ada.reshape(depth, 1, d3))


def _inproj_kernel(x_ref, sh_ref, sc_ref, nw_ref, w_ref, wdt_ref,
                   u_ref, zf_ref, zs_ref, xbc_ref, dt_ref, *, splits, col_tile):
    x = x_ref[...]
    ms = jnp.mean(x * x, axis=-1, keepdims=True)
    h = (x * lax.rsqrt(ms + EPS)) * nw_ref[...]
    h = h * (1.0 + sc_ref[...]) + sh_ref[...]
    hb = h.astype(BF16)
    col = 0
    for out_ref, width in zip((u_ref, zf_ref, zs_ref, xbc_ref), splits):
        for c0 in range(0, width, col_tile):
            out_ref[:, c0:c0 + col_tile] = jnp.dot(
                hb, w_ref[:, col + c0:col + c0 + col_tile],
                preferred_element_type=F32).astype(BF16)
        col += width
    dt_ref[...] = jnp.dot(hb, wdt_ref[...], preferred_element_type=F32)


def _inproj(x, shift, scale, norm_w, w_main, w_dt, splits):
    bsz, length, d = x.shape
    tm = min(length, 512)
    n_dt = w_dt.shape[1]
    mod_map = (lambda b, t: (b, 0, 0)) if shift.shape[0] > 1 else (lambda b, t: (0, 0, 0))
    row_map = lambda b, t: (b, t, 0)
    out_shape = [jax.ShapeDtypeStruct((bsz, length, w), BF16) for w in splits]
    out_shape.append(jax.ShapeDtypeStruct((bsz, length, n_dt), F32))
    out_specs = [pl.BlockSpec((None, tm, w), row_map) for w in splits]
    out_specs.append(pl.BlockSpec((None, tm, n_dt), row_map))
    return pl.pallas_call(
        functools.partial(_inproj_kernel, splits=splits, col_tile=512),
        out_shape=out_shape,
        grid=(bsz, length // tm),
        in_specs=[
            pl.BlockSpec((None, tm, d), row_map),
            pl.BlockSpec((None, 1, d), mod_map),
            pl.BlockSpec((None, 1, d), mod_map),
            _resident((1, d)),
            _resident(w_main.shape),
            _resident(w_dt.shape),
        ],
        out_specs=out_specs,
        compiler_params=_cparams("parallel", "parallel"),
        name="inproj",
    )(x, shift, scale, norm_w, w_main, w_dt)


def _fourier_kernel(u_ref, zf_ref, fseq_ref, fch_ref, wf_ref, bf_ref, o_ref, ucs_ref):
    length = u_ref.shape[0]

    @pl.when(pl.program_id(1) == 0)
    def _():
        u = u_ref[...]
        width = u.shape[1]
        ucs_ref[0:length, :] = jnp.dot(
            u, fch_ref[:, 0:width], preferred_element_type=F32).astype(BF16)
        ucs_ref[length:2 * length, :] = jnp.dot(
            u, fch_ref[:, width:2 * width], preferred_element_type=F32).astype(BF16)

    mixed = jnp.dot(fseq_ref[...], ucs_ref[...], preferred_element_type=F32)
    lin = jnp.dot(mixed.astype(BF16), wf_ref[...], preferred_element_type=F32) + bf_ref[...]
    o_ref[...] = (lin * _silu(zf_ref[...].astype(F32))).astype(BF16)


def _fourier(u, zf, fseq, fch, w_f, b_f):
    bsz, length, width = u.shape
    tr = min(length, 512)
    return pl.pallas_call(
        _fourier_kernel,
        out_shape=jax.ShapeDtypeStruct((bsz, length, width), BF16),
        grid=(bsz, length // tr),
        in_specs=[
            pl.BlockSpec((None, length, width), lambda b, t: (b, 0, 0)),
            pl.BlockSpec((None, tr, width), lambda b, t: (b, t, 0)),
            pl.BlockSpec((tr, 2 * length), lambda b, t: (t, 0)),
            _resident(fch.shape),
            _resident(w_f.shape),
            _resident(b_f.shape),
        ],
        out_specs=pl.BlockSpec((None, tr, width), lambda b, t: (b, t, 0)),
        scratch_shapes=[pltpu.VMEM((2 * length, width), BF16)],
        compiler_params=_cparams("parallel", "arbitrary"),
        name="fourier",
    )(u, zf, fseq, fch, w_f, b_f)


def _dft_tables(length, width):
    k = lax.broadcasted_iota(jnp.int32, (length, length), 0)
    l = lax.broadcasted_iota(jnp.int32, (length, length), 1)
    ang = ((k * l) % length).astype(F32) * (2.0 * math.pi / length)
    s = 1.0 / math.sqrt(length)
    fseq = jnp.concatenate([jnp.cos(ang) * s, jnp.sin(ang) * (-s)], axis=1).astype(BF16)
    ci = lax.broadcasted_iota(jnp.int32, (width, width), 0)
    co = lax.broadcasted_iota(jnp.int32, (width, width), 1)
    same = (ci // FOURIER_GROUP) == (co // FOURIER_GROUP)
    angc = (((ci % FOURIER_GROUP) * (co % FOURIER_GROUP)) % FOURIER_GROUP).astype(F32) * (
        2.0 * math.pi / FOURIER_GROUP)
    sc = 1.0 / math.sqrt(FOURIER_GROUP)
    fch = jnp.concatenate([jnp.where(same, jnp.cos(angc) * sc, 0.0),
                           jnp.where(same, jnp.sin(angc) * sc, 0.0)], axis=1).astype(BF16)
    return fseq, fch


def _conv_kernel(x_ref, w_ref, b_ref, o_ref, sm_ref, s0_ref, sp_ref, *, rows, cols, sub):
    length, ct = x_ref.shape
    halo = cols if rows > 1 else 0
    if halo:
        zeros = jnp.zeros((halo, ct), F32)
        for s_ref in (sm_ref, s0_ref, sp_ref):
            s_ref[0:halo, :] = zeros
            s_ref[halo + length:2 * halo + length, :] = zeros

    wpos = lax.broadcasted_iota(jnp.int32, (cols, ct), 0)

    def shift_row(r, carry):
        base = pl.multiple_of(r * cols, cols)
        xr = x_ref[pl.ds(base, cols), :].astype(F32)
        s0_ref[pl.ds(halo + base, cols), :] = xr
        sm_ref[pl.ds(halo + base, cols), :] = jnp.where(
            wpos == 0, 0.0, pltpu.roll(xr, 1, axis=0))
        sp_ref[pl.ds(halo + base, cols), :] = jnp.where(
            wpos == cols - 1, 0.0, pltpu.roll(xr, cols - 1, axis=0))
        return carry

    lax.fori_loop(0, rows, shift_row, 0)

    dhs = (-1, 0, 1) if rows > 1 else (0,)
    bias = b_ref[...]
    taps = [[w_ref[3 * (dh + 1) + kw:3 * (dh + 1) + kw + 1, :] for kw in range(3)] for dh in dhs]

    def out_chunk(c, carry):
        base = pl.multiple_of(c * sub, sub)
        acc = jnp.zeros((sub, ct), F32) + bias
        for di, dh in enumerate(dhs):
            off = halo + base + dh * cols
            for kw, s_ref in enumerate((sm_ref, s0_ref, sp_ref)):
                acc = acc + taps[di][kw] * s_ref[pl.ds(off, sub), :]
        o_ref[pl.ds(base, sub), :] = _silu(acc).astype(BF16)
        return carry

    lax.fori_loop(0, length // sub, out_chunk, 0)


def _conv(xbc, w9, bias, rows, cols):
    bsz, length, ch = xbc.shape
    ct = 256
    halo = cols if rows > 1 else 0
    scratch = [pltpu.VMEM((length + 2 * halo, ct), F32) for _ in range(3)]
    return pl.pallas_call(
        functools.partial(_conv_kernel, rows=rows, cols=cols, sub=64),
        out_shape=jax.ShapeDtypeStruct((bsz, length, ch), BF16),
        grid=(bsz, ch // ct),
        in_specs=[
            pl.BlockSpec((None, length, ct), lambda b, j: (b, 0, j)),
            pl.BlockSpec((9, ct), lambda b, j: (0, j)),
            pl.BlockSpec((1, ct), lambda b, j: (0, j)),
        ],
        out_specs=pl.BlockSpec((None, length, ct), lambda b, j: (b, 0, j)),
        scratch_shapes=scratch,
        compiler_params=_cparams("parallel", "parallel"),
        name="conv",
    )(xbc, w9, bias)


def _split3(v):
    hi = v.astype(BF16)
    r1 = v - hi.astype(F32)
    mid = r1.astype(BF16)
    lo = (r1 - mid.astype(F32)).astype(BF16)
    return hi, mid, lo


def _cumsum_rows(tri, v):
    return sum(jnp.dot(tri, p, preferred_element_type=F32) for p in _split3(v))


def _cumsum_lanes(v, tri_t):
    return sum(jnp.dot(p, tri_t, preferred_element_type=F32) for p in _split3(v))


def _expand_heads(cols, first, n_tiles, lane_lo):
    tiles = []
    for t in range(n_tiles):
        a = cols[:, first + 2 * t:first + 2 * t + 1]
        b = cols[:, first + 2 * t + 1:first + 2 * t + 2]
        tiles.append(jnp.where(lane_lo, a, b))
    return tiles


def _ssd_kernel(x_ref, b_ref, c_ref, z_ref, dtc_ref, dtr_ref, pc_ref, pr_ref, dsk_ref, nw_ref,
                h0f_ref, h0b_ref, y_ref, hf_ref, hb_ref, sball_ref, sf_ref, sb_ref, *, nc, hpg):
    n_tiles = hpg // HEADS_PER_TILE
    ri = lax.broadcasted_iota(jnp.int32, (CHUNK, CHUNK), 0)
    ci = lax.broadcasted_iota(jnp.int32, (CHUNK, CHUNK), 1)
    tri = (ri >= ci).astype(BF16)
    tri_t = (ri <= ci).astype(BF16)
    below = ci < ri
    on_diag = ci == ri
    lane_lo = lax.broadcasted_iota(jnp.int32, (1, LANES), 1) < HEAD_DIM

    bias_c, a_c = pc_ref[0:1, :], -jnp.exp(pc_ref[1:2, :])
    bias_r, a_r = pr_ref[:, 0:1], -jnp.exp(pr_ref[:, 1:2])

    def col_form(c):
        dt = _softplus(dtc_ref[c] + bias_c)
        a = dt * a_c
        return dt, a, _cumsum_rows(tri, a)

    def x_tiles(rows):
        return [x_ref[rows, t * LANES:(t + 1) * LANES] for t in range(n_tiles)]

    def state_update(s_ref, rows, decay_tiles, w_tiles):
        xs = jnp.concatenate(
            [(xt.astype(F32) * wt).astype(BF16) for xt, wt in zip(x_tiles(rows), w_tiles)], axis=1)
        upd = lax.dot_general(b_ref[rows, :], xs, (((0,), (0,)), ((), ())),
                              preferred_element_type=F32)
        s_ref[...] = s_ref[...] * jnp.concatenate(decay_tiles, axis=1) + upd

    sb_ref[...] = h0b_ref[...]

    def bwd_chunk(k, carry):
        c = nc - 1 - k
        rows = pl.ds(pl.multiple_of(c * CHUNK, CHUNK), CHUNK)
        sball_ref[c] = sb_ref[...]
        dt, a, cum = col_form(c)
        excl = cum - a
        total = cum[CHUNK - 1:CHUNK, :]
        w = jnp.exp(excl) * dt
        state_update(sb_ref, rows,
                     _expand_heads(jnp.exp(total), hpg, n_tiles, lane_lo),
                     _expand_heads(w, hpg, n_tiles, lane_lo))
        return carry

    lax.fori_loop(0, nc, bwd_chunk, 0)
    hb_ref[...] = sb_ref[...]

    sf_ref[...] = h0f_ref[...]

    def fwd_chunk(c, carry):
        rows = pl.ds(pl.multiple_of(c * CHUNK, CHUNK), CHUNK)
        dt, a, cum = col_form(c)
        excl = cum - a
        total = cum[CHUNK - 1:CHUNK, :]
        dt_r = _softplus(dtr_ref[c] + bias_r)
        a_row = dt_r * a_r
        cum_r = _cumsum_lanes(a_row, tri_t)
        excl_r = cum_r - a_row

        xs = x_tiles(rows)
        cmat = c_ref[rows, :]
        cb = lax.dot_general(cmat, b_ref[rows, :], (((1,), (1,)), ((), ())),
                             preferred_element_type=F32)
        s_cat = jnp.concatenate([sf_ref[...], sball_ref[c]], axis=1).astype(BF16)
        cs = jnp.dot(cmat, s_cat, preferred_element_type=F32)

        off_f = _expand_heads(jnp.exp(cum), 0, n_tiles, lane_lo)
        off_b = _expand_heads(jnp.exp(total - excl), hpg, n_tiles, lane_lo)
        width = hpg * HEAD_DIM

        def mix(h):
            arg = jnp.where(below, cum[:, h:h + 1] - cum_r[h:h + 1, :],
                            excl_r[hpg + h:hpg + h + 1, :] - excl[:, hpg + h:hpg + h + 1])
            coef = jnp.where(below, dt_r[h:h + 1, :], dt_r[hpg + h:hpg + h + 1, :])
            coef = coef + jnp.where(on_diag, dt_r[h:h + 1, :], 0.0)
            return (cb * jnp.exp(arg) * coef).astype(BF16)

        g_tiles = []
        ssq = jnp.zeros((CHUNK, 1), F32)
        for t in range(n_tiles):
            y_lo = jnp.dot(mix(2 * t), xs[t], preferred_element_type=F32)
            y_hi = jnp.dot(mix(2 * t + 1), xs[t], preferred_element_type=F32)
            y = jnp.where(lane_lo, y_lo, y_hi)
            y = y + cs[:, t * LANES:(t + 1) * LANES] * off_f[t]
            y = y + cs[:, width + t * LANES:width + (t + 1) * LANES] * off_b[t]
            y = y + dsk_ref[:, t * LANES:(t + 1) * LANES] * xs[t].astype(F32)
            g = y * _silu(z_ref[rows, t * LANES:(t + 1) * LANES].astype(F32))
            ssq = ssq + jnp.sum(g * g, axis=-1, keepdims=True)
            g_tiles.append(g)
        inv = lax.rsqrt(ssq * (1.0 / width) + EPS)
        for t in range(n_tiles):
            y_ref[rows, t * LANES:(t + 1) * LANES] = (
                (g_tiles[t] * inv) * nw_ref[:, t * LANES:(t + 1) * LANES]).astype(BF16)

        w = jnp.exp(total - cum) * dt
        state_update(sf_ref, rows,
                     _expand_heads(jnp.exp(total), 0, n_tiles, lane_lo),
                     _expand_heads(w, 0, n_tiles, lane_lo))
        return carry

    lax.fori_loop(0, nc, fwd_chunk, 0)
    hf_ref[...] = sf_ref[...]


def _ssd(xbc, zs, dt, p_col, p_row, d_exp, norm_w, h0f, h0b, d_ssd):
    bsz, length, _ = xbc.shape
    g = N_BC_GROUPS
    gw = d_ssd // g
    hpg = gw // HEAD_DIM
    nc = length // CHUNK
    x_blocks = d_ssd // gw
    bc_first = d_ssd // D_STATE
    dtg = dt.reshape(bsz, nc, CHUNK, 2, g, hpg).transpose(0, 4, 1, 2, 3, 5)
    dtc = dtg.reshape(bsz, g, nc, CHUNK, 2 * hpg)
    dtr = dtc.transpose(0, 1, 2, 4, 3)
    st_shape = (bsz, g, D_STATE, gw)
    st_spec = pl.BlockSpec((None, None, D_STATE, gw), lambda b, j: (b, j, 0, 0))
    return pl.pallas_call(
        functools.partial(_ssd_kernel, nc=nc, hpg=hpg),
        out_shape=[jax.ShapeDtypeStruct((bsz, length, d_ssd), BF16),
                   jax.ShapeDtypeStruct(st_shape, F32),
                   jax.ShapeDtypeStruct(st_shape, F32)],
        grid=(bsz, g),
        in_specs=[
            pl.BlockSpec((None, length, gw), lambda b, j: (b, 0, j)),
            pl.BlockSpec((None, length, D_STATE), lambda b, j: (b, 0, bc_first + j)),
            pl.BlockSpec((None, length, D_STATE), lambda b, j: (b, 0, bc_first + g + j)),
            pl.BlockSpec((None, length, gw), lambda b, j: (b, 0, j)),
            pl.BlockSpec((None, None, nc, CHUNK, 2 * hpg), lambda b, j: (b, j, 0, 0, 0)),
            pl.BlockSpec((None, None, nc, 2 * hpg, CHUNK), lambda b, j: (b, j, 0, 0, 0)),
            pl.BlockSpec((None, 2, 2 * hpg), lambda b, j: (j, 0, 0)),
            pl.BlockSpec((None, 2 * hpg, 2), lambda b, j: (j, 0, 0)),
            pl.BlockSpec((1, gw), lambda b, j: (0, j)),
            pl.BlockSpec((1, gw), lambda b, j: (0, j)),
            st_spec,
            st_spec,
        ],
        out_specs=[pl.BlockSpec((None, length, gw), lambda b, j: (b, 0, j)), st_spec, st_spec],
        scratch_shapes=[pltpu.VMEM((nc, D_STATE, gw), F32),
                        pltpu.VMEM((D_STATE, gw), F32),
                        pltpu.VMEM((D_STATE, gw), F32)],
        compiler_params=_cparams("parallel", "parallel"),
        name="ssd",
    )(xbc, xbc, xbc, zs, dtc, dtr, p_col, p_row, d_exp, norm_w, h0f, h0b)


def _outproj_kernel(four_ref, ssd_ref, x_ref, gate_ref, w_ref, fnw_ref, o_ref, *, final_norm):
    d_f = four_ref.shape[1]
    acc = jnp.dot(four_ref[...], w_ref[0:d_f, :], preferred_element_type=F32)
    acc = acc + jnp.dot(ssd_ref[...], w_ref[d_f:, :], preferred_element_type=F32)
    y = x_ref[...] + gate_ref[...] * acc
    if final_norm:
        ms = jnp.mean(y * y, axis=-1, keepdims=True)
        y = (y * lax.rsqrt(ms + EPS)) * fnw_ref[...]
    o_ref[...] = y


def _outproj(four, ssd, x, gate, w_out, final_norm_w, final_norm):
    bsz, length, d = x.shape
    tm = min(length, 512)
    row_map = lambda b, t: (b, t, 0)
    gate_map = (lambda b, t: (b, 0, 0)) if gate.shape[0] > 1 else (lambda b, t: (0, 0, 0))
    return pl.pallas_call(
        functools.partial(_outproj_kernel, final_norm=final_norm),
        out_shape=jax.ShapeDtypeStruct((bsz, length, d), F32),
        grid=(bsz, length // tm),
        in_specs=[
            pl.BlockSpec((None, tm, four.shape[2]), row_map),
            pl.BlockSpec((None, tm, ssd.shape[2]), row_map),
            pl.BlockSpec((None, tm, d), row_map),
            pl.BlockSpec((None, 1, d), gate_map),
            _resident(w_out.shape),
            _resident((1, d)),
        ],
        out_specs=pl.BlockSpec((None, tm, d), row_map),
        compiler_params=_cparams("parallel", "parallel"),
        name="outproj",
    )(four, ssd, x, gate, w_out, final_norm_w)


def kernel(x, c, ctx, c_ctx, norm_w, w_ada, b_ada, w_in, conv_w, conv_b, dt_bias, a_log, d_skip,
           ssd_norm_w, w_fourier, b_fourier, w_out, final_norm_w):
    bsz, seq_len, d = x.shape
    ctx_len = ctx.shape[1]
    depth = w_in.shape[0]
    d_f = w_fourier.shape[1]
    d_ssd = ssd_norm_w.shape[1]
    conv_ch = conv_w.shape[-1]
    n_heads = d_skip.shape[1]
    g = N_BC_GROUPS
    hpg = n_heads // g
    splits = (d_f, d_f, d_ssd, conv_ch)
    n_main = sum(splits)

    pad_rows = (-(bsz + 1)) % 8
    cc = jnp.concatenate([c, c_ctx[None, :], jnp.zeros((pad_rows, d), F32)], axis=0)
    mod = _adaln(cc, w_ada, b_ada)

    w_main = w_in[:, :, :n_main].astype(BF16)
    w_dt = w_in[:, :, n_main:].astype(BF16)
    w_out_b = w_out.astype(BF16)
    w_f_b = w_fourier.astype(BF16)
    w9 = conv_w.reshape(depth, 9, conv_ch)
    par = jnp.stack([dt_bias, a_log], axis=1)
    p_col = par.reshape(depth, 2, 2, g, hpg).transpose(0, 3, 1, 2, 4).reshape(depth, g, 2, 2 * hpg)
    p_row = p_col.transpose(0, 1, 3, 2)
    d_exp = jnp.repeat(d_skip, HEAD_DIM, axis=1).reshape(depth, 1, d_ssd)

    fseq_x, fch = _dft_tables(seq_len, d_f)
    fseq_c, _ = _dft_tables(ctx_len, d_f)
    zeros_state = jnp.zeros((bsz, g, D_STATE, d_ssd // g), F32)
    rows = seq_len // GRID_W
    fnw = final_norm_w.reshape(1, d)

    def mixer(tokens, shift, scale, i, grid_rows, grid_cols, fseq, h0f, h0b, need_out):
        u, zf, zs, xbc, dt = _inproj(tokens, shift, scale, norm_w[i].reshape(1, d),
                                     w_main[i], w_dt[i], splits)
        xbc = _conv(xbc, w9[i], conv_b[i].reshape(1, conv_ch), grid_rows, grid_cols)
        ssd, hf, hb = _ssd(xbc, zs, dt, p_col[i], p_row[i], d_exp[i],
                           ssd_norm_w[i].reshape(1, d_ssd), h0f, h0b, d_ssd)
        four = None
        if need_out:
            four = _fourier(u, zf, fseq, fch, w_f_b[i], b_fourier[i].reshape(1, d_f))
        return four, ssd, hf, hb

    for i in range(depth):
        last = i == depth - 1
        m = mod[i]
        shift, scale, gate = (m[:bsz, None, k * d:(k + 1) * d] for k in range(3))
        shift_c, scale_c, gate_c = (m[bsz:bsz + 1, None, k * d:(k + 1) * d] for k in range(3))
        four_c, ssd_c, hf, hb = mixer(ctx, shift_c, scale_c, i, 1, ctx_len, fseq_c,
                                      zeros_state, zeros_state, not last)
        four_x, ssd_x, _, _ = mixer(x, shift, scale, i, rows, GRID_W, fseq_x, hf, hb, True)
        x = _outproj(four_x, ssd_x, x, gate, w_out_b[i], fnw, last)
        if not last:
            ctx = _outproj(four_c, ssd_c, ctx, gate_c, w_out_b[i], fnw, False)
    return x
```

```python
import functools
import math

import jax
import jax.numpy as jnp
from jax import lax
from jax.experimental import pallas as pl
from jax.experimental.pallas import tpu as pltpu

F32 = jnp.float32
BF16 = jnp.bfloat16

GRID_W = 64
FOURIER_GROUP = 64
HEAD_DIM = 64
N_BC_GROUPS = 4
D_STATE = 128
CHUNK = 128
EPS = 1e-6

LANES = 128
BF16_SUBLANES = 16
VMEM_LIMIT_BYTES = 56 * 1024 * 1024

HEADS_PER_TILE = LANES // HEAD_DIM


def _cparams(*sem):
    return pltpu.CompilerParams(dimension_semantics=sem, vmem_limit_bytes=VMEM_LIMIT_BYTES)


def _silu(v):
    return v * (1.0 / (1.0 + jnp.exp(-v)))


def _softplus(v):
    return jnp.maximum(v, 0.0) + jnp.log1p(jnp.exp(-jnp.abs(v)))


def _resident(shape):
    zeros = (0,) * len(shape)
    return pl.BlockSpec(shape, lambda *_: zeros, pipeline_mode=pl.Buffered(1))


def _adaln_kernel(c_ref, w_ref, b_ref, o_ref):
    s = _silu(c_ref[...])
    o_ref[...] = jnp.dot(s, w_ref[...], preferred_element_type=F32,
                         precision=lax.Precision.HIGHEST) + b_ref[...]


def _adaln(cc, w_ada, b_ada):
    depth, d, d3 = w_ada.shape
    rows = cc.shape[0]
    tn = 512
    return pl.pallas_call(
        _adaln_kernel,
        out_shape=jax.ShapeDtypeStruct((depth, rows, d3), F32),
        grid=(depth, d3 // tn),
        in_specs=[
            pl.BlockSpec((rows, d), lambda i, j: (0, 0)),
            pl.BlockSpec((None, d, tn), lambda i, j: (i, 0, j)),
            pl.BlockSpec((None, 1, tn), lambda i, j: (i, 0, j)),
        ],
        out_specs=pl.BlockSpec((None, rows, tn), lambda i, j: (i, 0, j)),
        compiler_params=_cparams("parallel", "parallel"),
        name="adaln",
    )(cc, w_ada, b_ada.reshape(depth, 1, d3))


def _inproj_kernel(x_ref, sh_ref, sc_ref, nw_ref, w_ref, wdt_ref,
                   u_ref, zf_ref, zs_ref, xbc_ref, dt_ref, *, splits, col_tile):
    x = x_ref[...]
    ms = jnp.mean(x * x, axis=-1, keepdims=True)
    h = (x * lax.rsqrt(ms + EPS)) * nw_ref[...]
    h = h * (1.0 + sc_ref[...]) + sh_ref[...]
    hb = h.astype(BF16)
    col = 0
    for out_ref, width in zip((u_ref, zf_ref, zs_ref, xbc_ref), splits):
        for c0 in range(0, width, col_tile):
            out_ref[:, c0:c0 + col_tile] = jnp.dot(
                hb, w_ref[:, col + c0:col + c0 + col_tile],
                preferred_element_type=F32).astype(BF16)
        col += width
    dt_ref[...] = jnp.dot(hb, wdt_ref[...], preferred_element_type=F32)


def _inproj(x, shift, scale, norm_w, w_main, w_dt, splits):
    bsz, length, d = x.shape
    tm = min(length, 512)
    n_dt = w_dt.shape[1]
    mod_map = (lambda b, t: (b, 0, 0)) if shift.shape[0] > 1 else (lambda b, t: (0, 0, 0))
    row_map = lambda b, t: (b, t, 0)
    out_shape = [jax.ShapeDtypeStruct((bsz, length, w), BF16) for w in splits]
    out_shape.append(jax.ShapeDtypeStruct((bsz, length, n_dt), F32))
    out_specs = [pl.BlockSpec((None, tm, w), row_map) for w in splits]
    out_specs.append(pl.BlockSpec((None, tm, n_dt), row_map))
    return pl.pallas_call(
        functools.partial(_inproj_kernel, splits=splits, col_tile=512),
        out_shape=out_shape,
        grid=(bsz, length // tm),
        in_specs=[
            pl.BlockSpec((None, tm, d), row_map),
            pl.BlockSpec((None, 1, d), mod_map),
            pl.BlockSpec((None, 1, d), mod_map),
            _resident((1, d)),
            _resident(w_main.shape),
            _resident(w_dt.shape),
        ],
        out_specs=out_specs,
        compiler_params=_cparams("parallel", "parallel"),
        name="inproj",
    )(x, shift, scale, norm_w, w_main, w_dt)


def _fourier_kernel(u_ref, zf_ref, fseq_ref, fch_ref, wf_ref, bf_ref, o_ref, ucs_ref):
    length = u_ref.shape[0]

    @pl.when(pl.program_id(1) == 0)
    def _():
        u = u_ref[...]
        width = u.shape[1]
        ucs_ref[0:length, :] = jnp.dot(
            u, fch_ref[:, 0:width], preferred_element_type=F32).astype(BF16)
        ucs_ref[length:2 * length, :] = jnp.dot(
            u, fch_ref[:, width:2 * width], preferred_element_type=F32).astype(BF16)

    mixed = jnp.dot(fseq_ref[...], ucs_ref[...], preferred_element_type=F32)
    lin = jnp.dot(mixed.astype(BF16), wf_ref[...], preferred_element_type=F32) + bf_ref[...]
    o_ref[...] = (lin * _silu(zf_ref[...].astype(F32))).astype(BF16)


def _fourier(u, zf, fseq, fch, w_f, b_f):
    bsz, length, width = u.shape
    tr = min(length, 512)
    return pl.pallas_call(
        _fourier_kernel,
        out_shape=jax.ShapeDtypeStruct((bsz, length, width), BF16),
        grid=(bsz, length // tr),
        in_specs=[
            pl.BlockSpec((None, length, width), lambda b, t: (b, 0, 0)),
            pl.BlockSpec((None, tr, width), lambda b, t: (b, t, 0)),
            pl.BlockSpec((tr, 2 * length), lambda b, t: (t, 0)),
            _resident(fch.shape),
            _resident(w_f.shape),
            _resident(b_f.shape),
        ],
        out_specs=pl.BlockSpec((None, tr, width), lambda b, t: (b, t, 0)),
        scratch_shapes=[pltpu.VMEM((2 * length, width), BF16)],
        compiler_params=_cparams("parallel", "arbitrary"),
        name="fourier",
    )(u, zf, fseq, fch, w_f, b_f)


def _dft_tables(length, width):
    k = lax.broadcasted_iota(jnp.int32, (length, length), 0)
    l = lax.broadcasted_iota(jnp.int32, (length, length), 1)
    ang = ((k * l) % length).astype(F32) * (2.0 * math.pi / length)
    s = 1.0 / math.sqrt(length)
    fseq = jnp.concatenate([jnp.cos(ang) * s, jnp.sin(ang) * (-s)], axis=1).astype(BF16)
    ci = lax.broadcasted_iota(jnp.int32, (width, width), 0)
    co = lax.broadcasted_iota(jnp.int32, (width, width), 1)
    same = (ci // FOURIER_GROUP) == (co // FOURIER_GROUP)
    angc = (((ci % FOURIER_GROUP) * (co % FOURIER_GROUP)) % FOURIER_GROUP).astype(F32) * (
        2.0 * math.pi / FOURIER_GROUP)
    sc = 1.0 / math.sqrt(FOURIER_GROUP)
    fch = jnp.concatenate([jnp.where(same, jnp.cos(angc) * sc, 0.0),
                           jnp.where(same, jnp.sin(angc) * sc, 0.0)], axis=1).astype(BF16)
    return fseq, fch


def _conv_kernel(x_ref, w_ref, b_ref, o_ref, sm_ref, s0_ref, sp_ref, *, rows, cols, sub):
    length, ct = x_ref.shape
    halo = cols if rows > 1 else 0
    if halo:
        zeros = jnp.zeros((halo, ct), F32)
        for s_ref in (sm_ref, s0_ref, sp_ref):
            s_ref[0:halo, :] = zeros
            s_ref[halo + length:2 * halo + length, :] = zeros

    wpos = lax.broadcasted_iota(jnp.int32, (cols, ct), 0)

    def shift_row(r, carry):
        base = pl.multiple_of(r * cols, cols)
        xr = x_ref[pl.ds(base, cols), :].astype(F32)
        s0_ref[pl.ds(halo + base, cols), :] = xr
        sm_ref[pl.ds(halo + base, cols), :] = jnp.where(
            wpos == 0, 0.0, pltpu.roll(xr, 1, axis=0))
        sp_ref[pl.ds(halo + base, cols), :] = jnp.where(
            wpos == cols - 1, 0.0, pltpu.roll(xr, cols - 1, axis=0))
        return carry

    lax.fori_loop(0, rows, shift_row, 0)

    dhs = (-1, 0, 1) if rows > 1 else (0,)
    bias = b_ref[...]
    taps = [[w_ref[3 * (dh + 1) + kw:3 * (dh + 1) + kw + 1, :] for kw in range(3)] for dh in dhs]

    def out_chunk(c, carry):
        base = pl.multiple_of(c * sub, sub)
        acc = jnp.zeros((sub, ct), F32) + bias
        for di, dh in enumerate(dhs):
            off = halo + base + dh * cols
            for kw, s_ref in enumerate((sm_ref, s0_ref, sp_ref)):
                acc = acc + taps[di][kw] * s_ref[pl.ds(off, sub), :]
        o_ref[pl.ds(base, sub), :] = _silu(acc).astype(BF16)
        return carry

    lax.fori_loop(0, length // sub, out_chunk, 0)


def _conv(xbc, w9, bias, rows, cols):
    bsz, length, ch = xbc.shape
    ct = 256
    halo = cols if rows > 1 else 0
    scratch = [pltpu.VMEM((length + 2 * halo, ct), F32) for _ in range(3)]
    return pl.pallas_call(
        functools.partial(_conv_kernel, rows=rows, cols=cols, sub=64),
        out_shape=jax.ShapeDtypeStruct((bsz, length, ch), BF16),
        grid=(bsz, ch // ct),
        in_specs=[
            pl.BlockSpec((None, length, ct), lambda b, j: (b, 0, j)),
            pl.BlockSpec((9, ct), lambda b, j: (0, j)),
            pl.BlockSpec((1, ct), lambda b, j: (0, j)),
        ],
        out_specs=pl.BlockSpec((None, length, ct), lambda b, j: (b, 0, j)),
        scratch_shapes=scratch,
        compiler_params=_cparams("parallel", "parallel"),
        name="conv",
    )(xbc, w9, bias)


HP = BF16_SUBLANES


def _split2(v):
    hi = v.astype(BF16)
    return hi, (v - hi.astype(F32)).astype(BF16)


def _split3(v):
    hi, _ = _split2(v)
    r1 = v - hi.astype(F32)
    mid, lo = _split2(r1)
    return hi, mid, lo


def _expand_rows(cols, n_tiles, lane_lo):
    tiles = [jnp.where(lane_lo, cols[:, 2 * t:2 * t + 1], cols[:, 2 * t + 1:2 * t + 2])
             for t in range(n_tiles)]
    return jnp.concatenate(tiles, axis=1)


def _ssd_kernel(x_ref, b_ref, c_ref, z_ref, dtr_ref, pr_ref, dsk_ref, nw_ref,
                h0f_ref, h0b_ref, y_ref, hf_ref, hb_ref,
                rowq_ref, tblk_ref, tb_ref, yd_ref, contrib_ref, dec_ref, sball_ref, sf_ref, sb_ref,
                *, nc, hpg):
    gw = hpg * HEAD_DIM
    n_tiles = gw // LANES
    nrow = nc * HP
    ri = lax.broadcasted_iota(jnp.int32, (CHUNK, CHUNK), 0)
    ci = lax.broadcasted_iota(jnp.int32, (CHUNK, CHUNK), 1)
    tri_t = (ri <= ci).astype(BF16)
    below = ci < ri
    on_diag = ci == ri
    lane_lo = lax.broadcasted_iota(jnp.int32, (1, LANES), 1) < HEAD_DIM
    erow = lax.broadcasted_iota(jnp.int32, (CHUNK, 2 * gw), 0)
    ehead = lax.broadcasted_iota(jnp.int32, (CHUNK, 2 * gw), 1) // HEAD_DIM
    expand_w = (erow == ehead + 2 * HP).astype(BF16)
    expand_off = (erow == ehead + 3 * HP).astype(BF16)

    row_fwd = (lax.broadcasted_iota(jnp.int32, (nrow, 1), 0) & (HP - 1)) < hpg
    dt_r = _softplus(dtr_ref[...] + pr_ref[:, 0:1])
    a_r = dt_r * (-jnp.exp(pr_ref[:, 1:2]))
    cum_r = sum(jnp.dot(p, tri_t, preferred_element_type=F32) for p in _split3(a_r))
    excl_r = cum_r - a_r
    tot_r = jnp.broadcast_to(cum_r[:, CHUNK - 1:CHUNK], cum_r.shape)
    log_dt = jnp.log(dt_r)
    rowq_ref[0] = dt_r
    rowq_ref[1] = cum_r - log_dt
    rowq_ref[2] = excl_r + log_dt
    to_cols = (
        jnp.where(row_fwd, cum_r, excl_r),
        tot_r,
        jnp.where(row_fwd, jnp.exp(tot_r - cum_r), jnp.exp(excl_r)) * dt_r,
        jnp.where(row_fwd, jnp.exp(cum_r), jnp.exp(tot_r - excl_r)),
    )
    for c in range(nc):
        for q, val in enumerate(to_cols):
            tblk_ref[c, q * HP:(q + 1) * HP, :] = val[c * HP:(c + 1) * HP, :]
        tblk_ref[c, len(to_cols) * HP:, :] = jnp.zeros((CHUNK - len(to_cols) * HP, CHUNK), F32)

    def chunk_rows(c):
        return pl.ds(pl.multiple_of(c * CHUNK, CHUNK), CHUNK)

    def phase_a(c, carry):
        rows = chunk_rows(c)
        rrows = pl.ds(pl.multiple_of(c * HP, HP), HP)
        blk_t = tblk_ref[c].T
        blk_tb = blk_t.astype(BF16)
        tb_ref[c] = blk_tb
        dt_row = rowq_ref[0, rrows, :]
        cf_row = rowq_ref[1, rrows, :]
        eb_row = rowq_ref[2, rrows, :]

        bmat = b_ref[rows, :]
        cb = lax.dot_general(c_ref[rows, :], bmat, (((1,), (1,)), ((), ())),
                             preferred_element_type=F32)

        def mix(h):
            arg = jnp.where(below, blk_t[:, h:h + 1] - cf_row[h:h + 1, :],
                            eb_row[hpg + h:hpg + h + 1, :] - blk_t[:, hpg + h:hpg + h + 1])
            e = jnp.exp(arg) + jnp.where(on_diag, dt_row[h:h + 1, :], 0.0)
            return (cb * e).astype(BF16)

        w_exp = jnp.dot(blk_tb, expand_w, preferred_element_type=F32)

        xs_f, xs_b = [], []
        for t in range(n_tiles):
            lanes = slice(t * LANES, (t + 1) * LANES)
            xt = x_ref[rows, lanes]
            xf = xt.astype(F32)
            zero = jnp.zeros_like(xt)
            rhs = jnp.concatenate([jnp.where(lane_lo, xt, zero), jnp.where(lane_lo, zero, xt)], axis=0)
            lhs = jnp.concatenate([mix(2 * t), mix(2 * t + 1)], axis=1)
            yd_ref[rows, lanes] = (jnp.dot(lhs, rhs, preferred_element_type=F32)
                                   + dsk_ref[:, lanes] * xf)
            xs_f.append((xf * w_exp[:, t * LANES:(t + 1) * LANES]).astype(BF16))
            xs_b.append((xf * w_exp[:, gw + t * LANES:gw + (t + 1) * LANES]).astype(BF16))
        contrib_ref[c] = lax.dot_general(bmat, jnp.concatenate(xs_f + xs_b, axis=1),
                                         (((0,), (0,)), ((), ())), preferred_element_type=F32)
        dec_ref[c] = _expand_rows(jnp.exp(blk_t[0:1, HP:2 * HP]), 2 * n_tiles, lane_lo)
        return carry

    lax.fori_loop(0, nc, phase_a, 0, unroll=2)

    sb_ref[...] = h0b_ref[...]

    def phase_b(k, carry):
        c = nc - 1 - k
        sb = sb_ref[...]
        sball_ref[c] = sb.astype(BF16)
        sb_ref[...] = sb * dec_ref[c][:, gw:] + contrib_ref[c, :, gw:]
        return carry

    lax.fori_loop(0, nc, phase_b, 0, unroll=2)
    hb_ref[...] = sb_ref[...]

    sf_ref[...] = h0f_ref[...]

    def phase_c(c, carry):
        rows = chunk_rows(c)
        sf = sf_ref[...]
        s_cat = jnp.concatenate([sf.astype(BF16), sball_ref[c]], axis=1)
        cs = jnp.dot(c_ref[rows, :], s_cat, preferred_element_type=F32)
        off = jnp.dot(tb_ref[c], expand_off, preferred_element_type=F32)
        g_tiles = []
        ssq = jnp.zeros((CHUNK, 1), F32)
        for t in range(n_tiles):
            lanes = slice(t * LANES, (t + 1) * LANES)
            blanes = slice(gw + t * LANES, gw + (t + 1) * LANES)
            y = yd_ref[rows, lanes] + cs[:, lanes] * off[:, lanes] + cs[:, blanes] * off[:, blanes]
            g = y * _silu(z_ref[rows, lanes].astype(F32))
            ssq = ssq + jnp.sum(g * g, axis=-1, keepdims=True)
            g_tiles.append(g)
        inv = lax.rsqrt(ssq * (1.0 / gw) + EPS)
        for t in range(n_tiles):
            lanes = slice(t * LANES, (t + 1) * LANES)
            y_ref[rows, lanes] = ((g_tiles[t] * inv) * nw_ref[:, lanes]).astype(BF16)
        sf_ref[...] = sf * dec_ref[c][:, :gw] + contrib_ref[c, :, :gw]
        return carry

    lax.fori_loop(0, nc, phase_c, 0, unroll=2)
    hf_ref[...] = sf_ref[...]


def _ssd(xbc, zs, dt, p_col, d_exp, norm_w, h0f, h0b, d_ssd):
    bsz, length, _ = xbc.shape
    g = N_BC_GROUPS
    gw = d_ssd // g
    hpg = gw // HEAD_DIM
    nc = length // CHUNK
    bc_first = d_ssd // D_STATE
    pad = HP - 2 * hpg
    dtg = dt.reshape(bsz, nc, CHUNK, 2, g, hpg).transpose(0, 4, 1, 3, 5, 2)
    dtr = jnp.pad(dtg.reshape(bsz, g, nc, 2 * hpg, CHUNK), ((0, 0),) * 3 + ((0, pad), (0, 0)))
    dtr = dtr.reshape(bsz, g, nc * HP, CHUNK)
    p_col = jnp.pad(p_col, ((0, 0), (0, 0), (0, pad)))
    p_row = jnp.tile(p_col.transpose(0, 2, 1), (1, nc, 1))
    st_shape = (bsz, g, D_STATE, gw)
    st_spec = pl.BlockSpec((None, None, D_STATE, gw), lambda b, j: (b, j, 0, 0))
    return pl.pallas_call(
        functools.partial(_ssd_kernel, nc=nc, hpg=hpg),
        out_shape=[jax.ShapeDtypeStruct((bsz, length, d_ssd), BF16),
                   jax.ShapeDtypeStruct(st_shape, F32),
                   jax.ShapeDtypeStruct(st_shape, F32)],
        grid=(bsz, g),
        in_specs=[
            pl.BlockSpec((None, length, gw), lambda b, j: (b, 0, j)),
            pl.BlockSpec((None, length, D_STATE), lambda b, j: (b, 0, bc_first + j)),
            pl.BlockSpec((None, length, D_STATE), lambda b, j: (b, 0, bc_first + g + j)),
            pl.BlockSpec((None, length, gw), lambda b, j: (b, 0, j)),
            pl.BlockSpec((None, None, nc * HP, CHUNK), lambda b, j: (b, j, 0, 0)),
            pl.BlockSpec((None, nc * HP, 2), lambda b, j: (j, 0, 0)),
            pl.BlockSpec((1, gw), lambda b, j: (0, j)),
            pl.BlockSpec((1, gw), lambda b, j: (0, j)),
            st_spec,
            st_spec,
        ],
        out_specs=[pl.BlockSpec((None, length, gw), lambda b, j: (b, 0, j)), st_spec, st_spec],
        scratch_shapes=[pltpu.VMEM((3, nc * HP, CHUNK), F32),
                        pltpu.VMEM((nc, CHUNK, CHUNK), F32),
                        pltpu.VMEM((nc, CHUNK, CHUNK), BF16),
                        pltpu.VMEM((length, gw), F32),
                        pltpu.VMEM((nc, D_STATE, 2 * gw), F32),
                        pltpu.VMEM((nc, 1, 2 * gw), F32),
                        pltpu.VMEM((nc, D_STATE, gw), BF16),
                        pltpu.VMEM((D_STATE, gw), F32),
                        pltpu.VMEM((D_STATE, gw), F32)],
        compiler_params=_cparams("parallel", "parallel"),
        name="ssd",
    )(xbc, xbc, xbc, zs, dtr, p_row, d_exp, norm_w, h0f, h0b)


def _outproj_kernel(four_ref, ssd_ref, x_ref, gate_ref, w_ref, fnw_ref, o_ref, *, final_norm):
    d_f = four_ref.shape[1]
    acc = jnp.dot(four_ref[...], w_ref[0:d_f, :], preferred_element_type=F32)
    acc = acc + jnp.dot(ssd_ref[...], w_ref[d_f:, :], preferred_element_type=F32)
    y = x_ref[...] + gate_ref[...] * acc
    if final_norm:
        ms = jnp.mean(y * y, axis=-1, keepdims=True)
        y = (y * lax.rsqrt(ms + EPS)) * fnw_ref[...]
    o_ref[...] = y


def _outproj(four, ssd, x, gate, w_out, final_norm_w, final_norm):
    bsz, length, d = x.shape
    tm = min(length, 512)
    row_map = lambda b, t: (b, t, 0)
    gate_map = (lambda b, t: (b, 0, 0)) if gate.shape[0] > 1 else (lambda b, t: (0, 0, 0))
    return pl.pallas_call(
        functools.partial(_outproj_kernel, final_norm=final_norm),
        out_shape=jax.ShapeDtypeStruct((bsz, length, d), F32),
        grid=(bsz, length // tm),
        in_specs=[
            pl.BlockSpec((None, tm, four.shape[2]), row_map),
            pl.BlockSpec((None, tm, ssd.shape[2]), row_map),
            pl.BlockSpec((None, tm, d), row_map),
            pl.BlockSpec((None, 1, d), gate_map),
            _resident(w_out.shape),
            _resident((1, d)),
        ],
        out_specs=pl.BlockSpec((None, tm, d), row_map),
        compiler_params=_cparams("parallel", "parallel"),
        name="outproj",
    )(four, ssd, x, gate, w_out, final_norm_w)


def kernel(x, c, ctx, c_ctx, norm_w, w_ada, b_ada, w_in, conv_w, conv_b, dt_bias, a_log, d_skip,
           ssd_norm_w, w_fourier, b_fourier, w_out, final_norm_w):
    bsz, seq_len, d = x.shape
    ctx_len = ctx.shape[1]
    depth = w_in.shape[0]
    d_f = w_fourier.shape[1]
    d_ssd = ssd_norm_w.shape[1]
    conv_ch = conv_w.shape[-1]
    n_heads = d_skip.shape[1]
    g = N_BC_GROUPS
    hpg = n_heads // g
    splits = (d_f, d_f, d_ssd, conv_ch)
    n_main = sum(splits)

    pad_rows = (-(bsz + 1)) % 8
    cc = jnp.concatenate([c, c_ctx[None, :], jnp.zeros((pad_rows, d), F32)], axis=0)
    mod = _adaln(cc, w_ada, b_ada)

    w_main = w_in[:, :, :n_main].astype(BF16)
    w_dt = w_in[:, :, n_main:].astype(BF16)
    w_out_b = w_out.astype(BF16)
    w_f_b = w_fourier.astype(BF16)
    w9 = conv_w.reshape(depth, 9, conv_ch)
    par = jnp.stack([dt_bias, a_log], axis=1)
    p_col = par.reshape(depth, 2, 2, g, hpg).transpose(0, 3, 1, 2, 4).reshape(depth, g, 2, 2 * hpg)
    d_exp = jnp.repeat(d_skip, HEAD_DIM, axis=1).reshape(depth, 1, d_ssd)

    fseq_x, fch = _dft_tables(seq_len, d_f)
    fseq_c, _ = _dft_tables(ctx_len, d_f)
    zeros_state = jnp.zeros((bsz, g, D_STATE, d_ssd // g), F32)
    rows = seq_len // GRID_W
    fnw = final_norm_w.reshape(1, d)

    def mixer(tokens, shift, scale, i, grid_rows, grid_cols, fseq, h0f, h0b, need_out):
        u, zf, zs, xbc, dt = _inproj(tokens, shift, scale, norm_w[i].reshape(1, d),
                                     w_main[i], w_dt[i], splits)
        xbc = _conv(xbc, w9[i], conv_b[i].reshape(1, conv_ch), grid_rows, grid_cols)
        ssd, hf, hb = _ssd(xbc, zs, dt, p_col[i], d_exp[i],
                           ssd_norm_w[i].reshape(1, d_ssd), h0f, h0b, d_ssd)
        four = None
        if need_out:
            four = _fourier(u, zf, fseq, fch, w_f_b[i], b_fourier[i].reshape(1, d_f))
        return four, ssd, hf, hb

    for i in range(depth):
        last = i == depth - 1
        m = mod[i]
        shift, scale, gate = (m[:bsz, None, k * d:(k + 1) * d] for k in range(3))
        shift_c, scale_c, gate_c = (m[bsz:bsz + 1, None, k * d:(k + 1) * d] for k in range(3))
        four_c, ssd_c, hf, hb = mixer(ctx, shift_c, scale_c, i, 1, ctx_len, fseq_c,
                                      zeros_state, zeros_state, not last)
        four_x, ssd_x, _, _ = mixer(x, shift, scale, i, rows, GRID_W, fseq_x, hf, hb, True)
        x = _outproj(four_x, ssd_x, x, gate, w_out_b[i], fnw, last)
        if not last:
            ctx = _outproj(four_c, ssd_c, ctx, gate_c, w_out_b[i], fnw, False)
    return x
```

```python
import functools
import math

import jax
import jax.numpy as jnp
from jax import lax
from jax.experimental import pallas as pl
from jax.experimental.pallas import tpu as pltpu

F32 = jnp.float32
BF16 = jnp.bfloat16

GRID_W = 64
FOURIER_GROUP = 64
HEAD_DIM = 64
N_BC_GROUPS = 4
D_STATE = 128
CHUNK = 128
EPS = 1e-6
LOG2E = 1.4426950408889634

LANES = 128
BF16_SUBLANES = 16
VMEM_LIMIT_BYTES = 56 * 1024 * 1024

HEADS_PER_TILE = LANES // HEAD_DIM


def _cparams(*sem):
    return pltpu.CompilerParams(dimension_semantics=sem, vmem_limit_bytes=VMEM_LIMIT_BYTES)


def _silu(v):
    return v * (1.0 / (1.0 + jnp.exp2(v * (-LOG2E))))


def _softplus(v):
    return jnp.maximum(v, 0.0) + jnp.log1p(jnp.exp(-jnp.abs(v)))


def _resident(shape):
    zeros = (0,) * len(shape)
    return pl.BlockSpec(shape, lambda *_: zeros, pipeline_mode=pl.Buffered(1))


def _adaln_kernel(c_ref, w_ref, b_ref, o_ref):
    s = _silu(c_ref[...])
    o_ref[...] = jnp.dot(s, w_ref[...], preferred_element_type=F32,
                         precision=lax.Precision.HIGHEST) + b_ref[...]


def _adaln(cc, w_ada, b_ada):
    depth, d, d3 = w_ada.shape
    rows = cc.shape[0]
    tn = 512
    return pl.pallas_call(
        _adaln_kernel,
        out_shape=jax.ShapeDtypeStruct((depth, rows, d3), F32),
        grid=(depth, d3 // tn),
        in_specs=[
            pl.BlockSpec((rows, d), lambda i, j: (0, 0)),
            pl.BlockSpec((None, d, tn), lambda i, j: (i, 0, j)),
            pl.BlockSpec((None, 1, tn), lambda i, j: (i, 0, j)),
        ],
        out_specs=pl.BlockSpec((None, rows, tn), lambda i, j: (i, 0, j)),
        compiler_params=_cparams("parallel", "parallel"),
        name="adaln",
    )(cc, w_ada, b_ada.reshape(depth, 1, d3))


def _inproj_kernel(x_ref, sh_ref, sc_ref, nw_ref, w_ref, wdt_ref,
                   u_ref, zf_ref, zs_ref, xbc_ref, dt_ref, *, splits, col_tile):
    x = x_ref[...]
    ms = jnp.mean(x * x, axis=-1, keepdims=True)
    h = (x * lax.rsqrt(ms + EPS)) * nw_ref[...]
    h = h * (1.0 + sc_ref[...]) + sh_ref[...]
    hb = h.astype(BF16)
    col = 0
    for out_ref, width in zip((u_ref, zf_ref, zs_ref, xbc_ref), splits):
        for c0 in range(0, width, col_tile):
            out_ref[:, c0:c0 + col_tile] = jnp.dot(
                hb, w_ref[:, col + c0:col + c0 + col_tile],
                preferred_element_type=F32).astype(BF16)
        col += width
    dt_ref[...] = jnp.dot(hb, wdt_ref[...], preferred_element_type=F32)


def _inproj(x, shift, scale, norm_w, w_main, w_dt, splits):
    bsz, length, d = x.shape
    tm = min(length, 512)
    n_dt = w_dt.shape[1]
    mod_map = (lambda b, t: (b, 0, 0)) if shift.shape[0] > 1 else (lambda b, t: (0, 0, 0))
    row_map = lambda b, t: (b, t, 0)
    out_shape = [jax.ShapeDtypeStruct((bsz, length, w), BF16) for w in splits]
    out_shape.append(jax.ShapeDtypeStruct((bsz, length, n_dt), F32))
    out_specs = [pl.BlockSpec((None, tm, w), row_map) for w in splits]
    out_specs.append(pl.BlockSpec((None, tm, n_dt), row_map))
    return pl.pallas_call(
        functools.partial(_inproj_kernel, splits=splits, col_tile=512),
        out_shape=out_shape,
        grid=(bsz, length // tm),
        in_specs=[
            pl.BlockSpec((None, tm, d), row_map),
            pl.BlockSpec((None, 1, d), mod_map),
            pl.BlockSpec((None, 1, d), mod_map),
            _resident((1, d)),
            _resident(w_main.shape),
            _resident(w_dt.shape),
        ],
        out_specs=out_specs,
        compiler_params=_cparams("parallel", "parallel"),
        name="inproj",
    )(x, shift, scale, norm_w, w_main, w_dt)


def _fourier_kernel(u_ref, ur_ref, zf_ref, fseq_ref, fch_ref, wf_ref, bf_ref, o_ref, ucs_ref, mid_ref):
    length, width = u_ref.shape
    half = length // 2
    tr = o_ref.shape[0]

    @pl.when(pl.program_id(1) == 0)
    def _():
        uh = u_ref[0:half, :]
        ur = ur_ref[...]
        cos_ch = fch_ref[:, 0:width]
        sin_ch = fch_ref[:, width:2 * width]
        ucs_ref[0:half, :] = (jnp.dot(uh, cos_ch, preferred_element_type=F32)
                              + jnp.dot(ur, cos_ch, preferred_element_type=F32)).astype(BF16)
        ucs_ref[half:length, :] = (jnp.dot(uh, sin_ch, preferred_element_type=F32)
                                   - jnp.dot(ur, sin_ch, preferred_element_type=F32)).astype(BF16)
        mid_ref[...] = jnp.dot(u_ref[half:half + BF16_SUBLANES, :], cos_ch,
                               preferred_element_type=F32) * (1.0 / math.sqrt(length))

    mixed = jnp.dot(fseq_ref[...], ucs_ref[...], preferred_element_type=F32)
    k = pl.program_id(1) * tr + lax.broadcasted_iota(jnp.int32, (tr, 1), 0)
    sign = (1 - 2 * (k & 1)).astype(F32)
    mixed = mixed + sign * mid_ref[0:1, :]
    lin = jnp.dot(mixed.astype(BF16), wf_ref[...], preferred_element_type=F32) + bf_ref[...]
    o_ref[...] = (lin * _silu(zf_ref[...].astype(F32))).astype(BF16)


def _fourier(u, zf, fseq, fch, w_f, b_f):
    bsz, length, width = u.shape
    tr = min(length, 512)
    half = length // 2
    u_rev = jnp.roll(jnp.flip(u, axis=1), 1, axis=1)[:, :half]
    return pl.pallas_call(
        _fourier_kernel,
        out_shape=jax.ShapeDtypeStruct((bsz, length, width), BF16),
        grid=(bsz, length // tr),
        in_specs=[
            pl.BlockSpec((None, length, width), lambda b, t: (b, 0, 0)),
            pl.BlockSpec((None, half, width), lambda b, t: (b, 0, 0)),
            pl.BlockSpec((None, tr, width), lambda b, t: (b, t, 0)),
            pl.BlockSpec((tr, length), lambda b, t: (t, 0)),
            _resident(fch.shape),
            _resident(w_f.shape),
            _resident(b_f.shape),
        ],
        out_specs=pl.BlockSpec((None, tr, width), lambda b, t: (b, t, 0)),
        scratch_shapes=[pltpu.VMEM((length, width), BF16),
                        pltpu.VMEM((BF16_SUBLANES, width), F32)],
        compiler_params=_cparams("parallel", "arbitrary"),
        name="fourier",
    )(u, u_rev, zf, fseq, fch, w_f, b_f)


def _dft_tables(length, width):
    half = length // 2
    k = lax.broadcasted_iota(jnp.int32, (length, half), 0)
    l = lax.broadcasted_iota(jnp.int32, (length, half), 1)
    ang = ((k * l) % length).astype(F32) * (2.0 * math.pi / length)
    s = 1.0 / math.sqrt(length)
    cos_w = jnp.where(l == 0, 0.5 * s, s)
    fseq = jnp.concatenate([jnp.cos(ang) * cos_w, jnp.sin(ang) * (-s)], axis=1).astype(BF16)
    ci = lax.broadcasted_iota(jnp.int32, (width, width), 0)
    co = lax.broadcasted_iota(jnp.int32, (width, width), 1)
    same = (ci // FOURIER_GROUP) == (co // FOURIER_GROUP)
    angc = (((ci % FOURIER_GROUP) * (co % FOURIER_GROUP)) % FOURIER_GROUP).astype(F32) * (
        2.0 * math.pi / FOURIER_GROUP)
    sc = 1.0 / math.sqrt(FOURIER_GROUP)
    fch = jnp.concatenate([jnp.where(same, jnp.cos(angc) * sc, 0.0),
                           jnp.where(same, jnp.sin(angc) * sc, 0.0)], axis=1).astype(BF16)
    return fseq, fch


def _conv_kernel(x_ref, w_ref, b_ref, o_ref, sm_ref, s0_ref, sp_ref, *, rows, cols, sub):
    length, ct = x_ref.shape
    halo = cols if rows > 1 else 0
    if halo:
        zeros = jnp.zeros((halo, ct), F32)
        for s_ref in (sm_ref, s0_ref, sp_ref):
            s_ref[0:halo, :] = zeros
            s_ref[halo + length:2 * halo + length, :] = zeros

    wpos = lax.broadcasted_iota(jnp.int32, (cols, ct), 0)

    def shift_row(r, carry):
        base = pl.multiple_of(r * cols, cols)
        xr = x_ref[pl.ds(base, cols), :].astype(F32)
        s0_ref[pl.ds(halo + base, cols), :] = xr
        sm_ref[pl.ds(halo + base, cols), :] = jnp.where(
            wpos == 0, 0.0, pltpu.roll(xr, 1, axis=0))
        sp_ref[pl.ds(halo + base, cols), :] = jnp.where(
            wpos == cols - 1, 0.0, pltpu.roll(xr, cols - 1, axis=0))
        return carry

    lax.fori_loop(0, rows, shift_row, 0, unroll=2 if rows > 1 else 1)

    dhs = (-1, 0, 1) if rows > 1 else (0,)
    bias = b_ref[...]
    taps = [[w_ref[3 * (dh + 1) + kw:3 * (dh + 1) + kw + 1, :] for kw in range(3)] for dh in dhs]

    def out_chunk(c, carry):
        base = pl.multiple_of(c * sub, sub)
        acc = bias
        for di, dh in enumerate(dhs):
            off = halo + base + dh * cols
            for kw, s_ref in enumerate((sm_ref, s0_ref, sp_ref)):
                acc = acc + taps[di][kw] * s_ref[pl.ds(off, sub), :]
        o_ref[pl.ds(base, sub), :] = _silu(acc).astype(BF16)
        return carry

    lax.fori_loop(0, length // sub, out_chunk, 0, unroll=2)


def _conv(xbc, w9, bias, rows, cols):
    bsz, length, ch = xbc.shape
    ct = 256
    halo = cols if rows > 1 else 0
    scratch = [pltpu.VMEM((length + 2 * halo, ct), F32) for _ in range(3)]
    return pl.pallas_call(
        functools.partial(_conv_kernel, rows=rows, cols=cols, sub=64),
        out_shape=jax.ShapeDtypeStruct((bsz, length, ch), BF16),
        grid=(bsz, ch // ct),
        in_specs=[
            pl.BlockSpec((None, length, ct), lambda b, j: (b, 0, j)),
            pl.BlockSpec((9, ct), lambda b, j: (0, j)),
            pl.BlockSpec((1, ct), lambda b, j: (0, j)),
        ],
        out_specs=pl.BlockSpec((None, length, ct), lambda b, j: (b, 0, j)),
        scratch_shapes=scratch,
        compiler_params=_cparams("parallel", "parallel"),
        name="conv",
    )(xbc, w9, bias)


HP = BF16_SUBLANES


def _split2(v):
    hi = v.astype(BF16)
    return hi, (v - hi.astype(F32)).astype(BF16)


def _split3(v):
    hi, _ = _split2(v)
    r1 = v - hi.astype(F32)
    mid, lo = _split2(r1)
    return hi, mid, lo


def _expand_rows(cols, n_tiles, lane_lo):
    tiles = [jnp.where(lane_lo, cols[:, 2 * t:2 * t + 1], cols[:, 2 * t + 1:2 * t + 2])
             for t in range(n_tiles)]
    return jnp.concatenate(tiles, axis=1)


def _ssd_kernel(x_ref, b_ref, c_ref, z_ref, dtr_ref, pr_ref, dsk_ref, nw_ref,
                h0f_ref, h0b_ref, y_ref, hf_ref, hb_ref,
                rowq_ref, tblk_ref, tb_ref, yd_ref, contrib_ref, dec_ref, sball_ref, sf_ref, sb_ref,
                *, nc, hpg):
    gw = hpg * HEAD_DIM
    n_tiles = gw // LANES
    nrow = nc * HP
    ri = lax.broadcasted_iota(jnp.int32, (CHUNK, CHUNK), 0)
    ci = lax.broadcasted_iota(jnp.int32, (CHUNK, CHUNK), 1)
    tri_t = (ri <= ci).astype(BF16)
    below = ci < ri
    on_diag = ci == ri
    lane_lo = lax.broadcasted_iota(jnp.int32, (1, LANES), 1) < HEAD_DIM
    erow = lax.broadcasted_iota(jnp.int32, (CHUNK, 2 * gw), 0)
    ehead = lax.broadcasted_iota(jnp.int32, (CHUNK, 2 * gw), 1) // HEAD_DIM
    expand_w = (erow == ehead + 2 * HP).astype(BF16)
    expand_off = (erow == ehead + 3 * HP).astype(BF16)

    row_fwd = (lax.broadcasted_iota(jnp.int32, (nrow, 1), 0) & (HP - 1)) < hpg
    dt_r = _softplus(dtr_ref[...] + pr_ref[:, 0:1])
    a_r = dt_r * (-jnp.exp(pr_ref[:, 1:2]))
    cum_r = sum(jnp.dot(p, tri_t, preferred_element_type=F32) for p in _split3(a_r))
    excl_r = cum_r - a_r
    tot_r = jnp.broadcast_to(cum_r[:, CHUNK - 1:CHUNK], cum_r.shape)
    log_dt = jnp.log(dt_r)
    rowq_ref[0] = dt_r
    rowq_ref[1] = cum_r - log_dt
    rowq_ref[2] = excl_r + log_dt
    to_cols = (
        jnp.where(row_fwd, cum_r, excl_r),
        tot_r,
        jnp.where(row_fwd, jnp.exp(tot_r - cum_r), jnp.exp(excl_r)) * dt_r,
        jnp.where(row_fwd, jnp.exp(cum_r), jnp.exp(tot_r - excl_r)),
    )
    for c in range(nc):
        for q, val in enumerate(to_cols):
            tblk_ref[c, q * HP:(q + 1) * HP, :] = val[c * HP:(c + 1) * HP, :]
        tblk_ref[c, len(to_cols) * HP:, :] = jnp.zeros((CHUNK - len(to_cols) * HP, CHUNK), F32)

    def chunk_rows(c):
        return pl.ds(pl.multiple_of(c * CHUNK, CHUNK), CHUNK)

    def phase_a(c, carry):
        rows = chunk_rows(c)
        rrows = pl.ds(pl.multiple_of(c * HP, HP), HP)
        blk_t = tblk_ref[c].T
        blk_tb = blk_t.astype(BF16)
        tb_ref[c] = blk_tb
        dt_row = rowq_ref[0, rrows, :]
        cf_row = rowq_ref[1, rrows, :]
        eb_row = rowq_ref[2, rrows, :]

        bmat = b_ref[rows, :]
        cb = lax.dot_general(c_ref[rows, :], bmat, (((1,), (1,)), ((), ())),
                             preferred_element_type=F32)

        def mix(h):
            arg = jnp.where(below, blk_t[:, h:h + 1] - cf_row[h:h + 1, :],
                            eb_row[hpg + h:hpg + h + 1, :] - blk_t[:, hpg + h:hpg + h + 1])
            e = jnp.exp(arg) + jnp.where(on_diag, dt_row[h:h + 1, :], 0.0)
            return (cb * e).astype(BF16)

        w_exp = jnp.dot(blk_tb, expand_w, preferred_element_type=F32)

        xs_f, xs_b = [], []
        for t in range(n_tiles):
            lanes = slice(t * LANES, (t + 1) * LANES)
            xt = x_ref[rows, lanes]
            xf = xt.astype(F32)
            zero = jnp.zeros_like(xt)
            rhs = jnp.concatenate([jnp.where(lane_lo, xt, zero), jnp.where(lane_lo, zero, xt)], axis=0)
            lhs = jnp.concatenate([mix(2 * t), mix(2 * t + 1)], axis=1)
            yd_ref[rows, lanes] = (jnp.dot(lhs, rhs, preferred_element_type=F32)
                                   + dsk_ref[:, lanes] * xf)
            xs_f.append((xf * w_exp[:, t * LANES:(t + 1) * LANES]).astype(BF16))
            xs_b.append((xf * w_exp[:, gw + t * LANES:gw + (t + 1) * LANES]).astype(BF16))
        contrib_ref[c] = lax.dot_general(bmat, jnp.concatenate(xs_f + xs_b, axis=1),
                                         (((0,), (0,)), ((), ())), preferred_element_type=F32)
        dec_ref[c] = _expand_rows(jnp.exp(blk_t[0:1, HP:2 * HP]), 2 * n_tiles, lane_lo)
        return carry

    lax.fori_loop(0, nc, phase_a, 0, unroll=2)

    sb_ref[...] = h0b_ref[...]

    def phase_b(k, carry):
        c = nc - 1 - k
        sb = sb_ref[...]
        sball_ref[c] = sb.astype(BF16)
        sb_ref[...] = sb * dec_ref[c][:, gw:] + contrib_ref[c, :, gw:]
        return carry

    lax.fori_loop(0, nc, phase_b, 0, unroll=2)
    hb_ref[...] = sb_ref[...]

    sf_ref[...] = h0f_ref[...]

    def phase_c(c, carry):
        rows = chunk_rows(c)
        sf = sf_ref[...]
        s_cat = jnp.concatenate([sf.astype(BF16), sball_ref[c]], axis=1)
        cs = jnp.dot(c_ref[rows, :], s_cat, preferred_element_type=F32)
        off = jnp.dot(tb_ref[c], expand_off, preferred_element_type=F32)
        g_tiles = []
        ssq = jnp.zeros((CHUNK, 1), F32)
        for t in range(n_tiles):
            lanes = slice(t * LANES, (t + 1) * LANES)
            blanes = slice(gw + t * LANES, gw + (t + 1) * LANES)
            y = yd_ref[rows, lanes] + cs[:, lanes] * off[:, lanes] + cs[:, blanes] * off[:, blanes]
            g = y * _silu(z_ref[rows, lanes].astype(F32))
            ssq = ssq + jnp.sum(g * g, axis=-1, keepdims=True)
            g_tiles.append(g)
        inv = lax.rsqrt(ssq * (1.0 / gw) + EPS)
        for t in range(n_tiles):
            lanes = slice(t * LANES, (t + 1) * LANES)
            y_ref[rows, lanes] = ((g_tiles[t] * inv) * nw_ref[:, lanes]).astype(BF16)
        sf_ref[...] = sf * dec_ref[c][:, :gw] + contrib_ref[c, :, :gw]
        return carry

    lax.fori_loop(0, nc, phase_c, 0, unroll=2)
    hf_ref[...] = sf_ref[...]


def _ssd(xbc, zs, dt, p_col, d_exp, norm_w, h0f, h0b, d_ssd):
    bsz, length, _ = xbc.shape
    g = N_BC_GROUPS
    gw = d_ssd // g
    hpg = gw // HEAD_DIM
    nc = length // CHUNK
    bc_first = d_ssd // D_STATE
    pad = HP - 2 * hpg
    dtg = dt.reshape(bsz, nc, CHUNK, 2, g, hpg).transpose(0, 4, 1, 3, 5, 2)
    dtr = jnp.pad(dtg.reshape(bsz, g, nc, 2 * hpg, CHUNK), ((0, 0),) * 3 + ((0, pad), (0, 0)))
    dtr = dtr.reshape(bsz, g, nc * HP, CHUNK)
    p_col = jnp.pad(p_col, ((0, 0), (0, 0), (0, pad)))
    p_row = jnp.tile(p_col.transpose(0, 2, 1), (1, nc, 1))
    st_shape = (bsz, g, D_STATE, gw)
    st_spec = pl.BlockSpec((None, None, D_STATE, gw), lambda b, j: (b, j, 0, 0))
    return pl.pallas_call(
        functools.partial(_ssd_kernel, nc=nc, hpg=hpg),
        out_shape=[jax.ShapeDtypeStruct((bsz, length, d_ssd), BF16),
                   jax.ShapeDtypeStruct(st_shape, F32),
                   jax.ShapeDtypeStruct(st_shape, F32)],
        grid=(bsz, g),
        in_specs=[
            pl.BlockSpec((None, length, gw), lambda b, j: (b, 0, j)),
            pl.BlockSpec((None, length, D_STATE), lambda b, j: (b, 0, bc_first + j)),
            pl.BlockSpec((None, length, D_STATE), lambda b, j: (b, 0, bc_first + g + j)),
            pl.BlockSpec((None, length, gw), lambda b, j: (b, 0, j)),
            pl.BlockSpec((None, None, nc * HP, CHUNK), lambda b, j: (b, j, 0, 0)),
            pl.BlockSpec((None, nc * HP, 2), lambda b, j: (j, 0, 0)),
            pl.BlockSpec((1, gw), lambda b, j: (0, j)),
            pl.BlockSpec((1, gw), lambda b, j: (0, j)),
            st_spec,
            st_spec,
        ],
        out_specs=[pl.BlockSpec((None, length, gw), lambda b, j: (b, 0, j)), st_spec, st_spec],
        scratch_shapes=[pltpu.VMEM((3, nc * HP, CHUNK), F32),
                        pltpu.VMEM((nc, CHUNK, CHUNK), F32),
                        pltpu.VMEM((nc, CHUNK, CHUNK), BF16),
                        pltpu.VMEM((length, gw), F32),
                        pltpu.VMEM((nc, D_STATE, 2 * gw), F32),
                        pltpu.VMEM((nc, 1, 2 * gw), F32),
                        pltpu.VMEM((nc, D_STATE, gw), BF16),
                        pltpu.VMEM((D_STATE, gw), F32),
                        pltpu.VMEM((D_STATE, gw), F32)],
        compiler_params=_cparams("parallel", "parallel"),
        name="ssd",
    )(xbc, xbc, xbc, zs, dtr, p_row, d_exp, norm_w, h0f, h0b)


def _outproj_kernel(four_ref, ssd_ref, x_ref, gate_ref, w_ref, fnw_ref, o_ref, *, final_norm):
    d_f = four_ref.shape[1]
    acc = jnp.dot(four_ref[...], w_ref[0:d_f, :], preferred_element_type=F32)
    acc = acc + jnp.dot(ssd_ref[...], w_ref[d_f:, :], preferred_element_type=F32)
    y = x_ref[...] + gate_ref[...] * acc
    if final_norm:
        ms = jnp.mean(y * y, axis=-1, keepdims=True)
        y = (y * lax.rsqrt(ms + EPS)) * fnw_ref[...]
    o_ref[...] = y


def _outproj(four, ssd, x, gate, w_out, final_norm_w, final_norm):
    bsz, length, d = x.shape
    tm = min(length, 512)
    row_map = lambda b, t: (b, t, 0)
    gate_map = (lambda b, t: (b, 0, 0)) if gate.shape[0] > 1 else (lambda b, t: (0, 0, 0))
    return pl.pallas_call(
        functools.partial(_outproj_kernel, final_norm=final_norm),
        out_shape=jax.ShapeDtypeStruct((bsz, length, d), F32),
        grid=(bsz, length // tm),
        in_specs=[
            pl.BlockSpec((None, tm, four.shape[2]), row_map),
            pl.BlockSpec((None, tm, ssd.shape[2]), row_map),
            pl.BlockSpec((None, tm, d), row_map),
            pl.BlockSpec((None, 1, d), gate_map),
            _resident(w_out.shape),
            _resident((1, d)),
        ],
        out_specs=pl.BlockSpec((None, tm, d), row_map),
        compiler_params=_cparams("parallel", "parallel"),
        name="outproj",
    )(four, ssd, x, gate, w_out, final_norm_w)


def kernel(x, c, ctx, c_ctx, norm_w, w_ada, b_ada, w_in, conv_w, conv_b, dt_bias, a_log, d_skip,
           ssd_norm_w, w_fourier, b_fourier, w_out, final_norm_w):
    bsz, seq_len, d = x.shape
    ctx_len = ctx.shape[1]
    depth = w_in.shape[0]
    d_f = w_fourier.shape[1]
    d_ssd = ssd_norm_w.shape[1]
    conv_ch = conv_w.shape[-1]
    n_heads = d_skip.shape[1]
    g = N_BC_GROUPS
    hpg = n_heads // g
    splits = (d_f, d_f, d_ssd, conv_ch)
    n_main = sum(splits)

    pad_rows = (-(bsz + 1)) % 8
    cc = jnp.concatenate([c, c_ctx[None, :], jnp.zeros((pad_rows, d), F32)], axis=0)
    mod = _adaln(cc, w_ada, b_ada)

    w_main = w_in[:, :, :n_main].astype(BF16)
    w_dt = w_in[:, :, n_main:].astype(BF16)
    w_out_b = w_out.astype(BF16)
    w_f_b = w_fourier.astype(BF16)
    w9 = conv_w.reshape(depth, 9, conv_ch)
    par = jnp.stack([dt_bias, a_log], axis=1)
    p_col = par.reshape(depth, 2, 2, g, hpg).transpose(0, 3, 1, 2, 4).reshape(depth, g, 2, 2 * hpg)
    d_exp = jnp.repeat(d_skip, HEAD_DIM, axis=1).reshape(depth, 1, d_ssd)

    fseq_x, fch = _dft_tables(seq_len, d_f)
    fseq_c, _ = _dft_tables(ctx_len, d_f)
    zeros_state = jnp.zeros((bsz, g, D_STATE, d_ssd // g), F32)
    rows = seq_len // GRID_W
    fnw = final_norm_w.reshape(1, d)

    def mixer(tokens, shift, scale, i, grid_rows, grid_cols, fseq, h0f, h0b, need_out):
        u, zf, zs, xbc, dt = _inproj(tokens, shift, scale, norm_w[i].reshape(1, d),
                                     w_main[i], w_dt[i], splits)
        xbc = _conv(xbc, w9[i], conv_b[i].reshape(1, conv_ch), grid_rows, grid_cols)
        ssd, hf, hb = _ssd(xbc, zs, dt, p_col[i], d_exp[i],
                           ssd_norm_w[i].reshape(1, d_ssd), h0f, h0b, d_ssd)
        four = None
        if need_out:
            four = _fourier(u, zf, fseq, fch, w_f_b[i], b_fourier[i].reshape(1, d_f))
        return four, ssd, hf, hb

    for i in range(depth):
        last = i == depth - 1
        m = mod[i]
        shift, scale, gate = (m[:bsz, None, k * d:(k + 1) * d] for k in range(3))
        shift_c, scale_c, gate_c = (m[bsz:bsz + 1, None, k * d:(k + 1) * d] for k in range(3))
        four_c, ssd_c, hf, hb = mixer(ctx, shift_c, scale_c, i, 1, ctx_len, fseq_c,
                                      zeros_state, zeros_state, not last)
        four_x, ssd_x, _, _ = mixer(x, shift, scale, i, rows, GRID_W, fseq_x, hf, hb, True)
        x = _outproj(four_x, ssd_x, x, gate, w_out_b[i], fnw, last)
        if not last:
            ctx = _outproj(four_c, ssd_c, ctx, gate_c, w_out_b[i], fnw, False)
    return x
```

```python
import functools
import math
from typing import NamedTuple

import jax
import jax.numpy as jnp
import numpy as np
from jax import lax
from jax.experimental import pallas as pl
from jax.experimental.pallas import tpu as pltpu

F32 = jnp.float32
BF16 = jnp.bfloat16

GRID_W = 64
FOURIER_GROUP = 64
HEAD_DIM = 64
N_BC_GROUPS = 4
D_STATE = 128
CHUNK = 128
EPS = 1e-6
LOG2E = 1.4426950408889634

LANES = 128
BF16_SUBLANES = 16
MXU_WIDTH = 256
VMEM_LIMIT_BYTES = 56 * 1024 * 1024

HEADS_PER_TILE = LANES // HEAD_DIM


def _cparams(*sem):
    return pltpu.CompilerParams(dimension_semantics=sem, vmem_limit_bytes=VMEM_LIMIT_BYTES)


def _silu(v):
    return v * (1.0 / (1.0 + jnp.exp2(v * (-LOG2E))))


def _softplus(v):
    return jnp.maximum(v, 0.0) + jnp.log1p(jnp.exp(-jnp.abs(v)))


def _resident(shape):
    zeros = (0,) * len(shape)
    return pl.BlockSpec(shape, lambda *_: zeros, pipeline_mode=pl.Buffered(1))


def _adaln_kernel(c_ref, w_ref, b_ref, o_ref):
    s = _silu(c_ref[...])
    o_ref[...] = jnp.dot(s, w_ref[...], preferred_element_type=F32,
                         precision=lax.Precision.HIGHEST) + b_ref[...]


def _adaln(cc, w_ada, b_ada):
    depth, d, d3 = w_ada.shape
    rows = cc.shape[0]
    tn = 512
    return pl.pallas_call(
        _adaln_kernel,
        out_shape=jax.ShapeDtypeStruct((depth, rows, d3), F32),
        grid=(depth, d3 // tn),
        in_specs=[
            pl.BlockSpec((rows, d), lambda i, j: (0, 0)),
            pl.BlockSpec((None, d, tn), lambda i, j: (i, 0, j)),
            pl.BlockSpec((None, 1, tn), lambda i, j: (i, 0, j)),
        ],
        out_specs=pl.BlockSpec((None, rows, tn), lambda i, j: (i, 0, j)),
        compiler_params=_cparams("parallel", "parallel"),
        name="adaln",
    )(cc, w_ada, b_ada.reshape(depth, 1, d3))


def _project(x, sh_ref, sc_ref, nw_ref, w_ref, out_refs, dt_ref, splits, col_tile):
    ms = jnp.mean(x * x, axis=-1, keepdims=True)
    h = (x * lax.rsqrt(ms + EPS)) * nw_ref[...]
    h = h * (1.0 + sc_ref[...]) + sh_ref[...]
    hb = h.astype(BF16)
    col = 0
    for out_ref, width in zip(out_refs, splits):
        for c0 in range(0, width, col_tile):
            out_ref[:, c0:c0 + col_tile] = jnp.dot(
                hb, w_ref[:, col + c0:col + c0 + col_tile],
                preferred_element_type=F32).astype(BF16)
        col += width
    dt_ref[...] = jnp.dot(hb, w_ref[:, col:], preferred_element_type=F32)


def _residual(four_ref, ssd_ref, x_ref, gate_ref, wo_ref):
    d_f = four_ref.shape[1]
    acc = jnp.dot(four_ref[...], wo_ref[0:d_f, :], preferred_element_type=F32)
    acc = acc + jnp.dot(ssd_ref[...], wo_ref[d_f:, :], preferred_element_type=F32)
    return x_ref[...] + gate_ref[...] * acc


def _inproj_kernel(x_ref, sh_ref, sc_ref, nw_ref, w_ref,
                   u_ref, zf_ref, zs_ref, xbc_ref, dt_ref, *, splits, col_tile):
    _project(x_ref[...], sh_ref, sc_ref, nw_ref, w_ref, (u_ref, zf_ref, zs_ref, xbc_ref), dt_ref,
             splits, col_tile)


def _outproj_inproj_kernel(four_ref, ssd_ref, x_ref, gate_ref, wo_ref, sh_ref, sc_ref, nw_ref, w_ref,
                           xo_ref, u_ref, zf_ref, zs_ref, xbc_ref, dt_ref, *, splits, col_tile):
    x_new = _residual(four_ref, ssd_ref, x_ref, gate_ref, wo_ref)
    xo_ref[...] = x_new
    _project(x_new, sh_ref, sc_ref, nw_ref, w_ref, (u_ref, zf_ref, zs_ref, xbc_ref), dt_ref,
             splits, col_tile)


def _inproj(x, shift, scale, norm_w, w_in, splits, prev=None):
    bsz, length, d = x.shape
    tm = min(length, 512)
    n_dt = w_in.shape[1] - sum(splits)
    row_map = lambda b, t: (b, t, 0)

    def per_batch(arr):
        return pl.BlockSpec((None, 1, d), (lambda b, t: (b, 0, 0)) if arr.shape[0] > 1
                            else (lambda b, t: (0, 0, 0)))

    out_shape = [jax.ShapeDtypeStruct((bsz, length, w), BF16) for w in splits]
    out_shape.append(jax.ShapeDtypeStruct((bsz, length, n_dt), F32))
    out_specs = [pl.BlockSpec((None, tm, w), row_map) for w in splits]
    out_specs.append(pl.BlockSpec((None, tm, n_dt), row_map))
    in_specs = [pl.BlockSpec((None, tm, d), row_map), per_batch(shift), per_batch(scale),
                _resident((1, d)), _resident(w_in.shape)]
    args = [x, shift, scale, norm_w, w_in]
    body = _inproj_kernel
    if prev is not None:
        four, ssd, gate, w_out = prev
        in_specs = [pl.BlockSpec((None, tm, four.shape[2]), row_map),
                    pl.BlockSpec((None, tm, ssd.shape[2]), row_map),
                    in_specs[0], per_batch(gate), _resident(w_out.shape)] + in_specs[1:]
        args = [four, ssd, x, gate, w_out] + args[1:]
        out_shape = [jax.ShapeDtypeStruct((bsz, length, d), F32)] + out_shape
        out_specs = [pl.BlockSpec((None, tm, d), row_map)] + out_specs
        body = _outproj_inproj_kernel
    return pl.pallas_call(
        functools.partial(body, splits=splits, col_tile=512),
        out_shape=out_shape,
        grid=(bsz, length // tm),
        in_specs=in_specs,
        out_specs=out_specs,
        compiler_params=_cparams("parallel", "parallel"),
        name="inproj" if prev is None else "outproj_inproj",
    )(*args)


def _fourier_kernel(u_ref, zf_ref, fseq_ref, fch_ref, wf_ref, bf_ref, o_ref, urev_ref, ucs_ref, mid_ref):
    length, width = u_ref.shape
    half = length // 2
    tr = o_ref.shape[0]
    blk = LANES

    @pl.when(pl.program_id(1) == 0)
    def _():
        ri = lax.broadcasted_iota(jnp.int32, (blk, 2 * blk), 0)
        ci = lax.broadcasted_iota(jnp.int32, (blk, 2 * blk), 1)
        rev = ((ri + ci == blk) | ((ri == 0) & (ci == blk))).astype(BF16)
        nb = half // blk
        for i in range(nb):
            lo = u_ref[length - (i + 1) * blk:length - i * blk, :]
            hi = (u_ref[length - i * blk:length - (i - 1) * blk, :] if i > 0
                  else jnp.zeros((blk, width), BF16))
            urev_ref[i * blk:(i + 1) * blk, :] = jnp.dot(
                rev, jnp.concatenate([lo, hi], axis=0), preferred_element_type=F32).astype(BF16)
        for c0 in range(0, width, MXU_WIDTH):
            cols = slice(c0, c0 + MXU_WIDTH)
            uh = u_ref[0:half, cols]
            ur = urev_ref[:, cols]
            cos_ch = fch_ref[cols, cols]
            sin_ch = fch_ref[cols, width + c0:width + c0 + MXU_WIDTH]
            ucs_ref[0:half, cols] = (jnp.dot(uh, cos_ch, preferred_element_type=F32)
                                     + jnp.dot(ur, cos_ch, preferred_element_type=F32)).astype(BF16)
            ucs_ref[half:length, cols] = (jnp.dot(uh, sin_ch, preferred_element_type=F32)
                                          - jnp.dot(ur, sin_ch, preferred_element_type=F32)).astype(BF16)
            mid_ref[:, cols] = jnp.dot(u_ref[half:half + BF16_SUBLANES, cols], cos_ch,
                                       preferred_element_type=F32) * (1.0 / math.sqrt(length))

    mixed = jnp.dot(fseq_ref[...], ucs_ref[...], preferred_element_type=F32)
    k = pl.program_id(1) * tr + lax.broadcasted_iota(jnp.int32, (tr, 1), 0)
    sign = (1 - 2 * (k & 1)).astype(F32)
    mixed = mixed + sign * mid_ref[0:1, :]
    lin = jnp.dot(mixed.astype(BF16), wf_ref[...], preferred_element_type=F32) + bf_ref[...]
    o_ref[...] = (lin * _silu(zf_ref[...].astype(F32))).astype(BF16)


def _fourier(u, zf, fseq, fch, w_f, b_f):
    bsz, length, width = u.shape
    tr = min(length, 512)
    half = length // 2
    assert half % LANES == 0, "the block-wise reversal needs L/2 to be a multiple of 128"
    assert width % MXU_WIDTH == 0 and MXU_WIDTH % FOURIER_GROUP == 0
    return pl.pallas_call(
        _fourier_kernel,
        out_shape=jax.ShapeDtypeStruct((bsz, length, width), BF16),
        grid=(bsz, length // tr),
        in_specs=[
            pl.BlockSpec((None, length, width), lambda b, t: (b, 0, 0)),
            pl.BlockSpec((None, tr, width), lambda b, t: (b, t, 0)),
            pl.BlockSpec((tr, length), lambda b, t: (t, 0)),
            _resident(fch.shape),
            _resident(w_f.shape),
            _resident(b_f.shape),
        ],
        out_specs=pl.BlockSpec((None, tr, width), lambda b, t: (b, t, 0)),
        scratch_shapes=[pltpu.VMEM((half, width), BF16),
                        pltpu.VMEM((length, width), BF16),
                        pltpu.VMEM((BF16_SUBLANES, width), F32)],
        compiler_params=_cparams("parallel", "arbitrary"),
        name="fourier",
    )(u, zf, fseq, fch, w_f, b_f)


@functools.lru_cache(maxsize=None)
def _dft_tables(length, width):
    half = length // 2
    kl = (np.arange(length)[:, None] * np.arange(half)[None, :]) % length
    ang = kl * (2.0 * np.pi / length)
    fseq = np.concatenate([np.cos(ang), -np.sin(ang)], axis=1) / np.sqrt(length)
    ch = np.arange(width)
    same = (ch[:, None] // FOURIER_GROUP) == (ch[None, :] // FOURIER_GROUP)
    angc = (((ch[:, None] % FOURIER_GROUP) * (ch[None, :] % FOURIER_GROUP)) % FOURIER_GROUP) * (
        2.0 * np.pi / FOURIER_GROUP)
    fch = np.concatenate([np.where(same, np.cos(angc), 0.0),
                          np.where(same, np.sin(angc), 0.0)], axis=1) / np.sqrt(FOURIER_GROUP)
    return fseq.astype(np.float32), fch.astype(np.float32)


HP = BF16_SUBLANES
CONV_SUB = 64


def _split2(v):
    hi = v.astype(BF16)
    return hi, (v - hi.astype(F32)).astype(BF16)


def _split3(v):
    hi, _ = _split2(v)
    r1 = v - hi.astype(F32)
    mid, lo = _split2(r1)
    return hi, mid, lo


def _expand_rows(cols, n_tiles, lane_lo):
    tiles = [jnp.where(lane_lo, cols[:, 2 * t:2 * t + 1], cols[:, 2 * t + 1:2 * t + 2])
             for t in range(n_tiles)]
    return jnp.concatenate(tiles, axis=1)


def _ssd_kernel(*refs, segments, hpg):
    n_seg = len(segments)
    seg_refs = [refs[6 * i:6 * i + 6] for i in range(n_seg)]
    pos = 6 * n_seg
    cwx_ref, cwb_ref, cwc_ref, cbx_ref, cbb_ref, cbc_ref, dsk_ref, nw_ref = refs[pos:pos + 8]
    pos += 8
    y_refs = []
    for seg in segments:
        y_refs.append(refs[pos] if seg.emit_y else None)
        pos += 1 if seg.emit_y else 0
    (sm_ref, s0_ref, sp_ref, cc_ref, rowq_ref, tblk_ref, tb_ref, yd_ref, contrib_ref, dec_ref,
     sball_ref, sf_ref, sb_ref) = refs[pos:]

    gw = hpg * HEAD_DIM
    n_tiles = gw // LANES
    conv_w = s0_ref.shape[1]
    shifted = (sm_ref, s0_ref, sp_ref)
    ri = lax.broadcasted_iota(jnp.int32, (CHUNK, CHUNK), 0)
    ci = lax.broadcasted_iota(jnp.int32, (CHUNK, CHUNK), 1)
    tri_t = (ri <= ci).astype(BF16)
    below = ci < ri
    on_diag = ci == ri
    lane_lo = lax.broadcasted_iota(jnp.int32, (1, LANES), 1) < HEAD_DIM
    erow = lax.broadcasted_iota(jnp.int32, (CHUNK, 2 * gw), 0)
    ehead = lax.broadcasted_iota(jnp.int32, (CHUNK, 2 * gw), 1) // HEAD_DIM
    expand_w = (erow == ehead + 2 * HP).astype(BF16)
    expand_off = (erow == ehead + 3 * HP).astype(BF16)

    def chunk_rows(c):
        return pl.ds(pl.multiple_of(c * CHUNK, CHUNK), CHUNK)

    def run_segment(x_ref, b_ref, c_ref, z_ref, dtr_ref, pr_ref, y_ref, seg):
        length, grid_rows, grid_cols = seg.length, seg.grid_rows, seg.grid_cols
        nc = length // CHUNK
        nrow = nc * HP

        halo = grid_cols if grid_rows > 1 else 0
        if halo:
            zeros = jnp.zeros((halo, conv_w), F32)
            for s_ref in shifted:
                s_ref[0:halo, :] = zeros
                s_ref[halo + length:2 * halo + length, :] = zeros
        wpos = lax.broadcasted_iota(jnp.int32, (grid_cols, LANES), 0)
        sources = ((x_ref, 0, n_tiles), (b_ref, gw, 1), (c_ref, gw + D_STATE, 1))

        def shift_row(r, carry):
            src_rows = pl.ds(pl.multiple_of(r * grid_cols, grid_cols), grid_cols)
            dst_rows = pl.ds(pl.multiple_of(halo + r * grid_cols, grid_cols), grid_cols)
            for ref, lane0, tiles in sources:
                for t in range(tiles):
                    xr = ref[src_rows, t * LANES:(t + 1) * LANES].astype(F32)
                    dst = slice(lane0 + t * LANES, lane0 + (t + 1) * LANES)
                    s0_ref[dst_rows, dst] = xr
                    sm_ref[dst_rows, dst] = jnp.where(wpos == 0, 0.0, pltpu.roll(xr, 1, axis=0))
                    sp_ref[dst_rows, dst] = jnp.where(wpos == grid_cols - 1, 0.0,
                                                      pltpu.roll(xr, grid_cols - 1, axis=0))
            return carry

        lax.fori_loop(0, grid_rows, shift_row, 0, unroll=min(grid_rows, 2))
        dhs = (-1, 0, 1) if grid_rows > 1 else (0,)

        def conv_tile(c, lane0, w_ref, b_ref_, t):
            wl = slice(t * LANES, (t + 1) * LANES)
            sl = slice(lane0 + t * LANES, lane0 + (t + 1) * LANES)
            bias = b_ref_[:, wl]
            outs = []
            for s in range(CHUNK // CONV_SUB):
                base = pl.multiple_of(c * CHUNK, CHUNK) + s * CONV_SUB
                acc = bias
                for dh in dhs:
                    off = halo + base + dh * grid_cols
                    for kw, s_ref in enumerate(shifted):
                        k = 3 * (dh + 1) + kw
                        acc = acc + w_ref[k:k + 1, wl] * s_ref[pl.ds(off, CONV_SUB), sl]
                outs.append(_silu(acc))
            return jnp.concatenate(outs, axis=0)

        row_fwd = (lax.broadcasted_iota(jnp.int32, (nrow, 1), 0) & (HP - 1)) < hpg
        dt_r = _softplus(dtr_ref[...] + pr_ref[:, 0:1])
        a_r = dt_r * (-jnp.exp(pr_ref[:, 1:2]))
        cum_r = sum(jnp.dot(p, tri_t, preferred_element_type=F32) for p in _split3(a_r))
        excl_r = cum_r - a_r
        tot_r = jnp.broadcast_to(cum_r[:, CHUNK - 1:CHUNK], cum_r.shape)
        log_dt = jnp.log(dt_r)
        rowq_ref[0, 0:nrow, :] = dt_r
        rowq_ref[1, 0:nrow, :] = cum_r - log_dt
        rowq_ref[2, 0:nrow, :] = excl_r + log_dt
        to_cols = (
            jnp.where(row_fwd, cum_r, excl_r),
            tot_r,
            jnp.where(row_fwd, jnp.exp(tot_r - cum_r), jnp.exp(excl_r)) * dt_r,
            jnp.where(row_fwd, jnp.exp(cum_r), jnp.exp(tot_r - excl_r)),
        )
        for c in range(nc):
            for q, val in enumerate(to_cols):
                tblk_ref[c, q * HP:(q + 1) * HP, :] = val[c * HP:(c + 1) * HP, :]
            tblk_ref[c, len(to_cols) * HP:, :] = jnp.zeros((CHUNK - len(to_cols) * HP, CHUNK), F32)

        def phase_a(c, carry):
            rows = chunk_rows(c)
            rrows = pl.ds(pl.multiple_of(c * HP, HP), HP)
            blk_t = tblk_ref[c].T
            blk_tb = blk_t.astype(BF16)
            tb_ref[c] = blk_tb
            dt_row = rowq_ref[0, rrows, :]
            cf_row = rowq_ref[1, rrows, :]
            eb_row = rowq_ref[2, rrows, :]

            bmat = conv_tile(c, gw, cwb_ref, cbb_ref, 0).astype(BF16)
            cmat = conv_tile(c, gw + D_STATE, cwc_ref, cbc_ref, 0).astype(BF16)
            cc_ref[rows, :] = cmat
            cb = lax.dot_general(cmat, bmat, (((1,), (1,)), ((), ())), preferred_element_type=F32)

            def mix(h):
                arg = jnp.where(below, blk_t[:, h:h + 1] - cf_row[h:h + 1, :],
                                eb_row[hpg + h:hpg + h + 1, :] - blk_t[:, hpg + h:hpg + h + 1])
                e = jnp.exp(arg) + jnp.where(on_diag, dt_row[h:h + 1, :], 0.0)
                return (cb * e).astype(BF16)

            w_exp = jnp.dot(blk_tb, expand_w, preferred_element_type=F32)

            xs_f, xs_b = [], []
            for t in range(n_tiles):
                lanes = slice(t * LANES, (t + 1) * LANES)
                xf = conv_tile(c, 0, cwx_ref, cbx_ref, t)
                xt = xf.astype(BF16)
                zero = jnp.zeros_like(xt)
                rhs = jnp.concatenate([jnp.where(lane_lo, xt, zero), jnp.where(lane_lo, zero, xt)],
                                      axis=0)
                lhs = jnp.concatenate([mix(2 * t), mix(2 * t + 1)], axis=1)
                yd_ref[rows, lanes] = (jnp.dot(lhs, rhs, preferred_element_type=F32)
                                       + dsk_ref[:, lanes] * xf)
                xs_f.append((xf * w_exp[:, t * LANES:(t + 1) * LANES]).astype(BF16))
                xs_b.append((xf * w_exp[:, gw + t * LANES:gw + (t + 1) * LANES]).astype(BF16))
            contrib_ref[c] = lax.dot_general(bmat, jnp.concatenate(xs_f + xs_b, axis=1),
                                             (((0,), (0,)), ((), ())), preferred_element_type=F32)
            dec_ref[c] = _expand_rows(jnp.exp(blk_t[0:1, HP:2 * HP]), 2 * n_tiles, lane_lo)
            return carry

        lax.fori_loop(0, nc, phase_a, 0, unroll=min(nc, 4))

        def phase_b(k, carry):
            c = nc - 1 - k
            sb = sb_ref[...]
            sball_ref[c] = sb.astype(BF16)
            sb_ref[...] = sb * dec_ref[c][:, gw:] + contrib_ref[c, :, gw:]
            return carry

        lax.fori_loop(0, nc, phase_b, 0, unroll=min(nc, 4))

        def phase_c(c, carry):
            rows = chunk_rows(c)
            sf = sf_ref[...]
            if y_ref is not None:
                s_cat = jnp.concatenate([sf.astype(BF16), sball_ref[c]], axis=1)
                cs = jnp.dot(cc_ref[rows, :], s_cat, preferred_element_type=F32)
                off = jnp.dot(tb_ref[c], expand_off, preferred_element_type=F32)
                g_tiles = []
                ssq = jnp.zeros((CHUNK, 1), F32)
                for t in range(n_tiles):
                    lanes = slice(t * LANES, (t + 1) * LANES)
                    blanes = slice(gw + t * LANES, gw + (t + 1) * LANES)
                    y = (yd_ref[rows, lanes] + cs[:, lanes] * off[:, lanes]
                         + cs[:, blanes] * off[:, blanes])
                    g = y * _silu(z_ref[rows, lanes].astype(F32))
                    ssq = ssq + jnp.sum(g * g, axis=-1, keepdims=True)
                    g_tiles.append(g)
                inv = lax.rsqrt(ssq * (1.0 / gw) + EPS)
                for t in range(n_tiles):
                    lanes = slice(t * LANES, (t + 1) * LANES)
                    y_ref[rows, lanes] = ((g_tiles[t] * inv) * nw_ref[:, lanes]).astype(BF16)
            sf_ref[...] = sf * dec_ref[c][:, :gw] + contrib_ref[c, :, :gw]
            return carry

        lax.fori_loop(0, nc, phase_c, 0, unroll=min(nc, 8))

    sf_ref[...] = jnp.zeros(sf_ref.shape, F32)
    sb_ref[...] = jnp.zeros(sb_ref.shape, F32)
    for (x_ref, b_ref, c_ref, z_ref, dtr_ref, pr_ref), y_ref, seg in zip(seg_refs, y_refs, segments):
        run_segment(x_ref, b_ref, c_ref, z_ref, dtr_ref, pr_ref, y_ref, seg)


class _Segment(NamedTuple):
    length: int
    grid_rows: int
    grid_cols: int
    emit_y: bool


def _ssd(token_sets, p_col, conv_w9, conv_b, d_exp, norm_w, d_ssd):
    g = N_BC_GROUPS
    gw = d_ssd // g
    hpg = gw // HEAD_DIM
    bc_first = d_ssd // D_STATE
    pad = HP - 2 * hpg
    p_col = jnp.pad(p_col, ((0, 0), (0, 0), (0, pad)))
    bsz = token_sets[0][0].shape[0]
    segments, args, in_specs, out_shape, out_specs = [], [], [], [], []
    for xbc, zs, dt, grid_rows, grid_cols, emit_y in token_sets:
        length = xbc.shape[1]
        nc = length // CHUNK
        segments.append(_Segment(length, grid_rows, grid_cols, emit_y))
        dtg = dt.reshape(bsz, nc, CHUNK, 2, g, hpg).transpose(0, 4, 1, 3, 5, 2)
        dtr = jnp.pad(dtg.reshape(bsz, g, nc, 2 * hpg, CHUNK), ((0, 0),) * 3 + ((0, pad), (0, 0)))
        dtr = dtr.reshape(bsz, g, nc * HP, CHUNK)
        p_row = jnp.tile(p_col.transpose(0, 2, 1), (1, nc, 1))
        args += [xbc, xbc, xbc, zs, dtr, p_row]
        in_specs += [
            pl.BlockSpec((None, length, gw), lambda b, j: (b, 0, j)),
            pl.BlockSpec((None, length, D_STATE), lambda b, j: (b, 0, bc_first + j)),
            pl.BlockSpec((None, length, D_STATE), lambda b, j: (b, 0, bc_first + g + j)),
            pl.BlockSpec((None, length, gw), lambda b, j: (b, 0, j)),
            pl.BlockSpec((None, None, nc * HP, CHUNK), lambda b, j: (b, j, 0, 0)),
            pl.BlockSpec((None, nc * HP, 2), lambda b, j: (j, 0, 0)),
        ]
        if emit_y:
            out_shape.append(jax.ShapeDtypeStruct((bsz, length, d_ssd), BF16))
            out_specs.append(pl.BlockSpec((None, length, gw), lambda b, j: (b, 0, j)))
    args += [conv_w9, conv_w9, conv_w9, conv_b, conv_b, conv_b, d_exp, norm_w]
    in_specs += [
        pl.BlockSpec((9, gw), lambda b, j: (0, j)),
        pl.BlockSpec((9, D_STATE), lambda b, j: (0, bc_first + j)),
        pl.BlockSpec((9, D_STATE), lambda b, j: (0, bc_first + g + j)),
        pl.BlockSpec((1, gw), lambda b, j: (0, j)),
        pl.BlockSpec((1, D_STATE), lambda b, j: (0, bc_first + j)),
        pl.BlockSpec((1, D_STATE), lambda b, j: (0, bc_first + g + j)),
        pl.BlockSpec((1, gw), lambda b, j: (0, j)),
        pl.BlockSpec((1, gw), lambda b, j: (0, j)),
    ]
    max_len = max(seg.length for seg in segments)
    max_nc = max_len // CHUNK
    conv_rows = max(seg.length + (2 * seg.grid_cols if seg.grid_rows > 1 else 0) for seg in segments)
    conv_w = gw + 2 * D_STATE
    outs = pl.pallas_call(
        functools.partial(_ssd_kernel, segments=tuple(segments), hpg=hpg),
        out_shape=out_shape,
        grid=(bsz, g),
        in_specs=in_specs,
        out_specs=out_specs,
        scratch_shapes=[pltpu.VMEM((conv_rows, conv_w), F32),
                        pltpu.VMEM((conv_rows, conv_w), F32),
                        pltpu.VMEM((conv_rows, conv_w), F32),
                        pltpu.VMEM((max_len, D_STATE), BF16),
                        pltpu.VMEM((3, max_nc * HP, CHUNK), F32),
                        pltpu.VMEM((max_nc, CHUNK, CHUNK), F32),
                        pltpu.VMEM((max_nc, CHUNK, CHUNK), BF16),
                        pltpu.VMEM((max_len, gw), F32),
                        pltpu.VMEM((max_nc, D_STATE, 2 * gw), F32),
                        pltpu.VMEM((max_nc, 1, 2 * gw), F32),
                        pltpu.VMEM((max_nc, D_STATE, gw), BF16),
                        pltpu.VMEM((D_STATE, gw), F32),
                        pltpu.VMEM((D_STATE, gw), F32)],
        compiler_params=_cparams("parallel", "parallel"),
        name="ssd",
    )(*args)
    outs = list(outs)
    return [outs.pop(0) if seg.emit_y else None for seg in segments]


def _final_kernel(four_ref, ssd_ref, x_ref, gate_ref, w_ref, fnw_ref, o_ref):
    y = _residual(four_ref, ssd_ref, x_ref, gate_ref, w_ref)
    ms = jnp.mean(y * y, axis=-1, keepdims=True)
    o_ref[...] = (y * lax.rsqrt(ms + EPS)) * fnw_ref[...]


def _final(four, ssd, x, gate, w_out, final_norm_w):
    bsz, length, d = x.shape
    tm = min(length, 512)
    row_map = lambda b, t: (b, t, 0)
    return pl.pallas_call(
        _final_kernel,
        out_shape=jax.ShapeDtypeStruct((bsz, length, d), F32),
        grid=(bsz, length // tm),
        in_specs=[
            pl.BlockSpec((None, tm, four.shape[2]), row_map),
            pl.BlockSpec((None, tm, ssd.shape[2]), row_map),
            pl.BlockSpec((None, tm, d), row_map),
            pl.BlockSpec((None, 1, d), lambda b, t: (b, 0, 0)),
            _resident(w_out.shape),
            _resident((1, d)),
        ],
        out_specs=pl.BlockSpec((None, tm, d), row_map),
        compiler_params=_cparams("parallel", "parallel"),
        name="final",
    )(four, ssd, x, gate, w_out, final_norm_w)


def kernel(x, c, ctx, c_ctx, norm_w, w_ada, b_ada, w_in, conv_w, conv_b, dt_bias, a_log, d_skip,
           ssd_norm_w, w_fourier, b_fourier, w_out, final_norm_w):
    bsz, seq_len, d = x.shape
    ctx_len = ctx.shape[1]
    depth = w_in.shape[0]
    d_f = w_fourier.shape[1]
    d_ssd = ssd_norm_w.shape[1]
    conv_ch = conv_w.shape[-1]
    n_heads = d_skip.shape[1]
    g = N_BC_GROUPS
    hpg = n_heads // g
    splits = (d_f, d_f, d_ssd, conv_ch)

    pad_rows = (-(bsz + 1)) % 8
    cc = jnp.concatenate([c, c_ctx[None, :], jnp.zeros((pad_rows, d), F32)], axis=0)
    mod = _adaln(cc, w_ada, b_ada)

    w_in_b = w_in.astype(BF16)
    w_out_b = w_out.astype(BF16)
    w_f_b = w_fourier.astype(BF16)
    w9 = conv_w.reshape(depth, 9, conv_ch)
    par = jnp.stack([dt_bias, a_log], axis=1)
    p_col = par.reshape(depth, 2, 2, g, hpg).transpose(0, 3, 1, 2, 4).reshape(depth, g, 2, 2 * hpg)
    d_exp = jnp.repeat(d_skip, HEAD_DIM, axis=1).reshape(depth, 1, d_ssd)

    fseq_x, fch = (jnp.asarray(t).astype(BF16) for t in _dft_tables(seq_len, d_f))
    fseq_c = jnp.asarray(_dft_tables(ctx_len, d_f)[0]).astype(BF16)
    rows = seq_len // GRID_W

    prev_c = prev_x = None
    for i in range(depth):
        last = i == depth - 1
        m = mod[i]
        shift, scale, gate = (m[:bsz, None, k * d:(k + 1) * d] for k in range(3))
        shift_c, scale_c, gate_c = (m[bsz:bsz + 1, None, k * d:(k + 1) * d] for k in range(3))
        nw_i = norm_w[i].reshape(1, d)
        outs_c = _inproj(ctx, shift_c, scale_c, nw_i, w_in_b[i], splits, prev_c)
        outs_x = _inproj(x, shift, scale, nw_i, w_in_b[i], splits, prev_x)
        if i > 0:
            ctx, outs_c = outs_c[0], outs_c[1:]
            x, outs_x = outs_x[0], outs_x[1:]
        (u_c, zf_c, zs_c, xbc_c, dt_c), (u_x, zf_x, zs_x, xbc_x, dt_x) = outs_c, outs_x
        ssd_c, ssd_x = _ssd(
            [(xbc_c, zs_c, dt_c, 1, ctx_len, not last), (xbc_x, zs_x, dt_x, rows, GRID_W, True)],
            p_col[i], w9[i], conv_b[i].reshape(1, conv_ch), d_exp[i],
            ssd_norm_w[i].reshape(1, d_ssd), d_ssd)
        b_f = b_fourier[i].reshape(1, d_f)
        four_x = _fourier(u_x, zf_x, fseq_x, fch, w_f_b[i], b_f)
        prev_x = (four_x, ssd_x, gate, w_out_b[i])
        if not last:
            four_c = _fourier(u_c, zf_c, fseq_c, fch, w_f_b[i], b_f)
            prev_c = (four_c, ssd_c, gate_c, w_out_b[i])
    four_x, ssd_x, gate, w_o = prev_x
    return _final(four_x, ssd_x, x, gate, w_o, final_norm_w.reshape(1, d))
```

```python
import functools
import math
from typing import NamedTuple

import jax
import jax.numpy as jnp
import numpy as np
from jax import lax
from jax.experimental import pallas as pl
from jax.experimental.pallas import tpu as pltpu

F32 = jnp.float32
BF16 = jnp.bfloat16

GRID_W = 64
FOURIER_GROUP = 64
HEAD_DIM = 64
N_BC_GROUPS = 4
D_STATE = 128
CHUNK = 128
EPS = 1e-6
LOG2E = 1.4426950408889634

LANES = 128
BF16_SUBLANES = 16
MXU_WIDTH = 256
VMEM_LIMIT_BYTES = 56 * 1024 * 1024

HEADS_PER_TILE = LANES // HEAD_DIM


def _cparams(*sem):
    return pltpu.CompilerParams(dimension_semantics=sem, vmem_limit_bytes=VMEM_LIMIT_BYTES)


def _silu(v):
    return v * (1.0 / (1.0 + jnp.exp2(v * (-LOG2E))))


def _softplus(v):
    return jnp.maximum(v, 0.0) + jnp.log1p(jnp.exp(-jnp.abs(v)))


def _resident(shape, layer=None):
    zeros = (0,) * len(shape)
    if layer is None:
        return pl.BlockSpec(shape, lambda *_: zeros, pipeline_mode=pl.Buffered(1))
    return pl.BlockSpec((None,) + tuple(shape), lambda *_: (layer,) + zeros,
                        pipeline_mode=pl.Buffered(1))


def _adaln_kernel(c_ref, w_ref, b_ref, o_ref):
    s = _silu(c_ref[...])
    o_ref[...] = jnp.dot(s, w_ref[...], preferred_element_type=F32,
                         precision=lax.Precision.HIGHEST) + b_ref[...]


def _adaln(cc, w_ada, b_ada):
    depth, d, d3 = w_ada.shape
    rows = cc.shape[0]
    tn = 512
    return pl.pallas_call(
        _adaln_kernel,
        out_shape=jax.ShapeDtypeStruct((depth, rows, d3), F32),
        grid=(depth, d3 // tn),
        in_specs=[
            pl.BlockSpec((rows, d), lambda i, j: (0, 0)),
            pl.BlockSpec((None, d, tn), lambda i, j: (i, 0, j)),
            pl.BlockSpec((None, 1, tn), lambda i, j: (i, 0, j)),
        ],
        out_specs=pl.BlockSpec((None, rows, tn), lambda i, j: (i, 0, j)),
        compiler_params=_cparams("parallel", "parallel"),
        name="adaln",
    )(cc, w_ada, b_ada.reshape(depth, 1, d3))


def _project(x, sh_ref, sc_ref, nw_ref, w_ref, wdt_ref, out_refs, dt_ref, splits, col_tile):
    ms = jnp.mean(x * x, axis=-1, keepdims=True)
    h = (x * lax.rsqrt(ms + EPS)) * nw_ref[...]
    h = h * (1.0 + sc_ref[...]) + sh_ref[...]
    hb = h.astype(BF16)
    col = 0
    for out_ref, width in zip(out_refs, splits):
        for c0 in range(0, width, col_tile):
            out_ref[:, c0:c0 + col_tile] = jnp.dot(
                hb, w_ref[:, col + c0:col + c0 + col_tile],
                preferred_element_type=F32).astype(BF16)
        col += width
    dt_t = lax.dot_general(wdt_ref[...], hb, (((1,), (1,)), ((), ())), preferred_element_type=F32)
    for k in range(dt_ref.shape[0]):
        dt_ref[k] = dt_t[:, k * CHUNK:(k + 1) * CHUNK]


def _residual(four_ref, ssd_ref, x_ref, gate_ref, wo_ref):
    d_f = four_ref.shape[1]
    acc = jnp.dot(four_ref[...], wo_ref[0:d_f, :], preferred_element_type=F32)
    acc = acc + jnp.dot(ssd_ref[...], wo_ref[d_f:, :], preferred_element_type=F32)
    return x_ref[...] + gate_ref[...] * acc


def _inproj_kernel(x_ref, sh_ref, sc_ref, nw_ref, w_ref, wdt_ref,
                   u_ref, zf_ref, zs_ref, xbc_ref, dt_ref, *, splits, col_tile):
    _project(x_ref[...], sh_ref, sc_ref, nw_ref, w_ref, wdt_ref, (u_ref, zf_ref, zs_ref, xbc_ref),
             dt_ref, splits, col_tile)


def _outproj_inproj_kernel(four_ref, ssd_ref, x_ref, gate_ref, wo_ref, sh_ref, sc_ref, nw_ref, w_ref,
                           wdt_ref, xo_ref, u_ref, zf_ref, zs_ref, xbc_ref, dt_ref, *, splits, col_tile):
    x_new = _residual(four_ref, ssd_ref, x_ref, gate_ref, wo_ref)
    xo_ref[...] = x_new
    _project(x_new, sh_ref, sc_ref, nw_ref, w_ref, wdt_ref, (u_ref, zf_ref, zs_ref, xbc_ref),
             dt_ref, splits, col_tile)


def _inproj(x, shift, scale, norm_w, w_in, w_dt, layer, splits, prev=None):
    bsz, length, d = x.shape
    tm = min(length, 512)
    n_dt = w_dt.shape[1]
    row_map = lambda b, t: (b, t, 0)

    def per_batch(arr):
        return pl.BlockSpec((None, 1, d), (lambda b, t: (b, 0, 0)) if arr.shape[0] > 1
                            else (lambda b, t: (0, 0, 0)))

    out_shape = [jax.ShapeDtypeStruct((bsz, length, w), BF16) for w in splits]
    out_shape.append(jax.ShapeDtypeStruct((bsz, length // CHUNK, n_dt, CHUNK), F32))
    out_specs = [pl.BlockSpec((None, tm, w), row_map) for w in splits]
    out_specs.append(pl.BlockSpec((None, tm // CHUNK, n_dt, CHUNK), lambda b, t: (b, t, 0, 0)))
    in_specs = [pl.BlockSpec((None, tm, d), row_map), per_batch(shift), per_batch(scale),
                _resident((1, d)), _resident(w_in.shape[1:], layer), _resident(w_dt.shape[1:], layer)]
    args = [x, shift, scale, norm_w, w_in, w_dt]
    body = _inproj_kernel
    if prev is not None:
        four, ssd, gate, w_out = prev
        in_specs = [pl.BlockSpec((None, tm, four.shape[2]), row_map),
                    pl.BlockSpec((None, tm, ssd.shape[2]), row_map),
                    in_specs[0], per_batch(gate), _resident(w_out.shape[1:], layer - 1)] + in_specs[1:]
        args = [four, ssd, x, gate, w_out] + args[1:]
        out_shape = [jax.ShapeDtypeStruct((bsz, length, d), F32)] + out_shape
        out_specs = [pl.BlockSpec((None, tm, d), row_map)] + out_specs
        body = _outproj_inproj_kernel
    return pl.pallas_call(
        functools.partial(body, splits=splits, col_tile=512),
        out_shape=out_shape,
        grid=(bsz, length // tm),
        in_specs=in_specs,
        out_specs=out_specs,
        compiler_params=_cparams("parallel", "parallel"),
        name="inproj" if prev is None else "outproj_inproj",
    )(*args)


def _fourier_kernel(u_ref, zf_ref, fseq_ref, fch_ref, wf_ref, bf_ref, o_ref, urev_ref, ucs_ref, mid_ref):
    length, width = u_ref.shape
    half = length // 2
    tr = o_ref.shape[0]
    blk = LANES

    @pl.when(pl.program_id(1) == 0)
    def _():
        ri = lax.broadcasted_iota(jnp.int32, (blk, 2 * blk), 0)
        ci = lax.broadcasted_iota(jnp.int32, (blk, 2 * blk), 1)
        rev = ((ri + ci == blk) | ((ri == 0) & (ci == blk))).astype(BF16)
        nb = half // blk
        for i in range(nb):
            lo = u_ref[length - (i + 1) * blk:length - i * blk, :]
            hi = (u_ref[length - i * blk:length - (i - 1) * blk, :] if i > 0
                  else jnp.zeros((blk, width), BF16))
            urev_ref[i * blk:(i + 1) * blk, :] = jnp.dot(
                rev, jnp.concatenate([lo, hi], axis=0), preferred_element_type=F32).astype(BF16)
        for c0 in range(0, width, MXU_WIDTH):
            cols = slice(c0, c0 + MXU_WIDTH)
            uh = u_ref[0:half, cols]
            ur = urev_ref[:, cols]
            cos_ch = fch_ref[cols, cols]
            sin_ch = fch_ref[cols, width + c0:width + c0 + MXU_WIDTH]
            ucs_ref[0:half, cols] = (jnp.dot(uh, cos_ch, preferred_element_type=F32)
                                     + jnp.dot(ur, cos_ch, preferred_element_type=F32)).astype(BF16)
            ucs_ref[half:length, cols] = (jnp.dot(uh, sin_ch, preferred_element_type=F32)
                                          - jnp.dot(ur, sin_ch, preferred_element_type=F32)).astype(BF16)
            mid_ref[:, cols] = jnp.dot(u_ref[half:half + BF16_SUBLANES, cols], cos_ch,
                                       preferred_element_type=F32) * (1.0 / math.sqrt(length))

    mixed = jnp.dot(fseq_ref[...], ucs_ref[...], preferred_element_type=F32)
    k = pl.program_id(1) * tr + lax.broadcasted_iota(jnp.int32, (tr, 1), 0)
    sign = (1 - 2 * (k & 1)).astype(F32)
    mixed = mixed + sign * mid_ref[0:1, :]
    lin = jnp.dot(mixed.astype(BF16), wf_ref[...], preferred_element_type=F32) + bf_ref[...]
    o_ref[...] = (lin * _silu(zf_ref[...].astype(F32))).astype(BF16)


def _fourier(u, zf, fseq, fch, w_f, layer, b_f):
    bsz, length, width = u.shape
    tr = min(length, 512)
    half = length // 2
    assert half % LANES == 0, "the block-wise reversal needs L/2 to be a multiple of 128"
    assert width % MXU_WIDTH == 0 and MXU_WIDTH % FOURIER_GROUP == 0
    return pl.pallas_call(
        _fourier_kernel,
        out_shape=jax.ShapeDtypeStruct((bsz, length, width), BF16),
        grid=(bsz, length // tr),
        in_specs=[
            pl.BlockSpec((None, length, width), lambda b, t: (b, 0, 0)),
            pl.BlockSpec((None, tr, width), lambda b, t: (b, t, 0)),
            pl.BlockSpec((tr, length), lambda b, t: (t, 0)),
            _resident(fch.shape),
            _resident(w_f.shape[1:], layer),
            _resident(b_f.shape),
        ],
        out_specs=pl.BlockSpec((None, tr, width), lambda b, t: (b, t, 0)),
        scratch_shapes=[pltpu.VMEM((half, width), BF16),
                        pltpu.VMEM((length, width), BF16),
                        pltpu.VMEM((BF16_SUBLANES, width), F32)],
        compiler_params=_cparams("parallel", "arbitrary"),
        name="fourier",
    )(u, zf, fseq, fch, w_f, b_f)


@functools.lru_cache(maxsize=None)
def _dft_tables(length, width):
    half = length // 2
    kl = (np.arange(length)[:, None] * np.arange(half)[None, :]) % length
    ang = kl * (2.0 * np.pi / length)
    fseq = np.concatenate([np.cos(ang), -np.sin(ang)], axis=1) / np.sqrt(length)
    ch = np.arange(width)
    same = (ch[:, None] // FOURIER_GROUP) == (ch[None, :] // FOURIER_GROUP)
    angc = (((ch[:, None] % FOURIER_GROUP) * (ch[None, :] % FOURIER_GROUP)) % FOURIER_GROUP) * (
        2.0 * np.pi / FOURIER_GROUP)
    fch = np.concatenate([np.where(same, np.cos(angc), 0.0),
                          np.where(same, np.sin(angc), 0.0)], axis=1) / np.sqrt(FOURIER_GROUP)
    return fseq.astype(np.float32), fch.astype(np.float32)


HP = BF16_SUBLANES
CONV_SUB = 128


def _split2(v):
    hi = v.astype(BF16)
    return hi, (v - hi.astype(F32)).astype(BF16)


def _split3(v):
    hi, _ = _split2(v)
    r1 = v - hi.astype(F32)
    mid, lo = _split2(r1)
    return hi, mid, lo


def _expand_rows(cols, n_tiles, lane_lo):
    tiles = [jnp.where(lane_lo, cols[:, 2 * t:2 * t + 1], cols[:, 2 * t + 1:2 * t + 2])
             for t in range(n_tiles)]
    return jnp.concatenate(tiles, axis=1)


def _ssd_kernel(*refs, segments, hpg):
    n_seg = len(segments)
    seg_refs = [refs[6 * i:6 * i + 6] for i in range(n_seg)]
    pos = 6 * n_seg
    cwx_ref, cwb_ref, cwc_ref, cbx_ref, cbb_ref, cbc_ref, dsk_ref, nw_ref = refs[pos:pos + 8]
    pos += 8
    y_refs = []
    for seg in segments:
        y_refs.append(refs[pos] if seg.emit_y else None)
        pos += 1 if seg.emit_y else 0
    (sm_ref, s0_ref, sp_ref, cc_ref, rowq_ref, tblk_ref, tb_ref, yd_ref, contrib_ref, dec_ref,
     sball_ref, sf_ref, sb_ref) = refs[pos:]

    gw = hpg * HEAD_DIM
    n_tiles = gw // LANES
    conv_w = s0_ref.shape[1]
    shifted = (sm_ref, s0_ref, sp_ref)
    ri = lax.broadcasted_iota(jnp.int32, (CHUNK, CHUNK), 0)
    ci = lax.broadcasted_iota(jnp.int32, (CHUNK, CHUNK), 1)
    tri_t = (ri <= ci).astype(BF16)
    below = ci < ri
    on_diag = ci == ri
    lane_lo = lax.broadcasted_iota(jnp.int32, (1, LANES), 1) < HEAD_DIM
    erow = lax.broadcasted_iota(jnp.int32, (CHUNK, 2 * gw), 0)
    ehead = lax.broadcasted_iota(jnp.int32, (CHUNK, 2 * gw), 1) // HEAD_DIM
    expand_w = (erow == ehead + 2 * HP).astype(BF16)
    expand_off = (erow == ehead + 3 * HP).astype(BF16)

    def chunk_rows(c):
        return pl.ds(pl.multiple_of(c * CHUNK, CHUNK), CHUNK)

    def run_segment(x_ref, b_ref, c_ref, z_ref, dtr_ref, pr_ref, y_ref, seg):
        length, grid_rows, grid_cols = seg.length, seg.grid_rows, seg.grid_cols
        nc = length // CHUNK
        nrow = nc * HP

        halo = grid_cols if grid_rows > 1 else 0
        if halo:
            zeros = jnp.zeros((halo, conv_w), F32)
            for s_ref in shifted:
                s_ref[0:halo, :] = zeros
                s_ref[halo + length:2 * halo + length, :] = zeros
        sources = ((x_ref, 0, gw), (b_ref, gw, D_STATE), (c_ref, gw + D_STATE, D_STATE))

        def shift_row(r, carry):
            src_rows = pl.ds(pl.multiple_of(r * grid_cols, grid_cols), grid_cols)
            dst_rows = pl.ds(pl.multiple_of(halo + r * grid_cols, grid_cols), grid_cols)
            for ref, lane0, width in sources:
                wpos = lax.broadcasted_iota(jnp.int32, (grid_cols, width), 0)
                xr = ref[src_rows, :].astype(F32)
                dst = slice(lane0, lane0 + width)
                s0_ref[dst_rows, dst] = xr
                sm_ref[dst_rows, dst] = jnp.where(wpos == 0, 0.0, pltpu.roll(xr, 1, axis=0))
                sp_ref[dst_rows, dst] = jnp.where(wpos == grid_cols - 1, 0.0,
                                                  pltpu.roll(xr, grid_cols - 1, axis=0))
            return carry

        lax.fori_loop(0, grid_rows, shift_row, 0)
        dhs = (-1, 0, 1) if grid_rows > 1 else (0,)

        def conv_tile(c, lane0, w_ref, b_ref_, t):
            wl = slice(t * LANES, (t + 1) * LANES)
            sl = slice(lane0 + t * LANES, lane0 + (t + 1) * LANES)
            bias = b_ref_[:, wl]
            outs = []
            for s in range(CHUNK // CONV_SUB):
                base = pl.multiple_of(c * CHUNK, CHUNK) + s * CONV_SUB
                acc = bias
                for dh in dhs:
                    off = halo + base + dh * grid_cols
                    for kw, s_ref in enumerate(shifted):
                        k = 3 * (dh + 1) + kw
                        acc = acc + w_ref[k:k + 1, wl] * s_ref[pl.ds(off, CONV_SUB), sl]
                outs.append(_silu(acc))
            return jnp.concatenate(outs, axis=0)

        row_fwd = (lax.broadcasted_iota(jnp.int32, (nrow, 1), 0) & (HP - 1)) < hpg
        dt_r = _softplus(dtr_ref[...].reshape(nrow, CHUNK) + pr_ref[:, 0:1])
        a_r = dt_r * (-jnp.exp(pr_ref[:, 1:2]))
        cum_r = sum(jnp.dot(p, tri_t, preferred_element_type=F32) for p in _split3(a_r))
        excl_r = cum_r - a_r
        tot_r = jnp.broadcast_to(cum_r[:, CHUNK - 1:CHUNK], cum_r.shape)
        log_dt = jnp.log(dt_r)
        rowq_ref[0, 0:nrow, :] = dt_r
        rowq_ref[1, 0:nrow, :] = cum_r - log_dt
        rowq_ref[2, 0:nrow, :] = excl_r + log_dt
        to_cols = (
            jnp.where(row_fwd, cum_r, excl_r),
            tot_r,
            jnp.where(row_fwd, jnp.exp(tot_r - cum_r), jnp.exp(excl_r)) * dt_r,
            jnp.where(row_fwd, jnp.exp(cum_r), jnp.exp(tot_r - excl_r)),
        )
        for c in range(nc):
            for q, val in enumerate(to_cols):
                tblk_ref[c, q * HP:(q + 1) * HP, :] = val[c * HP:(c + 1) * HP, :]
            tblk_ref[c, len(to_cols) * HP:, :] = jnp.zeros((CHUNK - len(to_cols) * HP, CHUNK), F32)

        def phase_a(c, carry):
            rows = chunk_rows(c)
            rrows = pl.ds(pl.multiple_of(c * HP, HP), HP)
            blk_t = tblk_ref[c].T
            blk_tb = blk_t.astype(BF16)
            tb_ref[c] = blk_tb
            dt_row = rowq_ref[0, rrows, :]
            cf_row = rowq_ref[1, rrows, :]
            eb_row = rowq_ref[2, rrows, :]

            bmat = conv_tile(c, gw, cwb_ref, cbb_ref, 0).astype(BF16)
            cmat = conv_tile(c, gw + D_STATE, cwc_ref, cbc_ref, 0).astype(BF16)
            cc_ref[rows, :] = cmat
            cb = lax.dot_general(cmat, bmat, (((1,), (1,)), ((), ())), preferred_element_type=F32)

            def mix(h):
                arg = jnp.where(below, blk_t[:, h:h + 1] - cf_row[h:h + 1, :],
                                eb_row[hpg + h:hpg + h + 1, :] - blk_t[:, hpg + h:hpg + h + 1])
                e = jnp.exp(arg) + jnp.where(on_diag, dt_row[h:h + 1, :], 0.0)
                return (cb * e).astype(BF16)

            w_exp = jnp.dot(blk_tb, expand_w, preferred_element_type=F32)

            xs_f, xs_b = [], []
            for t in range(n_tiles):
                lanes = slice(t * LANES, (t + 1) * LANES)
                xf = conv_tile(c, 0, cwx_ref, cbx_ref, t)
                xt = xf.astype(BF16)
                zero = jnp.zeros_like(xt)
                rhs = jnp.concatenate([jnp.where(lane_lo, xt, zero), jnp.where(lane_lo, zero, xt)],
                                      axis=0)
                lhs = jnp.concatenate([mix(2 * t), mix(2 * t + 1)], axis=1)
                yd_ref[rows, lanes] = (jnp.dot(lhs, rhs, preferred_element_type=F32)
                                       + dsk_ref[:, lanes] * xf)
                xs_f.append((xf * w_exp[:, t * LANES:(t + 1) * LANES]).astype(BF16))
                xs_b.append((xf * w_exp[:, gw + t * LANES:gw + (t + 1) * LANES]).astype(BF16))
            contrib_ref[c] = lax.dot_general(bmat, jnp.concatenate(xs_f + xs_b, axis=1),
                                             (((0,), (0,)), ((), ())), preferred_element_type=F32)
            dec_ref[c] = _expand_rows(jnp.exp(blk_t[0:1, HP:2 * HP]), 2 * n_tiles, lane_lo)
            return carry

        lax.fori_loop(0, nc, phase_a, 0, unroll=min(nc, 4))

        def phase_b(k, carry):
            c = nc - 1 - k
            sb = sb_ref[...]
            sball_ref[c] = sb.astype(BF16)
            sb_ref[...] = sb * dec_ref[c][:, gw:] + contrib_ref[c, :, gw:]
            return carry

        lax.fori_loop(0, nc, phase_b, 0, unroll=min(nc, 4))

        def phase_c(c, carry):
            rows = chunk_rows(c)
            sf = sf_ref[...]
            if y_ref is not None:
                s_cat = jnp.concatenate([sf.astype(BF16), sball_ref[c]], axis=1)
                cs = jnp.dot(cc_ref[rows, :], s_cat, preferred_element_type=F32)
                off = jnp.dot(tb_ref[c], expand_off, preferred_element_type=F32)
                g_tiles = []
                ssq = jnp.zeros((CHUNK, 1), F32)
                for t in range(n_tiles):
                    lanes = slice(t * LANES, (t + 1) * LANES)
                    blanes = slice(gw + t * LANES, gw + (t + 1) * LANES)
                    y = (yd_ref[rows, lanes] + cs[:, lanes] * off[:, lanes]
                         + cs[:, blanes] * off[:, blanes])
                    g = y * _silu(z_ref[rows, lanes].astype(F32))
                    ssq = ssq + jnp.sum(g * g, axis=-1, keepdims=True)
                    g_tiles.append(g)
                inv = lax.rsqrt(ssq * (1.0 / gw) + EPS)
                for t in range(n_tiles):
                    lanes = slice(t * LANES, (t + 1) * LANES)
                    y_ref[rows, lanes] = ((g_tiles[t] * inv) * nw_ref[:, lanes]).astype(BF16)
            sf_ref[...] = sf * dec_ref[c][:, :gw] + contrib_ref[c, :, :gw]
            return carry

        lax.fori_loop(0, nc, phase_c, 0, unroll=min(nc, 8))

    sf_ref[...] = jnp.zeros(sf_ref.shape, F32)
    sb_ref[...] = jnp.zeros(sb_ref.shape, F32)
    for (x_ref, b_ref, c_ref, z_ref, dtr_ref, pr_ref), y_ref, seg in zip(seg_refs, y_refs, segments):
        run_segment(x_ref, b_ref, c_ref, z_ref, dtr_ref, pr_ref, y_ref, seg)


class _Segment(NamedTuple):
    length: int
    grid_rows: int
    grid_cols: int
    emit_y: bool


def _ssd(token_sets, p_col, conv_w9, conv_b, d_exp, norm_w, d_ssd):
    g = N_BC_GROUPS
    gw = d_ssd // g
    hpg = gw // HEAD_DIM
    bc_first = d_ssd // D_STATE
    pad = HP - 2 * hpg
    p_col = jnp.pad(p_col, ((0, 0), (0, 0), (0, pad)))
    bsz = token_sets[0][0].shape[0]
    segments, args, in_specs, out_shape, out_specs = [], [], [], [], []
    for xbc, zs, dt, grid_rows, grid_cols, emit_y in token_sets:
        length = xbc.shape[1]
        nc = length // CHUNK
        segments.append(_Segment(length, grid_rows, grid_cols, emit_y))
        p_row = jnp.tile(p_col.transpose(0, 2, 1), (1, nc, 1))
        args += [xbc, xbc, xbc, zs, dt, p_row]
        in_specs += [
            pl.BlockSpec((None, length, gw), lambda b, j: (b, 0, j)),
            pl.BlockSpec((None, length, D_STATE), lambda b, j: (b, 0, bc_first + j)),
            pl.BlockSpec((None, length, D_STATE), lambda b, j: (b, 0, bc_first + g + j)),
            pl.BlockSpec((None, length, gw), lambda b, j: (b, 0, j)),
            pl.BlockSpec((None, nc, HP, CHUNK), lambda b, j: (b, 0, j, 0)),
            pl.BlockSpec((None, nc * HP, 2), lambda b, j: (j, 0, 0)),
        ]
        if emit_y:
            out_shape.append(jax.ShapeDtypeStruct((bsz, length, d_ssd), BF16))
            out_specs.append(pl.BlockSpec((None, length, gw), lambda b, j: (b, 0, j)))
    args += [conv_w9, conv_w9, conv_w9, conv_b, conv_b, conv_b, d_exp, norm_w]
    in_specs += [
        pl.BlockSpec((9, gw), lambda b, j: (0, j)),
        pl.BlockSpec((9, D_STATE), lambda b, j: (0, bc_first + j)),
        pl.BlockSpec((9, D_STATE), lambda b, j: (0, bc_first + g + j)),
        pl.BlockSpec((1, gw), lambda b, j: (0, j)),
        pl.BlockSpec((1, D_STATE), lambda b, j: (0, bc_first + j)),
        pl.BlockSpec((1, D_STATE), lambda b, j: (0, bc_first + g + j)),
        pl.BlockSpec((1, gw), lambda b, j: (0, j)),
        pl.BlockSpec((1, gw), lambda b, j: (0, j)),
    ]
    max_len = max(seg.length for seg in segments)
    max_nc = max_len // CHUNK
    conv_rows = max(seg.length + (2 * seg.grid_cols if seg.grid_rows > 1 else 0) for seg in segments)
    conv_w = gw + 2 * D_STATE
    outs = pl.pallas_call(
        functools.partial(_ssd_kernel, segments=tuple(segments), hpg=hpg),
        out_shape=out_shape,
        grid=(bsz, g),
        in_specs=in_specs,
        out_specs=out_specs,
        scratch_shapes=[pltpu.VMEM((conv_rows, conv_w), F32),
                        pltpu.VMEM((conv_rows, conv_w), F32),
                        pltpu.VMEM((conv_rows, conv_w), F32),
                        pltpu.VMEM((max_len, D_STATE), BF16),
                        pltpu.VMEM((3, max_nc * HP, CHUNK), F32),
                        pltpu.VMEM((max_nc, CHUNK, CHUNK), F32),
                        pltpu.VMEM((max_nc, CHUNK, CHUNK), BF16),
                        pltpu.VMEM((max_len, gw), F32),
                        pltpu.VMEM((max_nc, D_STATE, 2 * gw), F32),
                        pltpu.VMEM((max_nc, 1, 2 * gw), F32),
                        pltpu.VMEM((max_nc, D_STATE, gw), BF16),
                        pltpu.VMEM((D_STATE, gw), F32),
                        pltpu.VMEM((D_STATE, gw), F32)],
        compiler_params=_cparams("parallel", "parallel"),
        name="ssd",
    )(*args)
    outs = list(outs)
    return [outs.pop(0) if seg.emit_y else None for seg in segments]


def _final_kernel(four_ref, ssd_ref, x_ref, gate_ref, w_ref, fnw_ref, o_ref):
    y = _residual(four_ref, ssd_ref, x_ref, gate_ref, w_ref)
    ms = jnp.mean(y * y, axis=-1, keepdims=True)
    o_ref[...] = (y * lax.rsqrt(ms + EPS)) * fnw_ref[...]


def _final(four, ssd, x, gate, w_out, layer, final_norm_w):
    bsz, length, d = x.shape
    tm = min(length, 512)
    row_map = lambda b, t: (b, t, 0)
    return pl.pallas_call(
        _final_kernel,
        out_shape=jax.ShapeDtypeStruct((bsz, length, d), F32),
        grid=(bsz, length // tm),
        in_specs=[
            pl.BlockSpec((None, tm, four.shape[2]), row_map),
            pl.BlockSpec((None, tm, ssd.shape[2]), row_map),
            pl.BlockSpec((None, tm, d), row_map),
            pl.BlockSpec((None, 1, d), lambda b, t: (b, 0, 0)),
            _resident(w_out.shape[1:], layer),
            _resident((1, d)),
        ],
        out_specs=pl.BlockSpec((None, tm, d), row_map),
        compiler_params=_cparams("parallel", "parallel"),
        name="final",
    )(four, ssd, x, gate, w_out, final_norm_w)


def kernel(x, c, ctx, c_ctx, norm_w, w_ada, b_ada, w_in, conv_w, conv_b, dt_bias, a_log, d_skip,
           ssd_norm_w, w_fourier, b_fourier, w_out, final_norm_w):
    bsz, seq_len, d = x.shape
    ctx_len = ctx.shape[1]
    depth = w_in.shape[0]
    d_f = w_fourier.shape[1]
    d_ssd = ssd_norm_w.shape[1]
    conv_ch = conv_w.shape[-1]
    n_heads = d_skip.shape[1]
    g = N_BC_GROUPS
    hpg = n_heads // g
    splits = (d_f, d_f, d_ssd, conv_ch)

    pad_rows = (-(bsz + 1)) % 8
    cc = jnp.concatenate([c, c_ctx[None, :], jnp.zeros((pad_rows, d), F32)], axis=0)
    mod = _adaln(cc, w_ada, b_ada)

    w_in_b = w_in.astype(BF16)
    w_dt = w_in[:, :, sum(splits):].reshape(depth, d, 2, g, hpg).transpose(0, 3, 2, 4, 1)
    w_dt = jnp.pad(w_dt.reshape(depth, g, 2 * hpg, d), ((0, 0), (0, 0), (0, HP - 2 * hpg), (0, 0)))
    w_dt = w_dt.reshape(depth, g * HP, d).astype(BF16)
    w_out_b = w_out.astype(BF16)
    w_f_b = w_fourier.astype(BF16)
    w9 = conv_w.reshape(depth, 9, conv_ch)
    par = jnp.stack([dt_bias, a_log], axis=1)
    p_col = par.reshape(depth, 2, 2, g, hpg).transpose(0, 3, 1, 2, 4).reshape(depth, g, 2, 2 * hpg)
    d_exp = jnp.repeat(d_skip, HEAD_DIM, axis=1).reshape(depth, 1, d_ssd)

    fseq_x, fch = (jnp.asarray(t).astype(BF16) for t in _dft_tables(seq_len, d_f))
    fseq_c = jnp.asarray(_dft_tables(ctx_len, d_f)[0]).astype(BF16)
    rows = seq_len // GRID_W

    prev_c = prev_x = None
    for i in range(depth):
        last = i == depth - 1
        m = mod[i]
        shift, scale, gate = (m[:bsz, None, k * d:(k + 1) * d] for k in range(3))
        shift_c, scale_c, gate_c = (m[bsz:bsz + 1, None, k * d:(k + 1) * d] for k in range(3))
        nw_i = norm_w[i].reshape(1, d)
        outs_c = _inproj(ctx, shift_c, scale_c, nw_i, w_in_b, w_dt, i, splits, prev_c)
        outs_x = _inproj(x, shift, scale, nw_i, w_in_b, w_dt, i, splits, prev_x)
        if i > 0:
            ctx, outs_c = outs_c[0], outs_c[1:]
            x, outs_x = outs_x[0], outs_x[1:]
        (u_c, zf_c, zs_c, xbc_c, dt_c), (u_x, zf_x, zs_x, xbc_x, dt_x) = outs_c, outs_x
        ssd_c, ssd_x = _ssd(
            [(xbc_c, zs_c, dt_c, 1, ctx_len, not last), (xbc_x, zs_x, dt_x, rows, GRID_W, True)],
            p_col[i], w9[i], conv_b[i].reshape(1, conv_ch), d_exp[i],
            ssd_norm_w[i].reshape(1, d_ssd), d_ssd)
        b_f = b_fourier[i].reshape(1, d_f)
        four_x = _fourier(u_x, zf_x, fseq_x, fch, w_f_b, i, b_f)
        prev_x = (four_x, ssd_x, gate, w_out_b)
        if not last:
            four_c = _fourier(u_c, zf_c, fseq_c, fch, w_f_b, i, b_f)
            prev_c = (four_c, ssd_c, gate_c, w_out_b)
    four_x, ssd_x, gate, w_o = prev_x
    return _final(four_x, ssd_x, x, gate, w_o, depth - 1, final_norm_w.reshape(1, d))
```

```python
import functools
import math
from typing import NamedTuple

import jax
import jax.numpy as jnp
import numpy as np
from jax import lax
from jax.experimental import pallas as pl
from jax.experimental.pallas import tpu as pltpu

F32 = jnp.float32
BF16 = jnp.bfloat16

GRID_W = 64
FOURIER_GROUP = 64
HEAD_DIM = 64
N_BC_GROUPS = 4
D_STATE = 128
CHUNK = 128
EPS = 1e-6
LOG2E = 1.4426950408889634

LANES = 128
BF16_SUBLANES = 16
MXU_WIDTH = 256
VMEM_LIMIT_BYTES = 56 * 1024 * 1024

HEADS_PER_TILE = LANES // HEAD_DIM


def _cparams(*sem):
    return pltpu.CompilerParams(dimension_semantics=sem, vmem_limit_bytes=VMEM_LIMIT_BYTES)


def _silu(v):
    return v * (1.0 / (1.0 + jnp.exp2(v * (-LOG2E))))


def _softplus(v):
    return jnp.maximum(v, 0.0) + jnp.log1p(jnp.exp(-jnp.abs(v)))


def _resident(shape, layer=None):
    zeros = (0,) * len(shape)
    if layer is None:
        return pl.BlockSpec(shape, lambda *_: zeros, pipeline_mode=pl.Buffered(1))
    return pl.BlockSpec((None,) + tuple(shape), lambda *_: (layer,) + zeros,
                        pipeline_mode=pl.Buffered(1))


def _adaln_kernel(c_ref, w_ref, b_ref, o_ref):
    s = _silu(c_ref[...])
    o_ref[...] = jnp.dot(s, w_ref[...], preferred_element_type=F32,
                         precision=lax.Precision.HIGHEST) + b_ref[...]


def _adaln(cc, w_ada, b_ada):
    depth, d, d3 = w_ada.shape
    rows = cc.shape[0]
    tn = 1024
    return pl.pallas_call(
        _adaln_kernel,
        out_shape=jax.ShapeDtypeStruct((depth, rows, d3), F32),
        grid=(depth, d3 // tn),
        in_specs=[
            pl.BlockSpec((rows, d), lambda i, j: (0, 0)),
            pl.BlockSpec((None, d, tn), lambda i, j: (i, 0, j)),
            pl.BlockSpec((None, 1, tn), lambda i, j: (i, 0, j)),
        ],
        out_specs=pl.BlockSpec((None, rows, tn), lambda i, j: (i, 0, j)),
        compiler_params=_cparams("parallel", "parallel"),
        name="adaln",
    )(cc, w_ada, b_ada.reshape(depth, 1, d3))


def _project(x, sh_ref, sc_ref, nw_ref, w_ref, wdt_ref, out_refs, dt_ref, splits, col_tile):
    ms = jnp.mean(x * x, axis=-1, keepdims=True)
    h = (x * lax.rsqrt(ms + EPS)) * nw_ref[...]
    h = h * (1.0 + sc_ref[...]) + sh_ref[...]
    hb = h.astype(BF16)
    col = 0
    for out_ref, width in zip(out_refs, splits):
        for c0 in range(0, width, col_tile):
            out_ref[:, c0:c0 + col_tile] = jnp.dot(
                hb, w_ref[:, col + c0:col + c0 + col_tile],
                preferred_element_type=F32).astype(BF16)
        col += width
    dt_t = lax.dot_general(wdt_ref[...], hb, (((1,), (1,)), ((), ())), preferred_element_type=F32)
    for k in range(dt_ref.shape[0]):
        dt_ref[k] = dt_t[:, k * CHUNK:(k + 1) * CHUNK]


def _residual(four_ref, ssd_ref, x_ref, gate_ref, wo_ref):
    d_f = four_ref.shape[1]
    acc = jnp.dot(four_ref[...], wo_ref[0:d_f, :], preferred_element_type=F32)
    acc = acc + jnp.dot(ssd_ref[...], wo_ref[d_f:, :], preferred_element_type=F32)
    return x_ref[...] + gate_ref[...] * acc


def _inproj_kernel(x_ref, sh_ref, sc_ref, nw_ref, w_ref, wdt_ref,
                   u_ref, zf_ref, zs_ref, xbc_ref, dt_ref, *, splits, col_tile):
    _project(x_ref[...], sh_ref, sc_ref, nw_ref, w_ref, wdt_ref, (u_ref, zf_ref, zs_ref, xbc_ref),
             dt_ref, splits, col_tile)


def _outproj_inproj_kernel(four_ref, ssd_ref, x_ref, gate_ref, wo_ref, sh_ref, sc_ref, nw_ref, w_ref,
                           wdt_ref, xo_ref, u_ref, zf_ref, zs_ref, xbc_ref, dt_ref, *, splits, col_tile):
    x_new = _residual(four_ref, ssd_ref, x_ref, gate_ref, wo_ref)
    xo_ref[...] = x_new
    _project(x_new, sh_ref, sc_ref, nw_ref, w_ref, wdt_ref, (u_ref, zf_ref, zs_ref, xbc_ref),
             dt_ref, splits, col_tile)


def _inproj(x, shift, scale, norm_w, w_in, w_dt, layer, splits, prev=None):
    bsz, length, d = x.shape
    tm = min(length, 512)
    n_dt = w_dt.shape[1]
    row_map = lambda b, t: (b, t, 0)

    def per_batch(arr):
        return pl.BlockSpec((None, 1, d), (lambda b, t: (b, 0, 0)) if arr.shape[0] > 1
                            else (lambda b, t: (0, 0, 0)))

    out_shape = [jax.ShapeDtypeStruct((bsz, length, w), BF16) for w in splits]
    out_shape.append(jax.ShapeDtypeStruct((bsz, length // CHUNK, n_dt, CHUNK), F32))
    out_specs = [pl.BlockSpec((None, tm, w), row_map) for w in splits]
    out_specs.append(pl.BlockSpec((None, tm // CHUNK, n_dt, CHUNK), lambda b, t: (b, t, 0, 0)))
    in_specs = [pl.BlockSpec((None, tm, d), row_map), per_batch(shift), per_batch(scale),
                _resident((1, d)), _resident(w_in.shape[1:], layer), _resident(w_dt.shape[1:], layer)]
    args = [x, shift, scale, norm_w, w_in, w_dt]
    body = _inproj_kernel
    if prev is not None:
        four, ssd, gate, w_out = prev
        in_specs = [pl.BlockSpec((None, tm, four.shape[2]), row_map),
                    pl.BlockSpec((None, tm, ssd.shape[2]), row_map),
                    in_specs[0], per_batch(gate), _resident(w_out.shape[1:], layer - 1)] + in_specs[1:]
        args = [four, ssd, x, gate, w_out] + args[1:]
        out_shape = [jax.ShapeDtypeStruct((bsz, length, d), F32)] + out_shape
        out_specs = [pl.BlockSpec((None, tm, d), row_map)] + out_specs
        body = _outproj_inproj_kernel
    return pl.pallas_call(
        functools.partial(body, splits=splits, col_tile=512),
        out_shape=out_shape,
        grid=(bsz, length // tm),
        in_specs=in_specs,
        out_specs=out_specs,
        compiler_params=_cparams("parallel", "parallel"),
        name="inproj" if prev is None else "outproj_inproj",
    )(*args)


def _reverse_into(src_ref, src_end, dst_ref, dst0, count):
    blk = LANES
    ri = lax.broadcasted_iota(jnp.int32, (blk, 2 * blk), 0)
    ci = lax.broadcasted_iota(jnp.int32, (blk, 2 * blk), 1)
    rev = ((ri + ci == blk) | ((ri == 0) & (ci == blk))).astype(BF16)
    for i in range(count // blk):
        lo = src_ref[src_end - (i + 1) * blk:src_end - i * blk, :]
        hi = (src_ref[src_end - i * blk:src_end - (i - 1) * blk, :] if i > 0
              else jnp.zeros_like(lo))
        dst_ref[dst0 + i * blk:dst0 + (i + 1) * blk, :] = jnp.dot(
            rev, jnp.concatenate([lo, hi], axis=0), preferred_element_type=F32).astype(BF16)


def _fourier_kernel(u_ref, zf_ref, fseq_ref, fch_ref, wf_ref, bf_ref, o_ref,
                    urev_ref, ucs_ref, mix_ref, mirror_ref, *, row_tile):
    length, width = u_ref.shape
    half = length // 2
    scale = 1.0 / math.sqrt(length)

    _reverse_into(u_ref, length, urev_ref, 0, half)
    mids = []
    for c0 in range(0, width, MXU_WIDTH):
        cols = slice(c0, c0 + MXU_WIDTH)
        uh = u_ref[0:half, cols]
        ur = urev_ref[:, cols]
        cos_ch = fch_ref[cols, cols]
        sin_ch = fch_ref[cols, width + c0:width + c0 + MXU_WIDTH]
        ucs_ref[0:half, cols] = (jnp.dot(uh, cos_ch, preferred_element_type=F32)
                                 + jnp.dot(ur, cos_ch, preferred_element_type=F32)).astype(BF16)
        ucs_ref[half:length, cols] = (jnp.dot(uh, sin_ch, preferred_element_type=F32)
                                      - jnp.dot(ur, sin_ch, preferred_element_type=F32)).astype(BF16)
        mids.append(jnp.dot(u_ref[half:half + BF16_SUBLANES, cols], cos_ch,
                            preferred_element_type=F32)[0:1, :] * scale)
    mid = jnp.concatenate(mids, axis=1)

    for r in range(half // row_tile):
        rows = slice(r * row_tile, (r + 1) * row_tile)
        c_part = jnp.dot(fseq_ref[rows, 0:half], ucs_ref[0:half, :], preferred_element_type=F32)
        s_part = jnp.dot(fseq_ref[rows, half:length], ucs_ref[half:length, :],
                         preferred_element_type=F32)
        k = r * row_tile + lax.broadcasted_iota(jnp.int32, (row_tile, 1), 0)
        base = c_part + (1 - 2 * (k & 1)).astype(F32) * mid
        mix_ref[rows, :] = (base + s_part).astype(BF16)
        mirror_ref[rows, :] = (base - s_part).astype(BF16)
    _reverse_into(mirror_ref, half, mix_ref, half, half)
    li = lax.broadcasted_iota(jnp.int32, (BF16_SUBLANES, half), 1)
    alt = (1 - 2 * (li & 1)).astype(BF16)
    row_half = (jnp.dot(alt, ucs_ref[0:half, :], preferred_element_type=F32)[0:1, :] * scale
                + (1.0 if half % 2 == 0 else -1.0) * mid)
    first = mix_ref[half:half + BF16_SUBLANES, :]
    sub = lax.broadcasted_iota(jnp.int32, first.shape, 0)
    mix_ref[half:half + BF16_SUBLANES, :] = jnp.where(sub == 0, row_half.astype(BF16), first)

    for r in range(length // row_tile):
        rows = slice(r * row_tile, (r + 1) * row_tile)
        lin = jnp.dot(mix_ref[rows, :], wf_ref[...], preferred_element_type=F32) + bf_ref[...]
        o_ref[rows, :] = (lin * _silu(zf_ref[rows, :].astype(F32))).astype(BF16)


def _fourier(u, zf, fseq, fch, w_f, layer, b_f):
    bsz, length, width = u.shape
    half = length // 2
    assert half % LANES == 0, "the block-wise reversal needs L/2 to be a multiple of 128"
    assert width % MXU_WIDTH == 0 and MXU_WIDTH % FOURIER_GROUP == 0
    batch_block = pl.BlockSpec((None, length, width), lambda b: (b, 0, 0))
    return pl.pallas_call(
        functools.partial(_fourier_kernel, row_tile=min(half, 256)),
        out_shape=jax.ShapeDtypeStruct((bsz, length, width), BF16),
        grid=(bsz,),
        in_specs=[batch_block, batch_block, _resident(fseq.shape), _resident(fch.shape),
                  _resident(w_f.shape[1:], layer), _resident(b_f.shape)],
        out_specs=batch_block,
        scratch_shapes=[pltpu.VMEM((half, width), BF16),
                        pltpu.VMEM((length, width), BF16),
                        pltpu.VMEM((length, width), BF16),
                        pltpu.VMEM((half, width), BF16)],
        compiler_params=_cparams("parallel"),
        name="fourier",
    )(u, zf, fseq, fch, w_f, b_f)


@functools.lru_cache(maxsize=None)
def _dft_tables(length, width):
    half = length // 2
    kl = (np.arange(half)[:, None] * np.arange(half)[None, :]) % length
    ang = kl * (2.0 * np.pi / length)
    fseq = np.concatenate([np.cos(ang), -np.sin(ang)], axis=1) / np.sqrt(length)
    ch = np.arange(width)
    same = (ch[:, None] // FOURIER_GROUP) == (ch[None, :] // FOURIER_GROUP)
    angc = (((ch[:, None] % FOURIER_GROUP) * (ch[None, :] % FOURIER_GROUP)) % FOURIER_GROUP) * (
        2.0 * np.pi / FOURIER_GROUP)
    fch = np.concatenate([np.where(same, np.cos(angc), 0.0),
                          np.where(same, np.sin(angc), 0.0)], axis=1) / np.sqrt(FOURIER_GROUP)
    return fseq.astype(np.float32), fch.astype(np.float32)


HP = BF16_SUBLANES
CONV_SUB = 128


def _split2(v):
    hi = v.astype(BF16)
    return hi, (v - hi.astype(F32)).astype(BF16)


def _split3(v):
    hi, _ = _split2(v)
    r1 = v - hi.astype(F32)
    mid, lo = _split2(r1)
    return hi, mid, lo


def _expand_rows(cols, n_tiles, lane_lo):
    tiles = [jnp.where(lane_lo, cols[:, 2 * t:2 * t + 1], cols[:, 2 * t + 1:2 * t + 2])
             for t in range(n_tiles)]
    return jnp.concatenate(tiles, axis=1)


def _ssd_kernel(*refs, segments, hpg):
    n_seg = len(segments)
    seg_refs = [refs[6 * i:6 * i + 6] for i in range(n_seg)]
    pos = 6 * n_seg
    cwx_ref, cwb_ref, cwc_ref, cbx_ref, cbb_ref, cbc_ref, dsk_ref, nw_ref = refs[pos:pos + 8]
    pos += 8
    y_refs = []
    for seg in segments:
        y_refs.append(refs[pos] if seg.emit_y else None)
        pos += 1 if seg.emit_y else 0
    (sm_ref, s0_ref, sp_ref, cc_ref, rowq_ref, tblk_ref, tb_ref, yd_ref, contrib_ref, dec_ref,
     sball_ref, sf_ref, sb_ref) = refs[pos:]

    gw = hpg * HEAD_DIM
    n_tiles = gw // LANES
    conv_w = s0_ref.shape[1]
    shifted = (sm_ref, s0_ref, sp_ref)
    ri = lax.broadcasted_iota(jnp.int32, (CHUNK, CHUNK), 0)
    ci = lax.broadcasted_iota(jnp.int32, (CHUNK, CHUNK), 1)
    tri_t = (ri <= ci).astype(BF16)
    below = ci < ri
    on_diag = ci == ri
    lane_lo = lax.broadcasted_iota(jnp.int32, (1, LANES), 1) < HEAD_DIM
    erow = lax.broadcasted_iota(jnp.int32, (CHUNK, 2 * gw), 0)
    ehead = lax.broadcasted_iota(jnp.int32, (CHUNK, 2 * gw), 1) // HEAD_DIM
    expand_w = (erow == ehead + 2 * HP).astype(BF16)
    expand_off = (erow == ehead + 3 * HP).astype(BF16)

    def chunk_rows(c):
        return pl.ds(pl.multiple_of(c * CHUNK, CHUNK), CHUNK)

    def run_segment(x_ref, b_ref, c_ref, z_ref, dtr_ref, pr_ref, y_ref, seg):
        length, grid_rows, grid_cols = seg.length, seg.grid_rows, seg.grid_cols
        nc = length // CHUNK
        nrow = nc * HP

        halo = grid_cols if grid_rows > 1 else 0
        if halo:
            zeros = jnp.zeros((halo, conv_w), F32)
            for s_ref in shifted:
                s_ref[0:halo, :] = zeros
                s_ref[halo + length:2 * halo + length, :] = zeros
        sources = ((x_ref, 0, gw), (b_ref, gw, D_STATE), (c_ref, gw + D_STATE, D_STATE))

        def shift_row(r, carry):
            src_rows = pl.ds(pl.multiple_of(r * grid_cols, grid_cols), grid_cols)
            dst_rows = pl.ds(pl.multiple_of(halo + r * grid_cols, grid_cols), grid_cols)
            for ref, lane0, width in sources:
                wpos = lax.broadcasted_iota(jnp.int32, (grid_cols, width), 0)
                xr = ref[src_rows, :].astype(F32)
                dst = slice(lane0, lane0 + width)
                s0_ref[dst_rows, dst] = xr
                sm_ref[dst_rows, dst] = jnp.where(wpos == 0, 0.0, pltpu.roll(xr, 1, axis=0))
                sp_ref[dst_rows, dst] = jnp.where(wpos == grid_cols - 1, 0.0,
                                                  pltpu.roll(xr, grid_cols - 1, axis=0))
            return carry

        lax.fori_loop(0, grid_rows, shift_row, 0)
        dhs = (-1, 0, 1) if grid_rows > 1 else (0,)

        def conv_tile(c, lane0, w_ref, b_ref_, t):
            wl = slice(t * LANES, (t + 1) * LANES)
            sl = slice(lane0 + t * LANES, lane0 + (t + 1) * LANES)
            bias = b_ref_[:, wl]
            outs = []
            for s in range(CHUNK // CONV_SUB):
                base = pl.multiple_of(c * CHUNK, CHUNK) + s * CONV_SUB
                acc = bias
                for dh in dhs:
                    off = halo + base + dh * grid_cols
                    for kw, s_ref in enumerate(shifted):
                        k = 3 * (dh + 1) + kw
                        acc = acc + w_ref[k:k + 1, wl] * s_ref[pl.ds(off, CONV_SUB), sl]
                outs.append(_silu(acc))
            return jnp.concatenate(outs, axis=0)

        row_fwd = (lax.broadcasted_iota(jnp.int32, (nrow, 1), 0) & (HP - 1)) < hpg
        dt_r = _softplus(dtr_ref[...].reshape(nrow, CHUNK) + pr_ref[:, 0:1])
        a_r = dt_r * (-jnp.exp(pr_ref[:, 1:2]))
        cum_r = sum(jnp.dot(p, tri_t, preferred_element_type=F32) for p in _split3(a_r))
        excl_r = cum_r - a_r
        tot_r = jnp.broadcast_to(cum_r[:, CHUNK - 1:CHUNK], cum_r.shape)
        log_dt = jnp.log(dt_r)
        rowq_ref[0, 0:nrow, :] = dt_r
        rowq_ref[1, 0:nrow, :] = cum_r - log_dt
        rowq_ref[2, 0:nrow, :] = excl_r + log_dt
        to_cols = (
            jnp.where(row_fwd, cum_r, excl_r),
            tot_r,
            jnp.where(row_fwd, jnp.exp(tot_r - cum_r), jnp.exp(excl_r)) * dt_r,
            jnp.where(row_fwd, jnp.exp(cum_r), jnp.exp(tot_r - excl_r)),
        )
        for c in range(nc):
            for q, val in enumerate(to_cols):
                tblk_ref[c, q * HP:(q + 1) * HP, :] = val[c * HP:(c + 1) * HP, :]
            tblk_ref[c, len(to_cols) * HP:, :] = jnp.zeros((CHUNK - len(to_cols) * HP, CHUNK), F32)

        def phase_a(c, carry):
            rows = chunk_rows(c)
            rrows = pl.ds(pl.multiple_of(c * HP, HP), HP)
            blk_t = tblk_ref[c].T
            blk_tb = blk_t.astype(BF16)
            tb_ref[c] = blk_tb
            dt_row = rowq_ref[0, rrows, :]
            cf_row = rowq_ref[1, rrows, :]
            eb_row = rowq_ref[2, rrows, :]

            bmat = conv_tile(c, gw, cwb_ref, cbb_ref, 0).astype(BF16)
            cmat = conv_tile(c, gw + D_STATE, cwc_ref, cbc_ref, 0).astype(BF16)
            cc_ref[rows, :] = cmat
            cb = lax.dot_general(cmat, bmat, (((1,), (1,)), ((), ())), preferred_element_type=F32)

            def mix(h):
                arg = jnp.where(below, blk_t[:, h:h + 1] - cf_row[h:h + 1, :],
                                eb_row[hpg + h:hpg + h + 1, :] - blk_t[:, hpg + h:hpg + h + 1])
                e = jnp.exp(arg) + jnp.where(on_diag, dt_row[h:h + 1, :], 0.0)
                return (cb * e).astype(BF16)

            w_exp = jnp.dot(blk_tb, expand_w, preferred_element_type=F32)

            xs_f, xs_b = [], []
            for t in range(n_tiles):
                lanes = slice(t * LANES, (t + 1) * LANES)
                xf = conv_tile(c, 0, cwx_ref, cbx_ref, t)
                xt = xf.astype(BF16)
                zero = jnp.zeros_like(xt)
                rhs = jnp.concatenate([jnp.where(lane_lo, xt, zero), jnp.where(lane_lo, zero, xt)],
                                      axis=0)
                lhs = jnp.concatenate([mix(2 * t), mix(2 * t + 1)], axis=1)
                yd_ref[rows, lanes] = (jnp.dot(lhs, rhs, preferred_element_type=F32)
                                       + dsk_ref[:, lanes] * xf)
                xs_f.append((xf * w_exp[:, t * LANES:(t + 1) * LANES]).astype(BF16))
                xs_b.append((xf * w_exp[:, gw + t * LANES:gw + (t + 1) * LANES]).astype(BF16))
            contrib_ref[c] = lax.dot_general(bmat, jnp.concatenate(xs_f + xs_b, axis=1),
                                             (((0,), (0,)), ((), ())), preferred_element_type=F32)
            dec_ref[c] = _expand_rows(jnp.exp(blk_t[0:1, HP:2 * HP]), 2 * n_tiles, lane_lo)
            return carry

        lax.fori_loop(0, nc, phase_a, 0, unroll=min(nc, 4))

        def phase_b(k, carry):
            c = nc - 1 - k
            sb = sb_ref[...]
            sball_ref[c] = sb.astype(BF16)
            sb_ref[...] = sb * dec_ref[c][:, gw:] + contrib_ref[c, :, gw:]
            return carry

        lax.fori_loop(0, nc, phase_b, 0, unroll=min(nc, 4))

        def phase_c(c, carry):
            rows = chunk_rows(c)
            sf = sf_ref[...]
            if y_ref is not None:
                s_cat = jnp.concatenate([sf.astype(BF16), sball_ref[c]], axis=1)
                cs = jnp.dot(cc_ref[rows, :], s_cat, preferred_element_type=F32)
                off = jnp.dot(tb_ref[c], expand_off, preferred_element_type=F32)
                g_tiles = []
                ssq = jnp.zeros((CHUNK, 1), F32)
                for t in range(n_tiles):
                    lanes = slice(t * LANES, (t + 1) * LANES)
                    blanes = slice(gw + t * LANES, gw + (t + 1) * LANES)
                    y = (yd_ref[rows, lanes] + cs[:, lanes] * off[:, lanes]
                         + cs[:, blanes] * off[:, blanes])
                    g = y * _silu(z_ref[rows, lanes].astype(F32))
                    ssq = ssq + jnp.sum(g * g, axis=-1, keepdims=True)
                    g_tiles.append(g)
                inv = lax.rsqrt(ssq * (1.0 / gw) + EPS)
                for t in range(n_tiles):
                    lanes = slice(t * LANES, (t + 1) * LANES)
                    y_ref[rows, lanes] = ((g_tiles[t] * inv) * nw_ref[:, lanes]).astype(BF16)
            sf_ref[...] = sf * dec_ref[c][:, :gw] + contrib_ref[c, :, :gw]
            return carry

        lax.fori_loop(0, nc, phase_c, 0, unroll=min(nc, 8))

    sf_ref[...] = jnp.zeros(sf_ref.shape, F32)
    sb_ref[...] = jnp.zeros(sb_ref.shape, F32)
    for (x_ref, b_ref, c_ref, z_ref, dtr_ref, pr_ref), y_ref, seg in zip(seg_refs, y_refs, segments):
        run_segment(x_ref, b_ref, c_ref, z_ref, dtr_ref, pr_ref, y_ref, seg)


class _Segment(NamedTuple):
    length: int
    grid_rows: int
    grid_cols: int
    emit_y: bool


def _ssd(token_sets, p_col, conv_w9, conv_b, d_exp, norm_w, d_ssd):
    g = N_BC_GROUPS
    gw = d_ssd // g
    hpg = gw // HEAD_DIM
    bc_first = d_ssd // D_STATE
    pad = HP - 2 * hpg
    p_col = jnp.pad(p_col, ((0, 0), (0, 0), (0, pad)))
    bsz = token_sets[0][0].shape[0]
    segments, args, in_specs, out_shape, out_specs = [], [], [], [], []
    for xbc, zs, dt, grid_rows, grid_cols, emit_y in token_sets:
        length = xbc.shape[1]
        nc = length // CHUNK
        segments.append(_Segment(length, grid_rows, grid_cols, emit_y))
        p_row = jnp.tile(p_col.transpose(0, 2, 1), (1, nc, 1))
        args += [xbc, xbc, xbc, zs, dt, p_row]
        in_specs += [
            pl.BlockSpec((None, length, gw), lambda b, j: (b, 0, j)),
            pl.BlockSpec((None, length, D_STATE), lambda b, j: (b, 0, bc_first + j)),
            pl.BlockSpec((None, length, D_STATE), lambda b, j: (b, 0, bc_first + g + j)),
            pl.BlockSpec((None, length, gw), lambda b, j: (b, 0, j)),
            pl.BlockSpec((None, nc, HP, CHUNK), lambda b, j: (b, 0, j, 0)),
            pl.BlockSpec((None, nc * HP, 2), lambda b, j: (j, 0, 0)),
        ]
        if emit_y:
            out_shape.append(jax.ShapeDtypeStruct((bsz, length, d_ssd), BF16))
            out_specs.append(pl.BlockSpec((None, length, gw), lambda b, j: (b, 0, j)))
    args += [conv_w9, conv_w9, conv_w9, conv_b, conv_b, conv_b, d_exp, norm_w]
    in_specs += [
        pl.BlockSpec((9, gw), lambda b, j: (0, j)),
        pl.BlockSpec((9, D_STATE), lambda b, j: (0, bc_first + j)),
        pl.BlockSpec((9, D_STATE), lambda b, j: (0, bc_first + g + j)),
        pl.BlockSpec((1, gw), lambda b, j: (0, j)),
        pl.BlockSpec((1, D_STATE), lambda b, j: (0, bc_first + j)),
        pl.BlockSpec((1, D_STATE), lambda b, j: (0, bc_first + g + j)),
        pl.BlockSpec((1, gw), lambda b, j: (0, j)),
        pl.BlockSpec((1, gw), lambda b, j: (0, j)),
    ]
    max_len = max(seg.length for seg in segments)
    max_nc = max_len // CHUNK
    conv_rows = max(seg.length + (2 * seg.grid_cols if seg.grid_rows > 1 else 0) for seg in segments)
    conv_w = gw + 2 * D_STATE
    outs = pl.pallas_call(
        functools.partial(_ssd_kernel, segments=tuple(segments), hpg=hpg),
        out_shape=out_shape,
        grid=(bsz, g),
        in_specs=in_specs,
        out_specs=out_specs,
        scratch_shapes=[pltpu.VMEM((conv_rows, conv_w), F32),
                        pltpu.VMEM((conv_rows, conv_w), F32),
                        pltpu.VMEM((conv_rows, conv_w), F32),
                        pltpu.VMEM((max_len, D_STATE), BF16),
                        pltpu.VMEM((3, max_nc * HP, CHUNK), F32),
                        pltpu.VMEM((max_nc, CHUNK, CHUNK), F32),
                        pltpu.VMEM((max_nc, CHUNK, CHUNK), BF16),
                        pltpu.VMEM((max_len, gw), F32),
                        pltpu.VMEM((max_nc, D_STATE, 2 * gw), F32),
                        pltpu.VMEM((max_nc, 1, 2 * gw), F32),
                        pltpu.VMEM((max_nc, D_STATE, gw), BF16),
                        pltpu.VMEM((D_STATE, gw), F32),
                        pltpu.VMEM((D_STATE, gw), F32)],
        compiler_params=_cparams("parallel", "parallel"),
        name="ssd",
    )(*args)
    outs = list(outs)
    return [outs.pop(0) if seg.emit_y else None for seg in segments]


def _final_kernel(four_ref, ssd_ref, x_ref, gate_ref, w_ref, fnw_ref, o_ref):
    y = _residual(four_ref, ssd_ref, x_ref, gate_ref, w_ref)
    ms = jnp.mean(y * y, axis=-1, keepdims=True)
    o_ref[...] = (y * lax.rsqrt(ms + EPS)) * fnw_ref[...]


def _final(four, ssd, x, gate, w_out, layer, final_norm_w):
    bsz, length, d = x.shape
    tm = min(length, 1024)
    row_map = lambda b, t: (b, t, 0)
    return pl.pallas_call(
        _final_kernel,
        out_shape=jax.ShapeDtypeStruct((bsz, length, d), F32),
        grid=(bsz, length // tm),
        in_specs=[
            pl.BlockSpec((None, tm, four.shape[2]), row_map),
            pl.BlockSpec((None, tm, ssd.shape[2]), row_map),
            pl.BlockSpec((None, tm, d), row_map),
            pl.BlockSpec((None, 1, d), lambda b, t: (b, 0, 0)),
            _resident(w_out.shape[1:], layer),
            _resident((1, d)),
        ],
        out_specs=pl.BlockSpec((None, tm, d), row_map),
        compiler_params=_cparams("parallel", "parallel"),
        name="final",
    )(four, ssd, x, gate, w_out, final_norm_w)


def kernel(x, c, ctx, c_ctx, norm_w, w_ada, b_ada, w_in, conv_w, conv_b, dt_bias, a_log, d_skip,
           ssd_norm_w, w_fourier, b_fourier, w_out, final_norm_w):
    bsz, seq_len, d = x.shape
    ctx_len = ctx.shape[1]
    depth = w_in.shape[0]
    d_f = w_fourier.shape[1]
    d_ssd = ssd_norm_w.shape[1]
    conv_ch = conv_w.shape[-1]
    n_heads = d_skip.shape[1]
    g = N_BC_GROUPS
    hpg = n_heads // g
    splits = (d_f, d_f, d_ssd, conv_ch)

    pad_rows = (-(bsz + 1)) % 8
    cc = jnp.concatenate([c, c_ctx[None, :], jnp.zeros((pad_rows, d), F32)], axis=0)
    mod = _adaln(cc, w_ada, b_ada)

    w_in_b = w_in.astype(BF16)
    w_dt = w_in[:, :, sum(splits):].reshape(depth, d, 2, g, hpg).transpose(0, 3, 2, 4, 1)
    w_dt = jnp.pad(w_dt.reshape(depth, g, 2 * hpg, d), ((0, 0), (0, 0), (0, HP - 2 * hpg), (0, 0)))
    w_dt = w_dt.reshape(depth, g * HP, d).astype(BF16)
    w_out_b = w_out.astype(BF16)
    w_f_b = w_fourier.astype(BF16)
    w9 = conv_w.reshape(depth, 9, conv_ch)
    par = jnp.stack([dt_bias, a_log], axis=1)
    p_col = par.reshape(depth, 2, 2, g, hpg).transpose(0, 3, 1, 2, 4).reshape(depth, g, 2, 2 * hpg)
    d_exp = jnp.repeat(d_skip, HEAD_DIM, axis=1).reshape(depth, 1, d_ssd)

    fseq_x, fch = (jnp.asarray(t).astype(BF16) for t in _dft_tables(seq_len, d_f))
    fseq_c = jnp.asarray(_dft_tables(ctx_len, d_f)[0]).astype(BF16)
    rows = seq_len // GRID_W

    prev_c = prev_x = None
    for i in range(depth):
        last = i == depth - 1
        m = mod[i]
        shift, scale, gate = (m[:bsz, None, k * d:(k + 1) * d] for k in range(3))
        shift_c, scale_c, gate_c = (m[bsz:bsz + 1, None, k * d:(k + 1) * d] for k in range(3))
        nw_i = norm_w[i].reshape(1, d)
        outs_c = _inproj(ctx, shift_c, scale_c, nw_i, w_in_b, w_dt, i, splits, prev_c)
        outs_x = _inproj(x, shift, scale, nw_i, w_in_b, w_dt, i, splits, prev_x)
        if i > 0:
            ctx, outs_c = outs_c[0], outs_c[1:]
            x, outs_x = outs_x[0], outs_x[1:]
        (u_c, zf_c, zs_c, xbc_c, dt_c), (u_x, zf_x, zs_x, xbc_x, dt_x) = outs_c, outs_x
        ssd_c, ssd_x = _ssd(
            [(xbc_c, zs_c, dt_c, 1, ctx_len, not last), (xbc_x, zs_x, dt_x, rows, GRID_W, True)],
            p_col[i], w9[i], conv_b[i].reshape(1, conv_ch), d_exp[i],
            ssd_norm_w[i].reshape(1, d_ssd), d_ssd)
        b_f = b_fourier[i].reshape(1, d_f)
        four_x = _fourier(u_x, zf_x, fseq_x, fch, w_f_b, i, b_f)
        prev_x = (four_x, ssd_x, gate, w_out_b)
        if not last:
            four_c = _fourier(u_c, zf_c, fseq_c, fch, w_f_b, i, b_f)
            prev_c = (four_c, ssd_c, gate_c, w_out_b)
    four_x, ssd_x, gate, w_o = prev_x
    return _final(four_x, ssd_x, x, gate, w_o, depth - 1, final_norm_w.reshape(1, d))
```

```python
import functools
import math
from typing import NamedTuple

import jax
import jax.numpy as jnp
import numpy as np
from jax import lax
from jax.experimental import pallas as pl
from jax.experimental.pallas import tpu as pltpu

F32 = jnp.float32
BF16 = jnp.bfloat16

GRID_W = 64
FOURIER_GROUP = 64
HEAD_DIM = 64
N_BC_GROUPS = 4
D_STATE = 128
CHUNK = 128
EPS = 1e-6
LOG2E = 1.4426950408889634

LANES = 128
BF16_SUBLANES = 16
MXU_WIDTH = 256
VMEM_LIMIT_BYTES = 56 * 1024 * 1024

HEADS_PER_TILE = LANES // HEAD_DIM


def _cparams(*sem):
    return pltpu.CompilerParams(dimension_semantics=sem, vmem_limit_bytes=VMEM_LIMIT_BYTES)


def _silu(v):
    return v * (1.0 / (1.0 + jnp.exp2(v * (-LOG2E))))


def _softplus(v):
    return jnp.maximum(v, 0.0) + jnp.log1p(jnp.exp(-jnp.abs(v)))


def _resident(shape, layer=None):
    zeros = (0,) * len(shape)
    if layer is None:
        return pl.BlockSpec(shape, lambda *_: zeros, pipeline_mode=pl.Buffered(1))
    return pl.BlockSpec((None,) + tuple(shape), lambda *_: (layer,) + zeros,
                        pipeline_mode=pl.Buffered(1))


def _adaln_kernel(c_ref, w_ref, b_ref, o_ref):
    s = _silu(c_ref[...])
    o_ref[...] = jnp.dot(s, w_ref[...], preferred_element_type=F32,
                         precision=lax.Precision.HIGHEST) + b_ref[...]


def _adaln(cc, w_ada, b_ada):
    depth, d, d3 = w_ada.shape
    rows = cc.shape[0]
    tn = 1024
    return pl.pallas_call(
        _adaln_kernel,
        out_shape=jax.ShapeDtypeStruct((depth, rows, d3), F32),
        grid=(depth, d3 // tn),
        in_specs=[
            pl.BlockSpec((rows, d), lambda i, j: (0, 0)),
            pl.BlockSpec((None, d, tn), lambda i, j: (i, 0, j)),
            pl.BlockSpec((None, 1, tn), lambda i, j: (i, 0, j)),
        ],
        out_specs=pl.BlockSpec((None, rows, tn), lambda i, j: (i, 0, j)),
        compiler_params=_cparams("parallel", "parallel"),
        name="adaln",
    )(cc, w_ada, b_ada.reshape(depth, 1, d3))


def _project(x, sh_ref, sc_ref, nw_ref, w_ref, wdt_ref, out_refs, dt_ref, splits, col_tile):
    ms = jnp.mean(x * x, axis=-1, keepdims=True)
    h = (x * lax.rsqrt(ms + EPS)) * nw_ref[...]
    h = h * (1.0 + sc_ref[...]) + sh_ref[...]
    col = 0
    for out_ref, width in zip(out_refs, splits):
        for c0 in range(0, width, col_tile):
            out_ref[:, c0:c0 + col_tile] = jnp.dot(
                h, w_ref[:, col + c0:col + c0 + col_tile],
                preferred_element_type=F32).astype(BF16)
        col += width
    dt_t = lax.dot_general(wdt_ref[...], h, (((1,), (1,)), ((), ())), preferred_element_type=F32)
    for k in range(dt_ref.shape[0]):
        dt_ref[k] = dt_t[:, k * CHUNK:(k + 1) * CHUNK]


def _residual(four_ref, ssd_ref, x_ref, gate_ref, wo_ref):
    d_f = four_ref.shape[1]
    acc = jnp.dot(four_ref[...].astype(F32), wo_ref[0:d_f, :], preferred_element_type=F32)
    acc = acc + jnp.dot(ssd_ref[...].astype(F32), wo_ref[d_f:, :], preferred_element_type=F32)
    return x_ref[...] + gate_ref[...] * acc


def _inproj_kernel(x_ref, sh_ref, sc_ref, nw_ref, w_ref, wdt_ref,
                   u_ref, zf_ref, zs_ref, xbc_ref, dt_ref, *, splits, col_tile):
    _project(x_ref[...], sh_ref, sc_ref, nw_ref, w_ref, wdt_ref, (u_ref, zf_ref, zs_ref, xbc_ref),
             dt_ref, splits, col_tile)


def _outproj_inproj_kernel(four_ref, ssd_ref, x_ref, gate_ref, wo_ref, sh_ref, sc_ref, nw_ref, w_ref,
                           wdt_ref, xo_ref, u_ref, zf_ref, zs_ref, xbc_ref, dt_ref, *, splits, col_tile):
    x_new = _residual(four_ref, ssd_ref, x_ref, gate_ref, wo_ref)
    xo_ref[...] = x_new
    _project(x_new, sh_ref, sc_ref, nw_ref, w_ref, wdt_ref, (u_ref, zf_ref, zs_ref, xbc_ref),
             dt_ref, splits, col_tile)


def _inproj(x, shift, scale, norm_w, w_in, w_dt, layer, splits, prev=None):
    bsz, length, d = x.shape
    tm = min(length, 512)
    n_dt = w_dt.shape[1]
    row_map = lambda b, t: (b, t, 0)

    def per_batch(arr):
        return pl.BlockSpec((None, 1, d), (lambda b, t: (b, 0, 0)) if arr.shape[0] > 1
                            else (lambda b, t: (0, 0, 0)))

    out_shape = [jax.ShapeDtypeStruct((bsz, length, w), BF16) for w in splits]
    out_shape.append(jax.ShapeDtypeStruct((bsz, length // CHUNK, n_dt, CHUNK), F32))
    out_specs = [pl.BlockSpec((None, tm, w), row_map) for w in splits]
    out_specs.append(pl.BlockSpec((None, tm // CHUNK, n_dt, CHUNK), lambda b, t: (b, t, 0, 0)))
    in_specs = [pl.BlockSpec((None, tm, d), row_map), per_batch(shift), per_batch(scale),
                _resident((1, d)), _resident(w_in.shape[1:], layer), _resident(w_dt.shape[1:], layer)]
    args = [x, shift, scale, norm_w, w_in, w_dt]
    body = _inproj_kernel
    if prev is not None:
        four, ssd, gate, w_out = prev
        in_specs = [pl.BlockSpec((None, tm, four.shape[2]), row_map),
                    pl.BlockSpec((None, tm, ssd.shape[2]), row_map),
                    in_specs[0], per_batch(gate), _resident(w_out.shape[1:], layer - 1)] + in_specs[1:]
        args = [four, ssd, x, gate, w_out] + args[1:]
        out_shape = [jax.ShapeDtypeStruct((bsz, length, d), F32)] + out_shape
        out_specs = [pl.BlockSpec((None, tm, d), row_map)] + out_specs
        body = _outproj_inproj_kernel
    return pl.pallas_call(
        functools.partial(body, splits=splits, col_tile=512),
        out_shape=out_shape,
        grid=(bsz, length // tm),
        in_specs=in_specs,
        out_specs=out_specs,
        compiler_params=_cparams("parallel", "parallel"),
        name="inproj" if prev is None else "outproj_inproj",
    )(*args)


def _reverse_into(src_ref, src_end, dst_ref, dst0, count):
    blk = LANES
    ri = lax.broadcasted_iota(jnp.int32, (blk, 2 * blk), 0)
    ci = lax.broadcasted_iota(jnp.int32, (blk, 2 * blk), 1)
    rev = ((ri + ci == blk) | ((ri == 0) & (ci == blk))).astype(BF16)
    for i in range(count // blk):
        lo = src_ref[src_end - (i + 1) * blk:src_end - i * blk, :]
        hi = (src_ref[src_end - i * blk:src_end - (i - 1) * blk, :] if i > 0
              else jnp.zeros_like(lo))
        dst_ref[dst0 + i * blk:dst0 + (i + 1) * blk, :] = jnp.dot(
            rev, jnp.concatenate([lo, hi], axis=0), preferred_element_type=F32).astype(BF16)


def _fourier_kernel(u_ref, zf_ref, fseq_ref, fch_ref, wf_ref, bf_ref, o_ref,
                    urev_ref, ucs_ref, mix_ref, mirror_ref, *, row_tile):
    length, width = u_ref.shape
    half = length // 2
    scale = 1.0 / math.sqrt(length)

    _reverse_into(u_ref, length, urev_ref, 0, half)
    mids = []
    for c0 in range(0, width, MXU_WIDTH):
        cols = slice(c0, c0 + MXU_WIDTH)
        uh = u_ref[0:half, cols]
        ur = urev_ref[:, cols]
        cos_ch = fch_ref[cols, cols]
        sin_ch = fch_ref[cols, width + c0:width + c0 + MXU_WIDTH]
        ucs_ref[0:half, cols] = (jnp.dot(uh, cos_ch, preferred_element_type=F32)
                                 + jnp.dot(ur, cos_ch, preferred_element_type=F32)).astype(BF16)
        ucs_ref[half:length, cols] = (jnp.dot(uh, sin_ch, preferred_element_type=F32)
                                      - jnp.dot(ur, sin_ch, preferred_element_type=F32)).astype(BF16)
        mids.append(jnp.dot(u_ref[half:half + BF16_SUBLANES, cols], cos_ch,
                            preferred_element_type=F32)[0:1, :] * scale)
    mid = jnp.concatenate(mids, axis=1)

    for r in range(half // row_tile):
        rows = slice(r * row_tile, (r + 1) * row_tile)
        c_part = jnp.dot(fseq_ref[rows, 0:half], ucs_ref[0:half, :], preferred_element_type=F32)
        s_part = jnp.dot(fseq_ref[rows, half:length], ucs_ref[half:length, :],
                         preferred_element_type=F32)
        k = r * row_tile + lax.broadcasted_iota(jnp.int32, (row_tile, 1), 0)
        base = c_part + (1 - 2 * (k & 1)).astype(F32) * mid
        mix_ref[rows, :] = (base + s_part).astype(BF16)
        mirror_ref[rows, :] = (base - s_part).astype(BF16)
    _reverse_into(mirror_ref, half, mix_ref, half, half)
    li = lax.broadcasted_iota(jnp.int32, (BF16_SUBLANES, half), 1)
    alt = (1 - 2 * (li & 1)).astype(BF16)
    row_half = (jnp.dot(alt, ucs_ref[0:half, :], preferred_element_type=F32)[0:1, :] * scale
                + (1.0 if half % 2 == 0 else -1.0) * mid)
    first = mix_ref[half:half + BF16_SUBLANES, :]
    sub = lax.broadcasted_iota(jnp.int32, first.shape, 0)
    mix_ref[half:half + BF16_SUBLANES, :] = jnp.where(sub == 0, row_half.astype(BF16), first)

    for r in range(length // row_tile):
        rows = slice(r * row_tile, (r + 1) * row_tile)
        lin = jnp.dot(mix_ref[rows, :], wf_ref[...], preferred_element_type=F32) + bf_ref[...]
        o_ref[rows, :] = (lin * _silu(zf_ref[rows, :].astype(F32))).astype(BF16)


def _fourier(u, zf, fseq, fch, w_f, layer, b_f):
    bsz, length, width = u.shape
    half = length // 2
    assert half % LANES == 0, "the block-wise reversal needs L/2 to be a multiple of 128"
    assert width % MXU_WIDTH == 0 and MXU_WIDTH % FOURIER_GROUP == 0
    batch_block = pl.BlockSpec((None, length, width), lambda b: (b, 0, 0))
    return pl.pallas_call(
        functools.partial(_fourier_kernel, row_tile=min(half, 256)),
        out_shape=jax.ShapeDtypeStruct((bsz, length, width), BF16),
        grid=(bsz,),
        in_specs=[batch_block, batch_block, _resident(fseq.shape), _resident(fch.shape),
                  _resident(w_f.shape[1:], layer), _resident(b_f.shape)],
        out_specs=batch_block,
        scratch_shapes=[pltpu.VMEM((half, width), BF16),
                        pltpu.VMEM((length, width), BF16),
                        pltpu.VMEM((length, width), BF16),
                        pltpu.VMEM((half, width), BF16)],
        compiler_params=_cparams("parallel"),
        name="fourier",
    )(u, zf, fseq, fch, w_f, b_f)


@functools.lru_cache(maxsize=None)
def _dft_tables(length, width):
    half = length // 2
    kl = (np.arange(half)[:, None] * np.arange(half)[None, :]) % length
    ang = kl * (2.0 * np.pi / length)
    fseq = np.concatenate([np.cos(ang), -np.sin(ang)], axis=1) / np.sqrt(length)
    ch = np.arange(width)
    same = (ch[:, None] // FOURIER_GROUP) == (ch[None, :] // FOURIER_GROUP)
    angc = (((ch[:, None] % FOURIER_GROUP) * (ch[None, :] % FOURIER_GROUP)) % FOURIER_GROUP) * (
        2.0 * np.pi / FOURIER_GROUP)
    fch = np.concatenate([np.where(same, np.cos(angc), 0.0),
                          np.where(same, np.sin(angc), 0.0)], axis=1) / np.sqrt(FOURIER_GROUP)
    return fseq.astype(np.float32), fch.astype(np.float32)


HP = BF16_SUBLANES
CONV_SUB = 128


def _split2(v):
    hi = v.astype(BF16)
    return hi, (v - hi.astype(F32)).astype(BF16)


def _split3(v):
    hi, _ = _split2(v)
    r1 = v - hi.astype(F32)
    mid, lo = _split2(r1)
    return hi, mid, lo


def _expand_rows(cols, n_tiles, lane_lo):
    tiles = [jnp.where(lane_lo, cols[:, 2 * t:2 * t + 1], cols[:, 2 * t + 1:2 * t + 2])
             for t in range(n_tiles)]
    return jnp.concatenate(tiles, axis=1)


def _ssd_kernel(*refs, segments, hpg):
    n_seg = len(segments)
    seg_refs = [refs[6 * i:6 * i + 6] for i in range(n_seg)]
    pos = 6 * n_seg
    cwx_ref, cwb_ref, cwc_ref, cbx_ref, cbb_ref, cbc_ref, dsk_ref, nw_ref = refs[pos:pos + 8]
    pos += 8
    y_refs = []
    for seg in segments:
        y_refs.append(refs[pos] if seg.emit_y else None)
        pos += 1 if seg.emit_y else 0
    (sm_ref, s0_ref, sp_ref, cc_ref, rowq_ref, tblk_ref, tb_ref, yd_ref, contrib_ref, dec_ref,
     sball_ref, sf_ref, sb_ref) = refs[pos:]

    gw = hpg * HEAD_DIM
    n_tiles = gw // LANES
    conv_w = s0_ref.shape[1]
    shifted = (sm_ref, s0_ref, sp_ref)
    ri = lax.broadcasted_iota(jnp.int32, (CHUNK, CHUNK), 0)
    ci = lax.broadcasted_iota(jnp.int32, (CHUNK, CHUNK), 1)
    tri_t = (ri <= ci).astype(BF16)
    below = ci < ri
    on_diag = ci == ri
    lane_lo = lax.broadcasted_iota(jnp.int32, (1, LANES), 1) < HEAD_DIM
    erow = lax.broadcasted_iota(jnp.int32, (CHUNK, 2 * gw), 0)
    ehead = lax.broadcasted_iota(jnp.int32, (CHUNK, 2 * gw), 1) // HEAD_DIM
    expand_w = (erow == ehead + 2 * HP).astype(BF16)
    expand_off = (erow == ehead + 3 * HP).astype(BF16)

    def chunk_rows(c):
        return pl.ds(pl.multiple_of(c * CHUNK, CHUNK), CHUNK)

    def run_segment(x_ref, b_ref, c_ref, z_ref, dtr_ref, pr_ref, y_ref, seg):
        length, grid_rows, grid_cols = seg.length, seg.grid_rows, seg.grid_cols
        nc = length // CHUNK
        nrow = nc * HP

        halo = grid_cols if grid_rows > 1 else 0
        if halo:
            zeros = jnp.zeros((halo, conv_w), F32)
            for s_ref in shifted:
                s_ref[0:halo, :] = zeros
                s_ref[halo + length:2 * halo + length, :] = zeros
        sources = ((x_ref, 0, gw), (b_ref, gw, D_STATE), (c_ref, gw + D_STATE, D_STATE))

        def shift_row(r, carry):
            src_rows = pl.ds(pl.multiple_of(r * grid_cols, grid_cols), grid_cols)
            dst_rows = pl.ds(pl.multiple_of(halo + r * grid_cols, grid_cols), grid_cols)
            for ref, lane0, width in sources:
                wpos = lax.broadcasted_iota(jnp.int32, (grid_cols, width), 0)
                xr = ref[src_rows, :].astype(F32)
                dst = slice(lane0, lane0 + width)
                s0_ref[dst_rows, dst] = xr
                sm_ref[dst_rows, dst] = jnp.where(wpos == 0, 0.0, pltpu.roll(xr, 1, axis=0))
                sp_ref[dst_rows, dst] = jnp.where(wpos == grid_cols - 1, 0.0,
                                                  pltpu.roll(xr, grid_cols - 1, axis=0))
            return carry

        lax.fori_loop(0, grid_rows, shift_row, 0)
        dhs = (-1, 0, 1) if grid_rows > 1 else (0,)

        def conv_tile(c, lane0, w_ref, b_ref_, t):
            wl = slice(t * LANES, (t + 1) * LANES)
            sl = slice(lane0 + t * LANES, lane0 + (t + 1) * LANES)
            bias = b_ref_[:, wl]
            outs = []
            for s in range(CHUNK // CONV_SUB):
                base = pl.multiple_of(c * CHUNK, CHUNK) + s * CONV_SUB
                acc = bias
                for dh in dhs:
                    off = halo + base + dh * grid_cols
                    for kw, s_ref in enumerate(shifted):
                        k = 3 * (dh + 1) + kw
                        acc = acc + w_ref[k:k + 1, wl] * s_ref[pl.ds(off, CONV_SUB), sl]
                outs.append(_silu(acc))
            return jnp.concatenate(outs, axis=0)

        row_fwd = (lax.broadcasted_iota(jnp.int32, (nrow, 1), 0) & (HP - 1)) < hpg
        dt_r = _softplus(dtr_ref[...].reshape(nrow, CHUNK) + pr_ref[:, 0:1])
        a_r = dt_r * (-jnp.exp(pr_ref[:, 1:2]))
        cum_r = sum(jnp.dot(p, tri_t, preferred_element_type=F32) for p in _split3(a_r))
        excl_r = cum_r - a_r
        tot_r = jnp.broadcast_to(cum_r[:, CHUNK - 1:CHUNK], cum_r.shape)
        log_dt = jnp.log(dt_r)
        rowq_ref[0, 0:nrow, :] = dt_r
        rowq_ref[1, 0:nrow, :] = cum_r - log_dt
        rowq_ref[2, 0:nrow, :] = excl_r + log_dt
        to_cols = (
            jnp.where(row_fwd, cum_r, excl_r),
            tot_r,
            jnp.where(row_fwd, jnp.exp(tot_r - cum_r), jnp.exp(excl_r)) * dt_r,
            jnp.where(row_fwd, jnp.exp(cum_r), jnp.exp(tot_r - excl_r)),
        )
        for c in range(nc):
            for q, val in enumerate(to_cols):
                tblk_ref[c, q * HP:(q + 1) * HP, :] = val[c * HP:(c + 1) * HP, :]
            tblk_ref[c, len(to_cols) * HP:, :] = jnp.zeros((CHUNK - len(to_cols) * HP, CHUNK), F32)

        def phase_a(c, carry):
            rows = chunk_rows(c)
            rrows = pl.ds(pl.multiple_of(c * HP, HP), HP)
            blk_t = tblk_ref[c].T
            blk_tb = blk_t.astype(BF16)
            tb_ref[c] = blk_tb
            dt_row = rowq_ref[0, rrows, :]
            cf_row = rowq_ref[1, rrows, :]
            eb_row = rowq_ref[2, rrows, :]

            bmat = conv_tile(c, gw, cwb_ref, cbb_ref, 0).astype(BF16)
            cmat = conv_tile(c, gw + D_STATE, cwc_ref, cbc_ref, 0).astype(BF16)
            cc_ref[rows, :] = cmat
            cb = lax.dot_general(cmat, bmat, (((1,), (1,)), ((), ())), preferred_element_type=F32)

            def mix(h):
                arg = jnp.where(below, blk_t[:, h:h + 1] - cf_row[h:h + 1, :],
                                eb_row[hpg + h:hpg + h + 1, :] - blk_t[:, hpg + h:hpg + h + 1])
                e = jnp.exp(arg) + jnp.where(on_diag, dt_row[h:h + 1, :], 0.0)
                return (cb * e).astype(BF16)

            w_exp = jnp.dot(blk_tb, expand_w, preferred_element_type=F32)

            xs_f, xs_b = [], []
            for t in range(n_tiles):
                lanes = slice(t * LANES, (t + 1) * LANES)
                xf = conv_tile(c, 0, cwx_ref, cbx_ref, t)
                xt = xf.astype(BF16)
                zero = jnp.zeros_like(xt)
                rhs = jnp.concatenate([jnp.where(lane_lo, xt, zero), jnp.where(lane_lo, zero, xt)],
                                      axis=0)
                lhs = jnp.concatenate([mix(2 * t), mix(2 * t + 1)], axis=1)
                yd_ref[rows, lanes] = (jnp.dot(lhs, rhs, preferred_element_type=F32)
                                       + dsk_ref[:, lanes] * xf)
                xs_f.append((xf * w_exp[:, t * LANES:(t + 1) * LANES]).astype(BF16))
                xs_b.append((xf * w_exp[:, gw + t * LANES:gw + (t + 1) * LANES]).astype(BF16))
            contrib_ref[c] = lax.dot_general(bmat, jnp.concatenate(xs_f + xs_b, axis=1),
                                             (((0,), (0,)), ((), ())), preferred_element_type=F32)
            dec_ref[c] = _expand_rows(jnp.exp(blk_t[0:1, HP:2 * HP]), 2 * n_tiles, lane_lo)
            return carry

        lax.fori_loop(0, nc, phase_a, 0, unroll=min(nc, 4))

        def phase_b(k, carry):
            c = nc - 1 - k
            sb = sb_ref[...]
            sball_ref[c] = sb.astype(BF16)
            sb_ref[...] = sb * dec_ref[c][:, gw:] + contrib_ref[c, :, gw:]
            return carry

        lax.fori_loop(0, nc, phase_b, 0, unroll=min(nc, 4))

        def phase_c(c, carry):
            rows = chunk_rows(c)
            sf = sf_ref[...]
            if y_ref is not None:
                s_cat = jnp.concatenate([sf.astype(BF16), sball_ref[c]], axis=1)
                cs = jnp.dot(cc_ref[rows, :], s_cat, preferred_element_type=F32)
                off = jnp.dot(tb_ref[c], expand_off, preferred_element_type=F32)
                g_tiles = []
                ssq = jnp.zeros((CHUNK, 1), F32)
                for t in range(n_tiles):
                    lanes = slice(t * LANES, (t + 1) * LANES)
                    blanes = slice(gw + t * LANES, gw + (t + 1) * LANES)
                    y = (yd_ref[rows, lanes] + cs[:, lanes] * off[:, lanes]
                         + cs[:, blanes] * off[:, blanes])
                    g = y * _silu(z_ref[rows, lanes].astype(F32))
                    ssq = ssq + jnp.sum(g * g, axis=-1, keepdims=True)
                    g_tiles.append(g)
                inv = lax.rsqrt(ssq * (1.0 / gw) + EPS)
                for t in range(n_tiles):
                    lanes = slice(t * LANES, (t + 1) * LANES)
                    y_ref[rows, lanes] = ((g_tiles[t] * inv) * nw_ref[:, lanes]).astype(BF16)
            sf_ref[...] = sf * dec_ref[c][:, :gw] + contrib_ref[c, :, :gw]
            return carry

        lax.fori_loop(0, nc, phase_c, 0, unroll=min(nc, 8))

    sf_ref[...] = jnp.zeros(sf_ref.shape, F32)
    sb_ref[...] = jnp.zeros(sb_ref.shape, F32)
    for (x_ref, b_ref, c_ref, z_ref, dtr_ref, pr_ref), y_ref, seg in zip(seg_refs, y_refs, segments):
        run_segment(x_ref, b_ref, c_ref, z_ref, dtr_ref, pr_ref, y_ref, seg)


class _Segment(NamedTuple):
    length: int
    grid_rows: int
    grid_cols: int
    emit_y: bool


def _ssd(token_sets, p_col, conv_w9, conv_b, d_exp, norm_w, d_ssd):
    g = N_BC_GROUPS
    gw = d_ssd // g
    hpg = gw // HEAD_DIM
    bc_first = d_ssd // D_STATE
    pad = HP - 2 * hpg
    p_col = jnp.pad(p_col, ((0, 0), (0, 0), (0, pad)))
    bsz = token_sets[0][0].shape[0]
    segments, args, in_specs, out_shape, out_specs = [], [], [], [], []
    for xbc, zs, dt, grid_rows, grid_cols, emit_y in token_sets:
        length = xbc.shape[1]
        nc = length // CHUNK
        segments.append(_Segment(length, grid_rows, grid_cols, emit_y))
        p_row = jnp.tile(p_col.transpose(0, 2, 1), (1, nc, 1))
        args += [xbc, xbc, xbc, zs, dt, p_row]
        in_specs += [
            pl.BlockSpec((None, length, gw), lambda b, j: (b, 0, j)),
            pl.BlockSpec((None, length, D_STATE), lambda b, j: (b, 0, bc_first + j)),
            pl.BlockSpec((None, length, D_STATE), lambda b, j: (b, 0, bc_first + g + j)),
            pl.BlockSpec((None, length, gw), lambda b, j: (b, 0, j)),
            pl.BlockSpec((None, nc, HP, CHUNK), lambda b, j: (b, 0, j, 0)),
            pl.BlockSpec((None, nc * HP, 2), lambda b, j: (j, 0, 0)),
        ]
        if emit_y:
            out_shape.append(jax.ShapeDtypeStruct((bsz, length, d_ssd), BF16))
            out_specs.append(pl.BlockSpec((None, length, gw), lambda b, j: (b, 0, j)))
    args += [conv_w9, conv_w9, conv_w9, conv_b, conv_b, conv_b, d_exp, norm_w]
    in_specs += [
        pl.BlockSpec((9, gw), lambda b, j: (0, j)),
        pl.BlockSpec((9, D_STATE), lambda b, j: (0, bc_first + j)),
        pl.BlockSpec((9, D_STATE), lambda b, j: (0, bc_first + g + j)),
        pl.BlockSpec((1, gw), lambda b, j: (0, j)),
        pl.BlockSpec((1, D_STATE), lambda b, j: (0, bc_first + j)),
        pl.BlockSpec((1, D_STATE), lambda b, j: (0, bc_first + g + j)),
        pl.BlockSpec((1, gw), lambda b, j: (0, j)),
        pl.BlockSpec((1, gw), lambda b, j: (0, j)),
    ]
    max_len = max(seg.length for seg in segments)
    max_nc = max_len // CHUNK
    conv_rows = max(seg.length + (2 * seg.grid_cols if seg.grid_rows > 1 else 0) for seg in segments)
    conv_w = gw + 2 * D_STATE
    outs = pl.pallas_call(
        functools.partial(_ssd_kernel, segments=tuple(segments), hpg=hpg),
        out_shape=out_shape,
        grid=(bsz, g),
        in_specs=in_specs,
        out_specs=out_specs,
        scratch_shapes=[pltpu.VMEM((conv_rows, conv_w), F32),
                        pltpu.VMEM((conv_rows, conv_w), F32),
                        pltpu.VMEM((conv_rows, conv_w), F32),
                        pltpu.VMEM((max_len, D_STATE), BF16),
                        pltpu.VMEM((3, max_nc * HP, CHUNK), F32),
                        pltpu.VMEM((max_nc, CHUNK, CHUNK), F32),
                        pltpu.VMEM((max_nc, CHUNK, CHUNK), BF16),
                        pltpu.VMEM((max_len, gw), F32),
                        pltpu.VMEM((max_nc, D_STATE, 2 * gw), F32),
                        pltpu.VMEM((max_nc, 1, 2 * gw), F32),
                        pltpu.VMEM((max_nc, D_STATE, gw), BF16),
                        pltpu.VMEM((D_STATE, gw), F32),
                        pltpu.VMEM((D_STATE, gw), F32)],
        compiler_params=_cparams("parallel", "parallel"),
        name="ssd",
    )(*args)
    outs = list(outs)
    return [outs.pop(0) if seg.emit_y else None for seg in segments]


def _final_kernel(four_ref, ssd_ref, x_ref, gate_ref, w_ref, fnw_ref, o_ref):
    y = _residual(four_ref, ssd_ref, x_ref, gate_ref, w_ref)
    ms = jnp.mean(y * y, axis=-1, keepdims=True)
    o_ref[...] = (y * lax.rsqrt(ms + EPS)) * fnw_ref[...]


def _final(four, ssd, x, gate, w_out, layer, final_norm_w):
    bsz, length, d = x.shape
    tm = min(length, 1024)
    row_map = lambda b, t: (b, t, 0)
    return pl.pallas_call(
        _final_kernel,
        out_shape=jax.ShapeDtypeStruct((bsz, length, d), F32),
        grid=(bsz, length // tm),
        in_specs=[
            pl.BlockSpec((None, tm, four.shape[2]), row_map),
            pl.BlockSpec((None, tm, ssd.shape[2]), row_map),
            pl.BlockSpec((None, tm, d), row_map),
            pl.BlockSpec((None, 1, d), lambda b, t: (b, 0, 0)),
            _resident(w_out.shape[1:], layer),
            _resident((1, d)),
        ],
        out_specs=pl.BlockSpec((None, tm, d), row_map),
        compiler_params=_cparams("parallel", "parallel"),
        name="final",
    )(four, ssd, x, gate, w_out, final_norm_w)


def kernel(x, c, ctx, c_ctx, norm_w, w_ada, b_ada, w_in, conv_w, conv_b, dt_bias, a_log, d_skip,
           ssd_norm_w, w_fourier, b_fourier, w_out, final_norm_w):
    bsz, seq_len, d = x.shape
    ctx_len = ctx.shape[1]
    depth = w_in.shape[0]
    d_f = w_fourier.shape[1]
    d_ssd = ssd_norm_w.shape[1]
    conv_ch = conv_w.shape[-1]
    n_heads = d_skip.shape[1]
    g = N_BC_GROUPS
    hpg = n_heads // g
    splits = (d_f, d_f, d_ssd, conv_ch)

    pad_rows = (-(bsz + 1)) % 8
    cc = jnp.concatenate([c, c_ctx[None, :], jnp.zeros((pad_rows, d), F32)], axis=0)
    mod = _adaln(cc, w_ada, b_ada)

    w_dt = w_in[:, :, sum(splits):].reshape(depth, d, 2, g, hpg).transpose(0, 3, 2, 4, 1)
    w_dt = jnp.pad(w_dt.reshape(depth, g, 2 * hpg, d), ((0, 0), (0, 0), (0, HP - 2 * hpg), (0, 0)))
    w_dt = w_dt.reshape(depth, g * HP, d)
    w_f_b = w_fourier.astype(BF16)
    w9 = conv_w.reshape(depth, 9, conv_ch)
    par = jnp.stack([dt_bias, a_log], axis=1)
    p_col = par.reshape(depth, 2, 2, g, hpg).transpose(0, 3, 1, 2, 4).reshape(depth, g, 2, 2 * hpg)
    d_exp = jnp.repeat(d_skip, HEAD_DIM, axis=1).reshape(depth, 1, d_ssd)

    fseq_x, fch = (jnp.asarray(t).astype(BF16) for t in _dft_tables(seq_len, d_f))
    fseq_c = jnp.asarray(_dft_tables(ctx_len, d_f)[0]).astype(BF16)
    rows = seq_len // GRID_W

    prev_c = prev_x = None
    for i in range(depth):
        last = i == depth - 1
        m = mod[i]
        shift, scale, gate = (m[:bsz, None, k * d:(k + 1) * d] for k in range(3))
        shift_c, scale_c, gate_c = (m[bsz:bsz + 1, None, k * d:(k + 1) * d] for k in range(3))
        nw_i = norm_w[i].reshape(1, d)
        outs_c = _inproj(ctx, shift_c, scale_c, nw_i, w_in, w_dt, i, splits, prev_c)
        outs_x = _inproj(x, shift, scale, nw_i, w_in, w_dt, i, splits, prev_x)
        if i > 0:
            ctx, outs_c = outs_c[0], outs_c[1:]
            x, outs_x = outs_x[0], outs_x[1:]
        (u_c, zf_c, zs_c, xbc_c, dt_c), (u_x, zf_x, zs_x, xbc_x, dt_x) = outs_c, outs_x
        ssd_c, ssd_x = _ssd(
            [(xbc_c, zs_c, dt_c, 1, ctx_len, not last), (xbc_x, zs_x, dt_x, rows, GRID_W, True)],
            p_col[i], w9[i], conv_b[i].reshape(1, conv_ch), d_exp[i],
            ssd_norm_w[i].reshape(1, d_ssd), d_ssd)
        b_f = b_fourier[i].reshape(1, d_f)
        four_x = _fourier(u_x, zf_x, fseq_x, fch, w_f_b, i, b_f)
        prev_x = (four_x, ssd_x, gate, w_out)
        if not last:
            four_c = _fourier(u_c, zf_c, fseq_c, fch, w_f_b, i, b_f)
            prev_c = (four_c, ssd_c, gate_c, w_out)
    four_x, ssd_x, gate, w_o = prev_x
    return _final(four_x, ssd_x, x, gate, w_o, depth - 1, final_norm_w.reshape(1, d))
```

```python
import functools
import math
from typing import NamedTuple

import jax
import jax.numpy as jnp
import numpy as np
from jax import lax
from jax.experimental import pallas as pl
from jax.experimental.pallas import tpu as pltpu

F32 = jnp.float32
BF16 = jnp.bfloat16

GRID_W = 64
FOURIER_GROUP = 64
HEAD_DIM = 64
N_BC_GROUPS = 4
D_STATE = 128
CHUNK = 128
EPS = 1e-6
LOG2E = 1.4426950408889634

LANES = 128
BF16_SUBLANES = 16
MXU_WIDTH = 256
VMEM_LIMIT_BYTES = 56 * 1024 * 1024

HEADS_PER_TILE = LANES // HEAD_DIM


def _cparams(*sem):
    return pltpu.CompilerParams(dimension_semantics=sem, vmem_limit_bytes=VMEM_LIMIT_BYTES)


def _silu(v):
    return v * (1.0 / (1.0 + jnp.exp2(v * (-LOG2E))))


def _softplus(v):
    return jnp.maximum(v, 0.0) + jnp.log1p(jnp.exp(-jnp.abs(v)))


def _resident(shape, layer=None):
    zeros = (0,) * len(shape)
    if layer is None:
        return pl.BlockSpec(shape, lambda *_: zeros, pipeline_mode=pl.Buffered(1))
    return pl.BlockSpec((None,) + tuple(shape), lambda *_: (layer,) + zeros,
                        pipeline_mode=pl.Buffered(1))


def _adaln_kernel(c_ref, w_ref, b_ref, o_ref):
    s = _silu(c_ref[...])
    o_ref[...] = jnp.dot(s, w_ref[...], preferred_element_type=F32,
                         precision=lax.Precision.HIGHEST) + b_ref[...]


def _adaln(cc, w_ada, b_ada):
    depth, d, d3 = w_ada.shape
    rows = cc.shape[0]
    tn = 1024
    return pl.pallas_call(
        _adaln_kernel,
        out_shape=jax.ShapeDtypeStruct((depth, rows, d3), F32),
        grid=(depth, d3 // tn),
        in_specs=[
            pl.BlockSpec((rows, d), lambda i, j: (0, 0)),
            pl.BlockSpec((None, d, tn), lambda i, j: (i, 0, j)),
            pl.BlockSpec((None, 1, tn), lambda i, j: (i, 0, j)),
        ],
        out_specs=pl.BlockSpec((None, rows, tn), lambda i, j: (i, 0, j)),
        compiler_params=_cparams("parallel", "parallel"),
        name="adaln",
    )(cc, w_ada, b_ada.reshape(depth, 1, d3))


def _project(x, sh_ref, sc_ref, nw_ref, wt_ref, wdt_ref, out_refs, dt_ref, splits, col_tile):
    ms = jnp.mean(x * x, axis=-1, keepdims=True)
    h = (x * lax.rsqrt(ms + EPS)) * nw_ref[...]
    h = h * (1.0 + sc_ref[...]) + sh_ref[...]
    col = 0
    for out_ref, width in zip(out_refs, splits):
        for c0 in range(0, width, col_tile):
            out_ref[:, c0:c0 + col_tile] = lax.dot_general(
                h, wt_ref[col + c0:col + c0 + col_tile, :], (((1,), (1,)), ((), ())),
                preferred_element_type=F32).astype(BF16)
        col += width
    dt_t = lax.dot_general(wdt_ref[...], h, (((1,), (1,)), ((), ())), preferred_element_type=F32)
    for k in range(dt_ref.shape[0]):
        dt_ref[k] = dt_t[:, k * CHUNK:(k + 1) * CHUNK]


def _residual(four_ref, ssd_ref, x_ref, gate_ref, wo_ref):
    d_f = four_ref.shape[1]
    acc = jnp.dot(four_ref[...].astype(F32), wo_ref[0:d_f, :], preferred_element_type=F32)
    acc = acc + jnp.dot(ssd_ref[...].astype(F32), wo_ref[d_f:, :], preferred_element_type=F32)
    return x_ref[...] + gate_ref[...] * acc


def _inproj_kernel(x_ref, sh_ref, sc_ref, nw_ref, wt_ref, wdt_ref,
                   u_ref, zf_ref, zs_ref, xbc_ref, dt_ref, *, splits, col_tile):
    _project(x_ref[...], sh_ref, sc_ref, nw_ref, wt_ref, wdt_ref, (u_ref, zf_ref, zs_ref, xbc_ref),
             dt_ref, splits, col_tile)


def _outproj_inproj_kernel(four_ref, ssd_ref, x_ref, gate_ref, wo_ref, sh_ref, sc_ref, nw_ref, wt_ref,
                           wdt_ref, xo_ref, u_ref, zf_ref, zs_ref, xbc_ref, dt_ref, *, splits, col_tile):
    x_new = _residual(four_ref, ssd_ref, x_ref, gate_ref, wo_ref)
    xo_ref[...] = x_new
    _project(x_new, sh_ref, sc_ref, nw_ref, wt_ref, wdt_ref, (u_ref, zf_ref, zs_ref, xbc_ref),
             dt_ref, splits, col_tile)


def _inproj(x, shift, scale, norm_w, w_in_t, w_dt, layer, splits, prev=None):
    bsz, length, d = x.shape
    tm = min(length, 512)
    n_dt = w_dt.shape[1]
    row_map = lambda b, t: (b, t, 0)

    def per_batch(arr):
        return pl.BlockSpec((None, 1, d), (lambda b, t: (b, 0, 0)) if arr.shape[0] > 1
                            else (lambda b, t: (0, 0, 0)))

    out_shape = [jax.ShapeDtypeStruct((bsz, length, w), BF16) for w in splits]
    out_shape.append(jax.ShapeDtypeStruct((bsz, length // CHUNK, n_dt, CHUNK), F32))
    out_specs = [pl.BlockSpec((None, tm, w), row_map) for w in splits]
    out_specs.append(pl.BlockSpec((None, tm // CHUNK, n_dt, CHUNK), lambda b, t: (b, t, 0, 0)))
    in_specs = [pl.BlockSpec((None, tm, d), row_map), per_batch(shift), per_batch(scale),
                _resident((1, d)), _resident(w_in_t.shape[1:], layer), _resident(w_dt.shape[1:], layer)]
    args = [x, shift, scale, norm_w, w_in_t, w_dt]
    body = _inproj_kernel
    if prev is not None:
        four, ssd, gate, w_out = prev
        in_specs = [pl.BlockSpec((None, tm, four.shape[2]), row_map),
                    pl.BlockSpec((None, tm, ssd.shape[2]), row_map),
                    in_specs[0], per_batch(gate), _resident(w_out.shape[1:], layer - 1)] + in_specs[1:]
        args = [four, ssd, x, gate, w_out] + args[1:]
        out_shape = [jax.ShapeDtypeStruct((bsz, length, d), F32)] + out_shape
        out_specs = [pl.BlockSpec((None, tm, d), row_map)] + out_specs
        body = _outproj_inproj_kernel
    return pl.pallas_call(
        functools.partial(body, splits=splits, col_tile=512),
        out_shape=out_shape,
        grid=(bsz, length // tm),
        in_specs=in_specs,
        out_specs=out_specs,
        compiler_params=_cparams("parallel", "parallel"),
        name="inproj" if prev is None else "outproj_inproj",
    )(*args)


def _reverse_into(src_ref, src_end, dst_ref, dst0, count):
    blk = LANES
    ri = lax.broadcasted_iota(jnp.int32, (blk, 2 * blk), 0)
    ci = lax.broadcasted_iota(jnp.int32, (blk, 2 * blk), 1)
    rev = ((ri + ci == blk) | ((ri == 0) & (ci == blk))).astype(BF16)
    for i in range(count // blk):
        lo = src_ref[src_end - (i + 1) * blk:src_end - i * blk, :]
        hi = (src_ref[src_end - i * blk:src_end - (i - 1) * blk, :] if i > 0
              else jnp.zeros_like(lo))
        dst_ref[dst0 + i * blk:dst0 + (i + 1) * blk, :] = jnp.dot(
            rev, jnp.concatenate([lo, hi], axis=0), preferred_element_type=F32).astype(BF16)


def _fourier_kernel(u_ref, zf_ref, fseq_ref, fch_ref, wf_ref, bf_ref, o_ref,
                    urev_ref, ucs_ref, mix_ref, mirror_ref, *, row_tile):
    length, width = u_ref.shape
    half = length // 2
    scale = 1.0 / math.sqrt(length)

    _reverse_into(u_ref, length, urev_ref, 0, half)
    mids = []
    for c0 in range(0, width, MXU_WIDTH):
        cols = slice(c0, c0 + MXU_WIDTH)
        uh = u_ref[0:half, cols]
        ur = urev_ref[:, cols]
        cos_ch = fch_ref[cols, cols]
        sin_ch = fch_ref[cols, width + c0:width + c0 + MXU_WIDTH]
        ucs_ref[0:half, cols] = (jnp.dot(uh, cos_ch, preferred_element_type=F32)
                                 + jnp.dot(ur, cos_ch, preferred_element_type=F32)).astype(BF16)
        ucs_ref[half:length, cols] = (jnp.dot(uh, sin_ch, preferred_element_type=F32)
                                      - jnp.dot(ur, sin_ch, preferred_element_type=F32)).astype(BF16)
        mids.append(jnp.dot(u_ref[half:half + BF16_SUBLANES, cols], cos_ch,
                            preferred_element_type=F32)[0:1, :] * scale)
    mid = jnp.concatenate(mids, axis=1)

    for r in range(half // row_tile):
        rows = slice(r * row_tile, (r + 1) * row_tile)
        c_part = jnp.dot(fseq_ref[rows, 0:half], ucs_ref[0:half, :], preferred_element_type=F32)
        s_part = jnp.dot(fseq_ref[rows, half:length], ucs_ref[half:length, :],
                         preferred_element_type=F32)
        k = r * row_tile + lax.broadcasted_iota(jnp.int32, (row_tile, 1), 0)
        base = c_part + (1 - 2 * (k & 1)).astype(F32) * mid
        mix_ref[rows, :] = (base + s_part).astype(BF16)
        mirror_ref[rows, :] = (base - s_part).astype(BF16)
    _reverse_into(mirror_ref, half, mix_ref, half, half)
    li = lax.broadcasted_iota(jnp.int32, (BF16_SUBLANES, half), 1)
    alt = (1 - 2 * (li & 1)).astype(BF16)
    row_half = (jnp.dot(alt, ucs_ref[0:half, :], preferred_element_type=F32)[0:1, :] * scale
                + (1.0 if half % 2 == 0 else -1.0) * mid)
    first = mix_ref[half:half + BF16_SUBLANES, :]
    sub = lax.broadcasted_iota(jnp.int32, first.shape, 0)
    mix_ref[half:half + BF16_SUBLANES, :] = jnp.where(sub == 0, row_half.astype(BF16), first)

    for r in range(length // row_tile):
        rows = slice(r * row_tile, (r + 1) * row_tile)
        lin = jnp.dot(mix_ref[rows, :], wf_ref[...], preferred_element_type=F32) + bf_ref[...]
        o_ref[rows, :] = (lin * _silu(zf_ref[rows, :].astype(F32))).astype(BF16)


def _fourier(u, zf, fseq, fch, w_f, layer, b_f):
    bsz, length, width = u.shape
    half = length // 2
    assert half % LANES == 0, "the block-wise reversal needs L/2 to be a multiple of 128"
    assert width % MXU_WIDTH == 0 and MXU_WIDTH % FOURIER_GROUP == 0
    batch_block = pl.BlockSpec((None, length, width), lambda b: (b, 0, 0))
    return pl.pallas_call(
        functools.partial(_fourier_kernel, row_tile=min(half, 256)),
        out_shape=jax.ShapeDtypeStruct((bsz, length, width), BF16),
        grid=(bsz,),
        in_specs=[batch_block, batch_block, _resident(fseq.shape), _resident(fch.shape),
                  _resident(w_f.shape[1:], layer), _resident(b_f.shape)],
        out_specs=batch_block,
        scratch_shapes=[pltpu.VMEM((half, width), BF16),
                        pltpu.VMEM((length, width), BF16),
                        pltpu.VMEM((length, width), BF16),
                        pltpu.VMEM((half, width), BF16)],
        compiler_params=_cparams("parallel"),
        name="fourier",
    )(u, zf, fseq, fch, w_f, b_f)


@functools.lru_cache(maxsize=None)
def _dft_tables(length, width):
    half = length // 2
    kl = (np.arange(half)[:, None] * np.arange(half)[None, :]) % length
    ang = kl * (2.0 * np.pi / length)
    fseq = np.concatenate([np.cos(ang), -np.sin(ang)], axis=1) / np.sqrt(length)
    ch = np.arange(width)
    same = (ch[:, None] // FOURIER_GROUP) == (ch[None, :] // FOURIER_GROUP)
    angc = (((ch[:, None] % FOURIER_GROUP) * (ch[None, :] % FOURIER_GROUP)) % FOURIER_GROUP) * (
        2.0 * np.pi / FOURIER_GROUP)
    fch = np.concatenate([np.where(same, np.cos(angc), 0.0),
                          np.where(same, np.sin(angc), 0.0)], axis=1) / np.sqrt(FOURIER_GROUP)
    return fseq.astype(np.float32), fch.astype(np.float32)


HP = BF16_SUBLANES
CONV_SUB = 128


def _split2(v):
    hi = v.astype(BF16)
    return hi, (v - hi.astype(F32)).astype(BF16)


def _split3(v):
    hi, _ = _split2(v)
    r1 = v - hi.astype(F32)
    mid, lo = _split2(r1)
    return hi, mid, lo


def _expand_rows(cols, n_tiles, lane_lo):
    tiles = [jnp.where(lane_lo, cols[:, 2 * t:2 * t + 1], cols[:, 2 * t + 1:2 * t + 2])
             for t in range(n_tiles)]
    return jnp.concatenate(tiles, axis=1)


def _ssd_kernel(*refs, segments, hpg):
    n_seg = len(segments)
    seg_refs = [refs[6 * i:6 * i + 6] for i in range(n_seg)]
    pos = 6 * n_seg
    cwx_ref, cwb_ref, cwc_ref, cbx_ref, cbb_ref, cbc_ref, dsk_ref, nw_ref = refs[pos:pos + 8]
    pos += 8
    y_refs = []
    for seg in segments:
        y_refs.append(refs[pos] if seg.emit_y else None)
        pos += 1 if seg.emit_y else 0
    (sm_ref, s0_ref, sp_ref, cc_ref, rowq_ref, tblk_ref, tb_ref, yd_ref, contrib_ref, dec_ref,
     sball_ref, sf_ref, sb_ref) = refs[pos:]

    gw = hpg * HEAD_DIM
    n_tiles = gw // LANES
    conv_w = s0_ref.shape[1]
    shifted = (sm_ref, s0_ref, sp_ref)
    ri = lax.broadcasted_iota(jnp.int32, (CHUNK, CHUNK), 0)
    ci = lax.broadcasted_iota(jnp.int32, (CHUNK, CHUNK), 1)
    tri_t = (ri <= ci).astype(BF16)
    below = ci < ri
    on_diag = ci == ri
    lane_lo = lax.broadcasted_iota(jnp.int32, (1, LANES), 1) < HEAD_DIM
    erow = lax.broadcasted_iota(jnp.int32, (CHUNK, 2 * gw), 0)
    ehead = lax.broadcasted_iota(jnp.int32, (CHUNK, 2 * gw), 1) // HEAD_DIM
    expand_w = (erow == ehead + 2 * HP).astype(BF16)
    expand_off = (erow == ehead + 3 * HP).astype(BF16)

    def chunk_rows(c):
        return pl.ds(pl.multiple_of(c * CHUNK, CHUNK), CHUNK)

    def run_segment(x_ref, b_ref, c_ref, z_ref, dtr_ref, pr_ref, y_ref, seg):
        length, grid_rows, grid_cols = seg.length, seg.grid_rows, seg.grid_cols
        nc = length // CHUNK
        nrow = nc * HP

        halo = grid_cols if grid_rows > 1 else 0
        if halo:
            zeros = jnp.zeros((halo, conv_w), F32)
            for s_ref in shifted:
                s_ref[0:halo, :] = zeros
                s_ref[halo + length:2 * halo + length, :] = zeros
        sources = ((x_ref, 0, gw), (b_ref, gw, D_STATE), (c_ref, gw + D_STATE, D_STATE))

        def shift_row(r, carry):
            src_rows = pl.ds(pl.multiple_of(r * grid_cols, grid_cols), grid_cols)
            dst_rows = pl.ds(pl.multiple_of(halo + r * grid_cols, grid_cols), grid_cols)
            for ref, lane0, width in sources:
                wpos = lax.broadcasted_iota(jnp.int32, (grid_cols, width), 0)
                xr = ref[src_rows, :].astype(F32)
                dst = slice(lane0, lane0 + width)
                s0_ref[dst_rows, dst] = xr
                sm_ref[dst_rows, dst] = jnp.where(wpos == 0, 0.0, pltpu.roll(xr, 1, axis=0))
                sp_ref[dst_rows, dst] = jnp.where(wpos == grid_cols - 1, 0.0,
                                                  pltpu.roll(xr, grid_cols - 1, axis=0))
            return carry

        lax.fori_loop(0, grid_rows, shift_row, 0)
        dhs = (-1, 0, 1) if grid_rows > 1 else (0,)

        def conv_tile(c, lane0, w_ref, b_ref_, t):
            wl = slice(t * LANES, (t + 1) * LANES)
            sl = slice(lane0 + t * LANES, lane0 + (t + 1) * LANES)
            bias = b_ref_[:, wl]
            outs = []
            for s in range(CHUNK // CONV_SUB):
                base = pl.multiple_of(c * CHUNK, CHUNK) + s * CONV_SUB
                acc = bias
                for dh in dhs:
                    off = halo + base + dh * grid_cols
                    for kw, s_ref in enumerate(shifted):
                        k = 3 * (dh + 1) + kw
                        acc = acc + w_ref[k:k + 1, wl] * s_ref[pl.ds(off, CONV_SUB), sl]
                outs.append(_silu(acc))
            return jnp.concatenate(outs, axis=0)

        row_fwd = (lax.broadcasted_iota(jnp.int32, (nrow, 1), 0) & (HP - 1)) < hpg
        dt_r = _softplus(dtr_ref[...].reshape(nrow, CHUNK) + pr_ref[:, 0:1])
        a_r = dt_r * (-jnp.exp(pr_ref[:, 1:2]))
        cum_r = sum(jnp.dot(p, tri_t, preferred_element_type=F32) for p in _split3(a_r))
        excl_r = cum_r - a_r
        tot_r = jnp.broadcast_to(cum_r[:, CHUNK - 1:CHUNK], cum_r.shape)
        log_dt = jnp.log(dt_r)
        rowq_ref[0, 0:nrow, :] = dt_r
        rowq_ref[1, 0:nrow, :] = cum_r - log_dt
        rowq_ref[2, 0:nrow, :] = excl_r + log_dt
        to_cols = (
            jnp.where(row_fwd, cum_r, excl_r),
            tot_r,
            jnp.where(row_fwd, jnp.exp(tot_r - cum_r), jnp.exp(excl_r)) * dt_r,
            jnp.where(row_fwd, jnp.exp(cum_r), jnp.exp(tot_r - excl_r)),
        )
        for c in range(nc):
            for q, val in enumerate(to_cols):
                tblk_ref[c, q * HP:(q + 1) * HP, :] = val[c * HP:(c + 1) * HP, :]
            tblk_ref[c, len(to_cols) * HP:, :] = jnp.zeros((CHUNK - len(to_cols) * HP, CHUNK), F32)

        def phase_a(c, carry):
            rows = chunk_rows(c)
            rrows = pl.ds(pl.multiple_of(c * HP, HP), HP)
            blk_t = tblk_ref[c].T
            blk_tb = blk_t.astype(BF16)
            tb_ref[c] = blk_tb
            dt_row = rowq_ref[0, rrows, :]
            cf_row = rowq_ref[1, rrows, :]
            eb_row = rowq_ref[2, rrows, :]

            bmat = conv_tile(c, gw, cwb_ref, cbb_ref, 0).astype(BF16)
            cmat = conv_tile(c, gw + D_STATE, cwc_ref, cbc_ref, 0).astype(BF16)
            cc_ref[rows, :] = cmat
            cb = lax.dot_general(cmat, bmat, (((1,), (1,)), ((), ())), preferred_element_type=F32)

            def mix(h):
                arg = jnp.where(below, blk_t[:, h:h + 1] - cf_row[h:h + 1, :],
                                eb_row[hpg + h:hpg + h + 1, :] - blk_t[:, hpg + h:hpg + h + 1])
                e = jnp.exp(arg) + jnp.where(on_diag, dt_row[h:h + 1, :], 0.0)
                return (cb * e).astype(BF16)

            w_exp = jnp.dot(blk_tb, expand_w, preferred_element_type=F32)

            xs_f, xs_b = [], []
            for t in range(n_tiles):
                lanes = slice(t * LANES, (t + 1) * LANES)
                xf = conv_tile(c, 0, cwx_ref, cbx_ref, t)
                xt = xf.astype(BF16)
                zero = jnp.zeros_like(xt)
                rhs = jnp.concatenate([jnp.where(lane_lo, xt, zero), jnp.where(lane_lo, zero, xt)],
                                      axis=0)
                lhs = jnp.concatenate([mix(2 * t), mix(2 * t + 1)], axis=1)
                yd_ref[rows, lanes] = (jnp.dot(lhs, rhs, preferred_element_type=F32)
                                       + dsk_ref[:, lanes] * xf)
                xs_f.append((xf * w_exp[:, t * LANES:(t + 1) * LANES]).astype(BF16))
                xs_b.append((xf * w_exp[:, gw + t * LANES:gw + (t + 1) * LANES]).astype(BF16))
            contrib_ref[c] = lax.dot_general(bmat, jnp.concatenate(xs_f + xs_b, axis=1),
                                             (((0,), (0,)), ((), ())), preferred_element_type=F32)
            dec_ref[c] = _expand_rows(jnp.exp(blk_t[0:1, HP:2 * HP]), 2 * n_tiles, lane_lo)
            return carry

        lax.fori_loop(0, nc, phase_a, 0, unroll=min(nc, 4))

        def phase_b(k, carry):
            c = nc - 1 - k
            sb = sb_ref[...]
            sball_ref[c] = sb.astype(BF16)
            sb_ref[...] = sb * dec_ref[c][:, gw:] + contrib_ref[c, :, gw:]
            return carry

        lax.fori_loop(0, nc, phase_b, 0, unroll=min(nc, 4))

        def phase_c(c, carry):
            rows = chunk_rows(c)
            sf = sf_ref[...]
            if y_ref is not None:
                s_cat = jnp.concatenate([sf.astype(BF16), sball_ref[c]], axis=1)
                cs = jnp.dot(cc_ref[rows, :], s_cat, preferred_element_type=F32)
                off = jnp.dot(tb_ref[c], expand_off, preferred_element_type=F32)
                g_tiles = []
                ssq = jnp.zeros((CHUNK, 1), F32)
                for t in range(n_tiles):
                    lanes = slice(t * LANES, (t + 1) * LANES)
                    blanes = slice(gw + t * LANES, gw + (t + 1) * LANES)
                    y = (yd_ref[rows, lanes] + cs[:, lanes] * off[:, lanes]
                         + cs[:, blanes] * off[:, blanes])
                    g = y * _silu(z_ref[rows, lanes].astype(F32))
                    ssq = ssq + jnp.sum(g * g, axis=-1, keepdims=True)
                    g_tiles.append(g)
                inv = lax.rsqrt(ssq * (1.0 / gw) + EPS)
                for t in range(n_tiles):
                    lanes = slice(t * LANES, (t + 1) * LANES)
                    y_ref[rows, lanes] = ((g_tiles[t] * inv) * nw_ref[:, lanes]).astype(BF16)
            sf_ref[...] = sf * dec_ref[c][:, :gw] + contrib_ref[c, :, :gw]
            return carry

        lax.fori_loop(0, nc, phase_c, 0, unroll=min(nc, 8))

    sf_ref[...] = jnp.zeros(sf_ref.shape, F32)
    sb_ref[...] = jnp.zeros(sb_ref.shape, F32)
    for (x_ref, b_ref, c_ref, z_ref, dtr_ref, pr_ref), y_ref, seg in zip(seg_refs, y_refs, segments):
        run_segment(x_ref, b_ref, c_ref, z_ref, dtr_ref, pr_ref, y_ref, seg)


class _Segment(NamedTuple):
    length: int
    grid_rows: int
    grid_cols: int
    emit_y: bool


def _ssd(token_sets, p_col, conv_w9, conv_b, d_exp, norm_w, d_ssd):
    g = N_BC_GROUPS
    gw = d_ssd // g
    hpg = gw // HEAD_DIM
    bc_first = d_ssd // D_STATE
    pad = HP - 2 * hpg
    p_col = jnp.pad(p_col, ((0, 0), (0, 0), (0, pad)))
    bsz = token_sets[0][0].shape[0]
    segments, args, in_specs, out_shape, out_specs = [], [], [], [], []
    for xbc, zs, dt, grid_rows, grid_cols, emit_y in token_sets:
        length = xbc.shape[1]
        nc = length // CHUNK
        segments.append(_Segment(length, grid_rows, grid_cols, emit_y))
        p_row = jnp.tile(p_col.transpose(0, 2, 1), (1, nc, 1))
        args += [xbc, xbc, xbc, zs, dt, p_row]
        in_specs += [
            pl.BlockSpec((None, length, gw), lambda b, j: (b, 0, j)),
            pl.BlockSpec((None, length, D_STATE), lambda b, j: (b, 0, bc_first + j)),
            pl.BlockSpec((None, length, D_STATE), lambda b, j: (b, 0, bc_first + g + j)),
            pl.BlockSpec((None, length, gw), lambda b, j: (b, 0, j)),
            pl.BlockSpec((None, nc, HP, CHUNK), lambda b, j: (b, 0, j, 0)),
            pl.BlockSpec((None, nc * HP, 2), lambda b, j: (j, 0, 0)),
        ]
        if emit_y:
            out_shape.append(jax.ShapeDtypeStruct((bsz, length, d_ssd), BF16))
            out_specs.append(pl.BlockSpec((None, length, gw), lambda b, j: (b, 0, j)))
    args += [conv_w9, conv_w9, conv_w9, conv_b, conv_b, conv_b, d_exp, norm_w]
    in_specs += [
        pl.BlockSpec((9, gw), lambda b, j: (0, j)),
        pl.BlockSpec((9, D_STATE), lambda b, j: (0, bc_first + j)),
        pl.BlockSpec((9, D_STATE), lambda b, j: (0, bc_first + g + j)),
        pl.BlockSpec((1, gw), lambda b, j: (0, j)),
        pl.BlockSpec((1, D_STATE), lambda b, j: (0, bc_first + j)),
        pl.BlockSpec((1, D_STATE), lambda b, j: (0, bc_first + g + j)),
        pl.BlockSpec((1, gw), lambda b, j: (0, j)),
        pl.BlockSpec((1, gw), lambda b, j: (0, j)),
    ]
    max_len = max(seg.length for seg in segments)
    max_nc = max_len // CHUNK
    conv_rows = max(seg.length + (2 * seg.grid_cols if seg.grid_rows > 1 else 0) for seg in segments)
    conv_w = gw + 2 * D_STATE
    outs = pl.pallas_call(
        functools.partial(_ssd_kernel, segments=tuple(segments), hpg=hpg),
        out_shape=out_shape,
        grid=(bsz, g),
        in_specs=in_specs,
        out_specs=out_specs,
        scratch_shapes=[pltpu.VMEM((conv_rows, conv_w), F32),
                        pltpu.VMEM((conv_rows, conv_w), F32),
                        pltpu.VMEM((conv_rows, conv_w), F32),
                        pltpu.VMEM((max_len, D_STATE), BF16),
                        pltpu.VMEM((3, max_nc * HP, CHUNK), F32),
                        pltpu.VMEM((max_nc, CHUNK, CHUNK), F32),
                        pltpu.VMEM((max_nc, CHUNK, CHUNK), BF16),
                        pltpu.VMEM((max_len, gw), F32),
                        pltpu.VMEM((max_nc, D_STATE, 2 * gw), F32),
                        pltpu.VMEM((max_nc, 1, 2 * gw), F32),
                        pltpu.VMEM((max_nc, D_STATE, gw), BF16),
                        pltpu.VMEM((D_STATE, gw), F32),
                        pltpu.VMEM((D_STATE, gw), F32)],
        compiler_params=_cparams("parallel", "parallel"),
        name="ssd",
    )(*args)
    outs = list(outs)
    return [outs.pop(0) if seg.emit_y else None for seg in segments]


def _final_kernel(four_ref, ssd_ref, x_ref, gate_ref, w_ref, fnw_ref, o_ref):
    y = _residual(four_ref, ssd_ref, x_ref, gate_ref, w_ref)
    ms = jnp.mean(y * y, axis=-1, keepdims=True)
    o_ref[...] = (y * lax.rsqrt(ms + EPS)) * fnw_ref[...]


def _final(four, ssd, x, gate, w_out, layer, final_norm_w):
    bsz, length, d = x.shape
    tm = min(length, 1024)
    row_map = lambda b, t: (b, t, 0)
    return pl.pallas_call(
        _final_kernel,
        out_shape=jax.ShapeDtypeStruct((bsz, length, d), F32),
        grid=(bsz, length // tm),
        in_specs=[
            pl.BlockSpec((None, tm, four.shape[2]), row_map),
            pl.BlockSpec((None, tm, ssd.shape[2]), row_map),
            pl.BlockSpec((None, tm, d), row_map),
            pl.BlockSpec((None, 1, d), lambda b, t: (b, 0, 0)),
            _resident(w_out.shape[1:], layer),
            _resident((1, d)),
        ],
        out_specs=pl.BlockSpec((None, tm, d), row_map),
        compiler_params=_cparams("parallel", "parallel"),
        name="final",
    )(four, ssd, x, gate, w_out, final_norm_w)


def kernel(x, c, ctx, c_ctx, norm_w, w_ada, b_ada, w_in, conv_w, conv_b, dt_bias, a_log, d_skip,
           ssd_norm_w, w_fourier, b_fourier, w_out, final_norm_w):
    bsz, seq_len, d = x.shape
    ctx_len = ctx.shape[1]
    depth = w_in.shape[0]
    d_f = w_fourier.shape[1]
    d_ssd = ssd_norm_w.shape[1]
    conv_ch = conv_w.shape[-1]
    n_heads = d_skip.shape[1]
    g = N_BC_GROUPS
    hpg = n_heads // g
    splits = (d_f, d_f, d_ssd, conv_ch)

    pad_rows = (-(bsz + 1)) % 8
    cc = jnp.concatenate([c, c_ctx[None, :], jnp.zeros((pad_rows, d), F32)], axis=0)
    mod = _adaln(cc, w_ada, b_ada)

    w_in_t = jnp.swapaxes(w_in, 1, 2)
    w_dt = w_in[:, :, sum(splits):].reshape(depth, d, 2, g, hpg).transpose(0, 3, 2, 4, 1)
    w_dt = jnp.pad(w_dt.reshape(depth, g, 2 * hpg, d), ((0, 0), (0, 0), (0, HP - 2 * hpg), (0, 0)))
    w_dt = w_dt.reshape(depth, g * HP, d)
    w_f_b = w_fourier.astype(BF16)
    w9 = conv_w.reshape(depth, 9, conv_ch)
    par = jnp.stack([dt_bias, a_log], axis=1)
    p_col = par.reshape(depth, 2, 2, g, hpg).transpose(0, 3, 1, 2, 4).reshape(depth, g, 2, 2 * hpg)
    d_exp = jnp.repeat(d_skip, HEAD_DIM, axis=1).reshape(depth, 1, d_ssd)

    fseq_x, fch = (jnp.asarray(t).astype(BF16) for t in _dft_tables(seq_len, d_f))
    fseq_c = jnp.asarray(_dft_tables(ctx_len, d_f)[0]).astype(BF16)
    rows = seq_len // GRID_W

    prev_c = prev_x = None
    for i in range(depth):
        last = i == depth - 1
        m = mod[i]
        shift, scale, gate = (m[:bsz, None, k * d:(k + 1) * d] for k in range(3))
        shift_c, scale_c, gate_c = (m[bsz:bsz + 1, None, k * d:(k + 1) * d] for k in range(3))
        nw_i = norm_w[i].reshape(1, d)
        outs_c = _inproj(ctx, shift_c, scale_c, nw_i, w_in_t, w_dt, i, splits, prev_c)
        outs_x = _inproj(x, shift, scale, nw_i, w_in_t, w_dt, i, splits, prev_x)
        if i > 0:
            ctx, outs_c = outs_c[0], outs_c[1:]
            x, outs_x = outs_x[0], outs_x[1:]
        (u_c, zf_c, zs_c, xbc_c, dt_c), (u_x, zf_x, zs_x, xbc_x, dt_x) = outs_c, outs_x
        ssd_c, ssd_x = _ssd(
            [(xbc_c, zs_c, dt_c, 1, ctx_len, not last), (xbc_x, zs_x, dt_x, rows, GRID_W, True)],
            p_col[i], w9[i], conv_b[i].reshape(1, conv_ch), d_exp[i],
            ssd_norm_w[i].reshape(1, d_ssd), d_ssd)
        b_f = b_fourier[i].reshape(1, d_f)
        four_x = _fourier(u_x, zf_x, fseq_x, fch, w_f_b, i, b_f)
        prev_x = (four_x, ssd_x, gate, w_out)
        if not last:
            four_c = _fourier(u_c, zf_c, fseq_c, fch, w_f_b, i, b_f)
            prev_c = (four_c, ssd_c, gate_c, w_out)
    four_x, ssd_x, gate, w_o = prev_x
    return _final(four_x, ssd_x, x, gate, w_o, depth - 1, final_norm_w.reshape(1, d))
```

```python
import functools
import math
from typing import NamedTuple

import jax
import jax.numpy as jnp
import numpy as np
from jax import lax
from jax.experimental import pallas as pl
from jax.experimental.pallas import tpu as pltpu

F32 = jnp.float32
BF16 = jnp.bfloat16

GRID_W = 64
FOURIER_GROUP = 64
HEAD_DIM = 64
N_BC_GROUPS = 4
D_STATE = 128
CHUNK = 128
EPS = 1e-6
LOG2E = 1.4426950408889634

LANES = 128
BF16_SUBLANES = 16
MXU_WIDTH = 256
VMEM_LIMIT_BYTES = 56 * 1024 * 1024

PROJ_ROW_TILE = 512
PROJ_COL_TILE = 512
FINAL_ROW_TILE = 1024
ADALN_COL_TILE = 1024
FOURIER_ROW_TILE = 256

HEADS_PER_TILE = LANES // HEAD_DIM


def _cparams(*sem):
    return pltpu.CompilerParams(dimension_semantics=sem, vmem_limit_bytes=VMEM_LIMIT_BYTES)


def _silu(v):
    return v * (1.0 / (1.0 + jnp.exp2(v * (-LOG2E))))


def _softplus(v):
    return jnp.maximum(v, 0.0) + jnp.log1p(jnp.exp(-jnp.abs(v)))


def _resident(shape, layer=None):
    zeros = (0,) * len(shape)
    if layer is None:
        return pl.BlockSpec(shape, lambda *_: zeros, pipeline_mode=pl.Buffered(1))
    return pl.BlockSpec((None,) + tuple(shape), lambda *_: (layer,) + zeros,
                        pipeline_mode=pl.Buffered(1))


def _adaln_kernel(c_ref, w_ref, b_ref, o_ref):
    s = _silu(c_ref[...])
    o_ref[...] = jnp.dot(s, w_ref[...], preferred_element_type=F32,
                         precision=lax.Precision.HIGHEST) + b_ref[...]


def _adaln(cc, w_ada, b_ada):
    depth, d, d3 = w_ada.shape
    rows = cc.shape[0]
    tn = ADALN_COL_TILE
    return pl.pallas_call(
        _adaln_kernel,
        out_shape=jax.ShapeDtypeStruct((depth, rows, d3), F32),
        grid=(depth, d3 // tn),
        in_specs=[
            pl.BlockSpec((rows, d), lambda i, j: (0, 0)),
            pl.BlockSpec((None, d, tn), lambda i, j: (i, 0, j)),
            pl.BlockSpec((None, 1, tn), lambda i, j: (i, 0, j)),
        ],
        out_specs=pl.BlockSpec((None, rows, tn), lambda i, j: (i, 0, j)),
        compiler_params=_cparams("parallel", "parallel"),
        name="adaln",
    )(cc, w_ada, b_ada.reshape(depth, 1, d3))


def _project(x, sh_ref, sc_ref, nw_ref, wt_ref, wdt_ref, out_refs, dt_ref, splits, col_tile):
    ms = jnp.mean(x * x, axis=-1, keepdims=True)
    h = (x * lax.rsqrt(ms + EPS)) * nw_ref[...]
    h = h * (1.0 + sc_ref[...]) + sh_ref[...]
    col = 0
    for out_ref, width in zip(out_refs, splits):
        for c0 in range(0, width, col_tile):
            out_ref[:, c0:c0 + col_tile] = lax.dot_general(
                h, wt_ref[col + c0:col + c0 + col_tile, :], (((1,), (1,)), ((), ())),
                preferred_element_type=F32).astype(BF16)
        col += width
    dt_t = lax.dot_general(wdt_ref[...], h, (((1,), (1,)), ((), ())), preferred_element_type=F32)
    for k in range(dt_ref.shape[0]):
        dt_ref[k] = dt_t[:, k * CHUNK:(k + 1) * CHUNK]


def _residual(four_ref, ssd_ref, x_ref, gate_ref, wo_ref):
    d_f = four_ref.shape[1]
    acc = jnp.dot(four_ref[...].astype(F32), wo_ref[0:d_f, :], preferred_element_type=F32)
    acc = acc + jnp.dot(ssd_ref[...].astype(F32), wo_ref[d_f:, :], preferred_element_type=F32)
    return x_ref[...] + gate_ref[...] * acc


def _inproj_kernel(x_ref, sh_ref, sc_ref, nw_ref, wt_ref, wdt_ref,
                   u_ref, zf_ref, zs_ref, xbc_ref, dt_ref, *, splits, col_tile):
    _project(x_ref[...], sh_ref, sc_ref, nw_ref, wt_ref, wdt_ref, (u_ref, zf_ref, zs_ref, xbc_ref),
             dt_ref, splits, col_tile)


def _outproj_inproj_kernel(four_ref, ssd_ref, x_ref, gate_ref, wo_ref, sh_ref, sc_ref, nw_ref, wt_ref,
                           wdt_ref, xo_ref, u_ref, zf_ref, zs_ref, xbc_ref, dt_ref, *, splits, col_tile):
    x_new = _residual(four_ref, ssd_ref, x_ref, gate_ref, wo_ref)
    xo_ref[...] = x_new
    _project(x_new, sh_ref, sc_ref, nw_ref, wt_ref, wdt_ref, (u_ref, zf_ref, zs_ref, xbc_ref),
             dt_ref, splits, col_tile)


def _inproj(x, shift, scale, norm_w, w_in_t, w_dt, layer, splits, prev=None):
    bsz, length, d = x.shape
    tm = min(length, PROJ_ROW_TILE)
    n_dt = w_dt.shape[1]
    row_map = lambda b, t: (b, t, 0)

    def per_batch(arr):
        return pl.BlockSpec((None, 1, d), (lambda b, t: (b, 0, 0)) if arr.shape[0] > 1
                            else (lambda b, t: (0, 0, 0)))

    out_shape = [jax.ShapeDtypeStruct((bsz, length, w), BF16) for w in splits]
    out_shape.append(jax.ShapeDtypeStruct((bsz, length // CHUNK, n_dt, CHUNK), F32))
    out_specs = [pl.BlockSpec((None, tm, w), row_map) for w in splits]
    out_specs.append(pl.BlockSpec((None, tm // CHUNK, n_dt, CHUNK), lambda b, t: (b, t, 0, 0)))
    in_specs = [pl.BlockSpec((None, tm, d), row_map), per_batch(shift), per_batch(scale),
                _resident((1, d)), _resident(w_in_t.shape[1:], layer), _resident(w_dt.shape[1:], layer)]
    args = [x, shift, scale, norm_w, w_in_t, w_dt]
    body = _inproj_kernel
    if prev is not None:
        four, ssd, gate, w_out = prev
        in_specs = [pl.BlockSpec((None, tm, four.shape[2]), row_map),
                    pl.BlockSpec((None, tm, ssd.shape[2]), row_map),
                    in_specs[0], per_batch(gate), _resident(w_out.shape[1:], layer - 1)] + in_specs[1:]
        args = [four, ssd, x, gate, w_out] + args[1:]
        out_shape = [jax.ShapeDtypeStruct((bsz, length, d), F32)] + out_shape
        out_specs = [pl.BlockSpec((None, tm, d), row_map)] + out_specs
        body = _outproj_inproj_kernel
    return pl.pallas_call(
        functools.partial(body, splits=splits, col_tile=PROJ_COL_TILE),
        out_shape=out_shape,
        grid=(bsz, length // tm),
        in_specs=in_specs,
        out_specs=out_specs,
        compiler_params=_cparams("parallel", "parallel"),
        name="inproj" if prev is None else "outproj_inproj",
    )(*args)


def _reverse_into(src_ref, src_end, dst_ref, dst0, count):
    blk = LANES
    ri = lax.broadcasted_iota(jnp.int32, (blk, 2 * blk), 0)
    ci = lax.broadcasted_iota(jnp.int32, (blk, 2 * blk), 1)
    rev = ((ri + ci == blk) | ((ri == 0) & (ci == blk))).astype(BF16)
    for i in range(count // blk):
        lo = src_ref[src_end - (i + 1) * blk:src_end - i * blk, :]
        hi = (src_ref[src_end - i * blk:src_end - (i - 1) * blk, :] if i > 0
              else jnp.zeros_like(lo))
        dst_ref[dst0 + i * blk:dst0 + (i + 1) * blk, :] = jnp.dot(
            rev, jnp.concatenate([lo, hi], axis=0), preferred_element_type=F32).astype(BF16)


def _fourier_kernel(u_ref, zf_ref, fseq_ref, fch_ref, wf_ref, bf_ref, o_ref,
                    urev_ref, ucs_ref, mix_ref, mirror_ref, *, row_tile):
    length, width = u_ref.shape
    half = length // 2
    scale = 1.0 / math.sqrt(length)

    _reverse_into(u_ref, length, urev_ref, 0, half)
    mids = []
    for c0 in range(0, width, MXU_WIDTH):
        cols = slice(c0, c0 + MXU_WIDTH)
        uh = u_ref[0:half, cols]
        ur = urev_ref[:, cols]
        cos_ch = fch_ref[cols, cols]
        sin_ch = fch_ref[cols, width + c0:width + c0 + MXU_WIDTH]
        ucs_ref[0:half, cols] = (jnp.dot(uh, cos_ch, preferred_element_type=F32)
                                 + jnp.dot(ur, cos_ch, preferred_element_type=F32)).astype(BF16)
        ucs_ref[half:length, cols] = (jnp.dot(uh, sin_ch, preferred_element_type=F32)
                                      - jnp.dot(ur, sin_ch, preferred_element_type=F32)).astype(BF16)
        mids.append(jnp.dot(u_ref[half:half + BF16_SUBLANES, cols], cos_ch,
                            preferred_element_type=F32)[0:1, :] * scale)
    mid = jnp.concatenate(mids, axis=1)

    for r in range(half // row_tile):
        rows = slice(r * row_tile, (r + 1) * row_tile)
        c_part = jnp.dot(fseq_ref[rows, 0:half], ucs_ref[0:half, :], preferred_element_type=F32)
        s_part = jnp.dot(fseq_ref[rows, half:length], ucs_ref[half:length, :],
                         preferred_element_type=F32)
        k = r * row_tile + lax.broadcasted_iota(jnp.int32, (row_tile, 1), 0)
        base = c_part + (1 - 2 * (k & 1)).astype(F32) * mid
        mix_ref[rows, :] = (base + s_part).astype(BF16)
        mirror_ref[rows, :] = (base - s_part).astype(BF16)
    _reverse_into(mirror_ref, half, mix_ref, half, half)
    li = lax.broadcasted_iota(jnp.int32, (BF16_SUBLANES, half), 1)
    alt = (1 - 2 * (li & 1)).astype(BF16)
    row_half = (jnp.dot(alt, ucs_ref[0:half, :], preferred_element_type=F32)[0:1, :] * scale
                + (1.0 if half % 2 == 0 else -1.0) * mid)
    first = mix_ref[half:half + BF16_SUBLANES, :]
    sub = lax.broadcasted_iota(jnp.int32, first.shape, 0)
    mix_ref[half:half + BF16_SUBLANES, :] = jnp.where(sub == 0, row_half.astype(BF16), first)

    for r in range(length // row_tile):
        rows = slice(r * row_tile, (r + 1) * row_tile)
        lin = jnp.dot(mix_ref[rows, :], wf_ref[...], preferred_element_type=F32) + bf_ref[...]
        o_ref[rows, :] = (lin * _silu(zf_ref[rows, :].astype(F32))).astype(BF16)


def _fourier(u, zf, fseq, fch, w_f, layer, b_f):
    bsz, length, width = u.shape
    half = length // 2
    assert half % LANES == 0, "the block-wise reversal needs L/2 to be a multiple of 128"
    assert width % MXU_WIDTH == 0 and MXU_WIDTH % FOURIER_GROUP == 0
    batch_block = pl.BlockSpec((None, length, width), lambda b: (b, 0, 0))
    return pl.pallas_call(
        functools.partial(_fourier_kernel, row_tile=min(half, FOURIER_ROW_TILE)),
        out_shape=jax.ShapeDtypeStruct((bsz, length, width), BF16),
        grid=(bsz,),
        in_specs=[batch_block, batch_block, _resident(fseq.shape), _resident(fch.shape),
                  _resident(w_f.shape[1:], layer), _resident(b_f.shape)],
        out_specs=batch_block,
        scratch_shapes=[pltpu.VMEM((half, width), BF16),
                        pltpu.VMEM((length, width), BF16),
                        pltpu.VMEM((length, width), BF16),
                        pltpu.VMEM((half, width), BF16)],
        compiler_params=_cparams("parallel"),
        name="fourier",
    )(u, zf, fseq, fch, w_f, b_f)


@functools.lru_cache(maxsize=None)
def _dft_tables(length, width):
    half = length // 2
    kl = (np.arange(half)[:, None] * np.arange(half)[None, :]) % length
    ang = kl * (2.0 * np.pi / length)
    fseq = np.concatenate([np.cos(ang), -np.sin(ang)], axis=1) / np.sqrt(length)
    ch = np.arange(width)
    same = (ch[:, None] // FOURIER_GROUP) == (ch[None, :] // FOURIER_GROUP)
    angc = (((ch[:, None] % FOURIER_GROUP) * (ch[None, :] % FOURIER_GROUP)) % FOURIER_GROUP) * (
        2.0 * np.pi / FOURIER_GROUP)
    fch = np.concatenate([np.where(same, np.cos(angc), 0.0),
                          np.where(same, np.sin(angc), 0.0)], axis=1) / np.sqrt(FOURIER_GROUP)
    return fseq.astype(np.float32), fch.astype(np.float32)


HP = BF16_SUBLANES
CONV_SUB = 128


def _split2(v):
    hi = v.astype(BF16)
    return hi, (v - hi.astype(F32)).astype(BF16)


def _split3(v):
    hi, _ = _split2(v)
    r1 = v - hi.astype(F32)
    mid, lo = _split2(r1)
    return hi, mid, lo


def _expand_rows(cols, n_tiles, lane_lo):
    tiles = [jnp.where(lane_lo, cols[:, 2 * t:2 * t + 1], cols[:, 2 * t + 1:2 * t + 2])
             for t in range(n_tiles)]
    return jnp.concatenate(tiles, axis=1)


def _ssd_kernel(*refs, segments, hpg):
    n_seg = len(segments)
    seg_refs = [refs[6 * i:6 * i + 6] for i in range(n_seg)]
    pos = 6 * n_seg
    cwx_ref, cwb_ref, cwc_ref, cbx_ref, cbb_ref, cbc_ref, dsk_ref, nw_ref = refs[pos:pos + 8]
    pos += 8
    y_refs = []
    for seg in segments:
        y_refs.append(refs[pos] if seg.emit_y else None)
        pos += 1 if seg.emit_y else 0
    (sm_ref, s0_ref, sp_ref, cc_ref, rowq_ref, tblk_ref, tb_ref, yd_ref, contrib_ref, dec_ref,
     sball_ref, sf_ref, sb_ref) = refs[pos:]

    gw = hpg * HEAD_DIM
    n_tiles = gw // LANES
    conv_w = s0_ref.shape[1]
    shifted = (sm_ref, s0_ref, sp_ref)
    ri = lax.broadcasted_iota(jnp.int32, (CHUNK, CHUNK), 0)
    ci = lax.broadcasted_iota(jnp.int32, (CHUNK, CHUNK), 1)
    tri_t = (ri <= ci).astype(BF16)
    below = ci < ri
    on_diag = ci == ri
    lane_lo = lax.broadcasted_iota(jnp.int32, (1, LANES), 1) < HEAD_DIM
    erow = lax.broadcasted_iota(jnp.int32, (CHUNK, 2 * gw), 0)
    ehead = lax.broadcasted_iota(jnp.int32, (CHUNK, 2 * gw), 1) // HEAD_DIM
    expand_w = (erow == ehead + 2 * HP).astype(BF16)
    expand_off = (erow == ehead + 3 * HP).astype(BF16)

    def chunk_rows(c):
        return pl.ds(pl.multiple_of(c * CHUNK, CHUNK), CHUNK)

    def run_segment(x_ref, b_ref, c_ref, z_ref, dtr_ref, pr_ref, y_ref, seg):
        length, grid_rows, grid_cols = seg.length, seg.grid_rows, seg.grid_cols
        nc = length // CHUNK
        nrow = nc * HP

        halo = grid_cols if grid_rows > 1 else 0
        if halo:
            zeros = jnp.zeros((halo, conv_w), BF16)
            for s_ref in shifted:
                s_ref[0:halo, :] = zeros
                s_ref[halo + length:2 * halo + length, :] = zeros
        sources = ((x_ref, 0, gw), (b_ref, gw, D_STATE), (c_ref, gw + D_STATE, D_STATE))

        def shift_row(r, carry):
            src_rows = pl.ds(pl.multiple_of(r * grid_cols, grid_cols), grid_cols)
            dst_rows = pl.ds(pl.multiple_of(halo + r * grid_cols, grid_cols), grid_cols)
            for ref, lane0, width in sources:
                wpos = lax.broadcasted_iota(jnp.int32, (grid_cols, width), 0)
                xb = ref[src_rows, :]
                xr = xb.astype(F32)
                dst = slice(lane0, lane0 + width)
                s0_ref[dst_rows, dst] = xb
                sm_ref[dst_rows, dst] = jnp.where(
                    wpos == 0, 0.0, pltpu.roll(xr, 1, axis=0)).astype(BF16)
                sp_ref[dst_rows, dst] = jnp.where(
                    wpos == grid_cols - 1, 0.0, pltpu.roll(xr, grid_cols - 1, axis=0)).astype(BF16)
            return carry

        lax.fori_loop(0, grid_rows, shift_row, 0)
        dhs = (-1, 0, 1) if grid_rows > 1 else (0,)

        def conv_tile(c, lane0, w_ref, b_ref_, t):
            wl = slice(t * LANES, (t + 1) * LANES)
            sl = slice(lane0 + t * LANES, lane0 + (t + 1) * LANES)
            bias = b_ref_[:, wl]
            outs = []
            for s in range(CHUNK // CONV_SUB):
                base = pl.multiple_of(c * CHUNK, CHUNK) + s * CONV_SUB
                acc = bias
                for dh in dhs:
                    off = halo + base + dh * grid_cols
                    part = None
                    for kw, s_ref in enumerate(shifted):
                        k = 3 * (dh + 1) + kw
                        term = w_ref[k:k + 1, wl].astype(BF16) * s_ref[pl.ds(off, CONV_SUB), sl]
                        part = term if part is None else part + term
                    acc = acc + part.astype(F32)
                outs.append(_silu(acc))
            return jnp.concatenate(outs, axis=0)

        row_fwd = (lax.broadcasted_iota(jnp.int32, (nrow, 1), 0) & (HP - 1)) < hpg
        dt_r = _softplus(dtr_ref[...].reshape(nrow, CHUNK) + pr_ref[:, 0:1])
        a_r = dt_r * (-jnp.exp(pr_ref[:, 1:2]))
        cum_r = sum(jnp.dot(p, tri_t, preferred_element_type=F32) for p in _split3(a_r))
        excl_r = cum_r - a_r
        tot_r = jnp.broadcast_to(cum_r[:, CHUNK - 1:CHUNK], cum_r.shape)
        log_dt = jnp.log(dt_r)
        rowq_ref[0, 0:nrow, :] = dt_r
        rowq_ref[1, 0:nrow, :] = cum_r - log_dt
        rowq_ref[2, 0:nrow, :] = excl_r + log_dt
        to_cols = (
            jnp.where(row_fwd, cum_r, excl_r),
            tot_r,
            jnp.where(row_fwd, jnp.exp(tot_r - cum_r), jnp.exp(excl_r)) * dt_r,
            jnp.where(row_fwd, jnp.exp(cum_r), jnp.exp(tot_r - excl_r)),
        )
        for c in range(nc):
            for q, val in enumerate(to_cols):
                tblk_ref[c, q * HP:(q + 1) * HP, :] = val[c * HP:(c + 1) * HP, :]
            tblk_ref[c, len(to_cols) * HP:, :] = jnp.zeros((CHUNK - len(to_cols) * HP, CHUNK), F32)

        def phase_a(c, carry):
            rows = chunk_rows(c)
            rrows = pl.ds(pl.multiple_of(c * HP, HP), HP)
            blk_t = tblk_ref[c].T
            blk_tb = blk_t.astype(BF16)
            tb_ref[c] = blk_tb
            dt_row = rowq_ref[0, rrows, :]
            cf_row = rowq_ref[1, rrows, :]
            eb_row = rowq_ref[2, rrows, :]

            bmat = conv_tile(c, gw, cwb_ref, cbb_ref, 0).astype(BF16)
            cmat = conv_tile(c, gw + D_STATE, cwc_ref, cbc_ref, 0).astype(BF16)
            cc_ref[rows, :] = cmat
            cb = lax.dot_general(cmat, bmat, (((1,), (1,)), ((), ())), preferred_element_type=F32)

            def mix(h):
                arg = jnp.where(below, blk_t[:, h:h + 1] - cf_row[h:h + 1, :],
                                eb_row[hpg + h:hpg + h + 1, :] - blk_t[:, hpg + h:hpg + h + 1])
                e = jnp.exp(arg) + jnp.where(on_diag, dt_row[h:h + 1, :], 0.0)
                return (cb * e).astype(BF16)

            w_exp = jnp.dot(blk_tb, expand_w, preferred_element_type=F32)

            xs_f, xs_b = [], []
            for t in range(n_tiles):
                lanes = slice(t * LANES, (t + 1) * LANES)
                xf = conv_tile(c, 0, cwx_ref, cbx_ref, t)
                xt = xf.astype(BF16)
                zero = jnp.zeros_like(xt)
                rhs = jnp.concatenate([jnp.where(lane_lo, xt, zero), jnp.where(lane_lo, zero, xt)],
                                      axis=0)
                lhs = jnp.concatenate([mix(2 * t), mix(2 * t + 1)], axis=1)
                yd_ref[rows, lanes] = (jnp.dot(lhs, rhs, preferred_element_type=F32)
                                       + dsk_ref[:, lanes] * xf)
                xs_f.append((xf * w_exp[:, t * LANES:(t + 1) * LANES]).astype(BF16))
                xs_b.append((xf * w_exp[:, gw + t * LANES:gw + (t + 1) * LANES]).astype(BF16))
            contrib_ref[c] = lax.dot_general(bmat, jnp.concatenate(xs_f + xs_b, axis=1),
                                             (((0,), (0,)), ((), ())), preferred_element_type=F32)
            dec_ref[c] = _expand_rows(jnp.exp(blk_t[0:1, HP:2 * HP]), 2 * n_tiles, lane_lo)
            return carry

        lax.fori_loop(0, nc, phase_a, 0, unroll=min(nc, 4))

        def phase_b(k, carry):
            c = nc - 1 - k
            sb = sb_ref[...]
            sball_ref[c] = sb.astype(BF16)
            sb_ref[...] = sb * dec_ref[c][:, gw:] + contrib_ref[c, :, gw:]
            return carry

        lax.fori_loop(0, nc, phase_b, 0, unroll=min(nc, 4))

        def phase_c(c, carry):
            rows = chunk_rows(c)
            sf = sf_ref[...]
            if y_ref is not None:
                s_cat = jnp.concatenate([sf.astype(BF16), sball_ref[c]], axis=1)
                cs = jnp.dot(cc_ref[rows, :], s_cat, preferred_element_type=F32)
                off = jnp.dot(tb_ref[c], expand_off, preferred_element_type=F32)
                g_tiles = []
                ssq = jnp.zeros((CHUNK, 1), F32)
                for t in range(n_tiles):
                    lanes = slice(t * LANES, (t + 1) * LANES)
                    blanes = slice(gw + t * LANES, gw + (t + 1) * LANES)
                    y = (yd_ref[rows, lanes] + cs[:, lanes] * off[:, lanes]
                         + cs[:, blanes] * off[:, blanes])
                    g = y * _silu(z_ref[rows, lanes].astype(F32))
                    ssq = ssq + jnp.sum(g * g, axis=-1, keepdims=True)
                    g_tiles.append(g)
                inv = lax.rsqrt(ssq * (1.0 / gw) + EPS)
                for t in range(n_tiles):
                    lanes = slice(t * LANES, (t + 1) * LANES)
                    y_ref[rows, lanes] = ((g_tiles[t] * inv) * nw_ref[:, lanes]).astype(BF16)
            sf_ref[...] = sf * dec_ref[c][:, :gw] + contrib_ref[c, :, :gw]
            return carry

        lax.fori_loop(0, nc, phase_c, 0, unroll=min(nc, 8))

    sf_ref[...] = jnp.zeros(sf_ref.shape, F32)
    sb_ref[...] = jnp.zeros(sb_ref.shape, F32)
    for (x_ref, b_ref, c_ref, z_ref, dtr_ref, pr_ref), y_ref, seg in zip(seg_refs, y_refs, segments):
        run_segment(x_ref, b_ref, c_ref, z_ref, dtr_ref, pr_ref, y_ref, seg)


class _Segment(NamedTuple):
    length: int
    grid_rows: int
    grid_cols: int
    emit_y: bool


def _ssd(token_sets, p_col, conv_w9, conv_b, d_exp, norm_w, d_ssd):
    g = N_BC_GROUPS
    gw = d_ssd // g
    hpg = gw // HEAD_DIM
    bc_first = d_ssd // D_STATE
    pad = HP - 2 * hpg
    p_col = jnp.pad(p_col, ((0, 0), (0, 0), (0, pad)))
    bsz = token_sets[0][0].shape[0]
    segments, args, in_specs, out_shape, out_specs = [], [], [], [], []
    for xbc, zs, dt, grid_rows, grid_cols, emit_y in token_sets:
        length = xbc.shape[1]
        nc = length // CHUNK
        segments.append(_Segment(length, grid_rows, grid_cols, emit_y))
        p_row = jnp.tile(p_col.transpose(0, 2, 1), (1, nc, 1))
        args += [xbc, xbc, xbc, zs, dt, p_row]
        in_specs += [
            pl.BlockSpec((None, length, gw), lambda b, j: (b, 0, j)),
            pl.BlockSpec((None, length, D_STATE), lambda b, j: (b, 0, bc_first + j)),
            pl.BlockSpec((None, length, D_STATE), lambda b, j: (b, 0, bc_first + g + j)),
            pl.BlockSpec((None, length, gw), lambda b, j: (b, 0, j)),
            pl.BlockSpec((None, nc, HP, CHUNK), lambda b, j: (b, 0, j, 0)),
            pl.BlockSpec((None, nc * HP, 2), lambda b, j: (j, 0, 0)),
        ]
        if emit_y:
            out_shape.append(jax.ShapeDtypeStruct((bsz, length, d_ssd), BF16))
            out_specs.append(pl.BlockSpec((None, length, gw), lambda b, j: (b, 0, j)))
    args += [conv_w9, conv_w9, conv_w9, conv_b, conv_b, conv_b, d_exp, norm_w]
    in_specs += [
        pl.BlockSpec((9, gw), lambda b, j: (0, j)),
        pl.BlockSpec((9, D_STATE), lambda b, j: (0, bc_first + j)),
        pl.BlockSpec((9, D_STATE), lambda b, j: (0, bc_first + g + j)),
        pl.BlockSpec((1, gw), lambda b, j: (0, j)),
        pl.BlockSpec((1, D_STATE), lambda b, j: (0, bc_first + j)),
        pl.BlockSpec((1, D_STATE), lambda b, j: (0, bc_first + g + j)),
        pl.BlockSpec((1, gw), lambda b, j: (0, j)),
        pl.BlockSpec((1, gw), lambda b, j: (0, j)),
    ]
    max_len = max(seg.length for seg in segments)
    max_nc = max_len // CHUNK
    conv_rows = max(seg.length + (2 * seg.grid_cols if seg.grid_rows > 1 else 0) for seg in segments)
    conv_w = gw + 2 * D_STATE
    outs = pl.pallas_call(
        functools.partial(_ssd_kernel, segments=tuple(segments), hpg=hpg),
        out_shape=out_shape,
        grid=(bsz, g),
        in_specs=in_specs,
        out_specs=out_specs,
        scratch_shapes=[pltpu.VMEM((conv_rows, conv_w), BF16),
                        pltpu.VMEM((conv_rows, conv_w), BF16),
                        pltpu.VMEM((conv_rows, conv_w), BF16),
                        pltpu.VMEM((max_len, D_STATE), BF16),
                        pltpu.VMEM((3, max_nc * HP, CHUNK), F32),
                        pltpu.VMEM((max_nc, CHUNK, CHUNK), F32),
                        pltpu.VMEM((max_nc, CHUNK, CHUNK), BF16),
                        pltpu.VMEM((max_len, gw), F32),
                        pltpu.VMEM((max_nc, D_STATE, 2 * gw), F32),
                        pltpu.VMEM((max_nc, 1, 2 * gw), F32),
                        pltpu.VMEM((max_nc, D_STATE, gw), BF16),
                        pltpu.VMEM((D_STATE, gw), F32),
                        pltpu.VMEM((D_STATE, gw), F32)],
        compiler_params=_cparams("parallel", "parallel"),
        name="ssd",
    )(*args)
    outs = list(outs)
    return [outs.pop(0) if seg.emit_y else None for seg in segments]


def _final_kernel(four_ref, ssd_ref, x_ref, gate_ref, w_ref, fnw_ref, o_ref):
    y = _residual(four_ref, ssd_ref, x_ref, gate_ref, w_ref)
    ms = jnp.mean(y * y, axis=-1, keepdims=True)
    o_ref[...] = (y * lax.rsqrt(ms + EPS)) * fnw_ref[...]


def _final(four, ssd, x, gate, w_out, layer, final_norm_w):
    bsz, length, d = x.shape
    tm = min(length, FINAL_ROW_TILE)
    row_map = lambda b, t: (b, t, 0)
    return pl.pallas_call(
        _final_kernel,
        out_shape=jax.ShapeDtypeStruct((bsz, length, d), F32),
        grid=(bsz, length // tm),
        in_specs=[
            pl.BlockSpec((None, tm, four.shape[2]), row_map),
            pl.BlockSpec((None, tm, ssd.shape[2]), row_map),
            pl.BlockSpec((None, tm, d), row_map),
            pl.BlockSpec((None, 1, d), lambda b, t: (b, 0, 0)),
            _resident(w_out.shape[1:], layer),
            _resident((1, d)),
        ],
        out_specs=pl.BlockSpec((None, tm, d), row_map),
        compiler_params=_cparams("parallel", "parallel"),
        name="final",
    )(four, ssd, x, gate, w_out, final_norm_w)


def kernel(x, c, ctx, c_ctx, norm_w, w_ada, b_ada, w_in, conv_w, conv_b, dt_bias, a_log, d_skip,
           ssd_norm_w, w_fourier, b_fourier, w_out, final_norm_w):
    bsz, seq_len, d = x.shape
    ctx_len = ctx.shape[1]
    depth = w_in.shape[0]
    d_f = w_fourier.shape[1]
    d_ssd = ssd_norm_w.shape[1]
    conv_ch = conv_w.shape[-1]
    n_heads = d_skip.shape[1]
    g = N_BC_GROUPS
    hpg = n_heads // g
    splits = (d_f, d_f, d_ssd, conv_ch)

    pad_rows = (-(bsz + 1)) % 8
    cc = jnp.concatenate([c, c_ctx[None, :], jnp.zeros((pad_rows, d), F32)], axis=0)
    mod = _adaln(cc, w_ada, b_ada)

    w_in_t = jnp.swapaxes(w_in, 1, 2)
    w_dt = w_in[:, :, sum(splits):].reshape(depth, d, 2, g, hpg).transpose(0, 3, 2, 4, 1)
    w_dt = jnp.pad(w_dt.reshape(depth, g, 2 * hpg, d), ((0, 0), (0, 0), (0, HP - 2 * hpg), (0, 0)))
    w_dt = w_dt.reshape(depth, g * HP, d)
    w_f_b = w_fourier.astype(BF16)
    w9 = conv_w.reshape(depth, 9, conv_ch)
    par = jnp.stack([dt_bias, a_log], axis=1)
    p_col = par.reshape(depth, 2, 2, g, hpg).transpose(0, 3, 1, 2, 4).reshape(depth, g, 2, 2 * hpg)
    d_exp = jnp.repeat(d_skip, HEAD_DIM, axis=1).reshape(depth, 1, d_ssd)

    fseq_x, fch = (jnp.asarray(t).astype(BF16) for t in _dft_tables(seq_len, d_f))
    fseq_c = jnp.asarray(_dft_tables(ctx_len, d_f)[0]).astype(BF16)
    rows = seq_len // GRID_W

    prev_c = prev_x = None
    for i in range(depth):
        last = i == depth - 1
        m = mod[i]
        shift, scale, gate = (m[:bsz, None, k * d:(k + 1) * d] for k in range(3))
        shift_c, scale_c, gate_c = (m[bsz:bsz + 1, None, k * d:(k + 1) * d] for k in range(3))
        nw_i = norm_w[i].reshape(1, d)
        outs_c = _inproj(ctx, shift_c, scale_c, nw_i, w_in_t, w_dt, i, splits, prev_c)
        outs_x = _inproj(x, shift, scale, nw_i, w_in_t, w_dt, i, splits, prev_x)
        if i > 0:
            ctx, outs_c = outs_c[0], outs_c[1:]
            x, outs_x = outs_x[0], outs_x[1:]
        (u_c, zf_c, zs_c, xbc_c, dt_c), (u_x, zf_x, zs_x, xbc_x, dt_x) = outs_c, outs_x
        ssd_c, ssd_x = _ssd(
            [(xbc_c, zs_c, dt_c, 1, ctx_len, not last), (xbc_x, zs_x, dt_x, rows, GRID_W, True)],
            p_col[i], w9[i], conv_b[i].reshape(1, conv_ch), d_exp[i],
            ssd_norm_w[i].reshape(1, d_ssd), d_ssd)
        b_f = b_fourier[i].reshape(1, d_f)
        four_x = _fourier(u_x, zf_x, fseq_x, fch, w_f_b, i, b_f)
        prev_x = (four_x, ssd_x, gate, w_out)
        if not last:
            four_c = _fourier(u_c, zf_c, fseq_c, fch, w_f_b, i, b_f)
            prev_c = (four_c, ssd_c, gate_c, w_out)
    four_x, ssd_x, gate, w_o = prev_x
    return _final(four_x, ssd_x, x, gate, w_o, depth - 1, final_norm_w.reshape(1, d))
```

```python
import functools
import math
from typing import NamedTuple

import jax
import jax.numpy as jnp
import numpy as np
from jax import lax
from jax.experimental import pallas as pl
from jax.experimental.pallas import tpu as pltpu

F32 = jnp.float32
BF16 = jnp.bfloat16

GRID_W = 64
FOURIER_GROUP = 64
HEAD_DIM = 64
N_BC_GROUPS = 4
D_STATE = 128
CHUNK = 128
EPS = 1e-6
LOG2E = 1.4426950408889634

LANES = 128
BF16_SUBLANES = 16
MXU_WIDTH = 256
VMEM_LIMIT_BYTES = 56 * 1024 * 1024

PROJ_ROW_TILE = 512
PROJ_COL_TILE = 512
FINAL_ROW_TILE = 1024
ADALN_COL_TILE = 1024
FOURIER_ROW_TILE = 256

HEADS_PER_TILE = LANES // HEAD_DIM


def _cparams(*sem):
    return pltpu.CompilerParams(dimension_semantics=sem, vmem_limit_bytes=VMEM_LIMIT_BYTES)


def _silu(v):
    return v * (1.0 / (1.0 + jnp.exp2(v * (-LOG2E))))


def _softplus(v):
    return jnp.maximum(v, 0.0) + jnp.log1p(jnp.exp(-jnp.abs(v)))


def _resident(shape, layer=None):
    zeros = (0,) * len(shape)
    if layer is None:
        return pl.BlockSpec(shape, lambda *_: zeros, pipeline_mode=pl.Buffered(1))
    return pl.BlockSpec((None,) + tuple(shape), lambda *_: (layer,) + zeros,
                        pipeline_mode=pl.Buffered(1))


def _adaln_kernel(c_ref, w_ref, b_ref, o_ref):
    s = _silu(c_ref[...])
    o_ref[...] = jnp.dot(s, w_ref[...], preferred_element_type=F32,
                         precision=lax.Precision.HIGHEST) + b_ref[...]


def _adaln(cc, w_ada, b_ada):
    depth, d, d3 = w_ada.shape
    rows = cc.shape[0]
    tn = ADALN_COL_TILE
    return pl.pallas_call(
        _adaln_kernel,
        out_shape=jax.ShapeDtypeStruct((depth, rows, d3), F32),
        grid=(depth, d3 // tn),
        in_specs=[
            pl.BlockSpec((rows, d), lambda i, j: (0, 0)),
            pl.BlockSpec((None, d, tn), lambda i, j: (i, 0, j)),
            pl.BlockSpec((None, 1, tn), lambda i, j: (i, 0, j)),
        ],
        out_specs=pl.BlockSpec((None, rows, tn), lambda i, j: (i, 0, j)),
        compiler_params=_cparams("parallel", "parallel"),
        name="adaln",
    )(cc, w_ada, b_ada.reshape(depth, 1, d3))


def _project(x, sh_ref, sc_ref, nw_ref, wt_ref, wdt_ref, out_refs, dt_ref, splits, col_tile):
    ms = jnp.mean(x * x, axis=-1, keepdims=True)
    h = (x * lax.rsqrt(ms + EPS)) * nw_ref[...]
    h = h * (1.0 + sc_ref[...]) + sh_ref[...]
    col = 0
    for out_ref, width in zip(out_refs, splits):
        for c0 in range(0, width, col_tile):
            out_ref[:, c0:c0 + col_tile] = lax.dot_general(
                h, wt_ref[col + c0:col + c0 + col_tile, :], (((1,), (1,)), ((), ())),
                preferred_element_type=F32).astype(BF16)
        col += width
    dt_t = lax.dot_general(wdt_ref[...], h, (((1,), (1,)), ((), ())), preferred_element_type=F32)
    for k in range(dt_ref.shape[0]):
        dt_ref[k] = dt_t[:, k * CHUNK:(k + 1) * CHUNK]


def _residual(four_ref, ssd_ref, x_ref, gate_ref, wo_ref):
    d_f = four_ref.shape[1]
    acc = jnp.dot(four_ref[...].astype(F32), wo_ref[0:d_f, :], preferred_element_type=F32)
    acc = acc + jnp.dot(ssd_ref[...].astype(F32), wo_ref[d_f:, :], preferred_element_type=F32)
    return x_ref[...] + gate_ref[...] * acc


def _inproj_kernel(x_ref, sh_ref, sc_ref, nw_ref, wt_ref, wdt_ref,
                   u_ref, zf_ref, zs_ref, xbc_ref, dt_ref, *, splits, col_tile):
    _project(x_ref[...], sh_ref, sc_ref, nw_ref, wt_ref, wdt_ref, (u_ref, zf_ref, zs_ref, xbc_ref),
             dt_ref, splits, col_tile)


def _outproj_inproj_kernel(four_ref, ssd_ref, x_ref, gate_ref, wo_ref, sh_ref, sc_ref, nw_ref, wt_ref,
                           wdt_ref, xo_ref, u_ref, zf_ref, zs_ref, xbc_ref, dt_ref, *, splits, col_tile):
    x_new = _residual(four_ref, ssd_ref, x_ref, gate_ref, wo_ref)
    xo_ref[...] = x_new
    _project(x_new, sh_ref, sc_ref, nw_ref, wt_ref, wdt_ref, (u_ref, zf_ref, zs_ref, xbc_ref),
             dt_ref, splits, col_tile)


def _inproj(x, shift, scale, norm_w, w_in_t, w_dt, layer, splits, prev=None):
    bsz, length, d = x.shape
    tm = min(length, PROJ_ROW_TILE)
    n_dt = w_dt.shape[1]
    row_map = lambda b, t: (b, t, 0)

    def per_batch(arr):
        return pl.BlockSpec((None, 1, d), (lambda b, t: (b, 0, 0)) if arr.shape[0] > 1
                            else (lambda b, t: (0, 0, 0)))

    out_shape = [jax.ShapeDtypeStruct((bsz, length, w), BF16) for w in splits]
    out_shape.append(jax.ShapeDtypeStruct((bsz, length // CHUNK, n_dt, CHUNK), F32))
    out_specs = [pl.BlockSpec((None, tm, w), row_map) for w in splits]
    out_specs.append(pl.BlockSpec((None, tm // CHUNK, n_dt, CHUNK), lambda b, t: (b, t, 0, 0)))
    in_specs = [pl.BlockSpec((None, tm, d), row_map), per_batch(shift), per_batch(scale),
                _resident((1, d)), _resident(w_in_t.shape[1:], layer), _resident(w_dt.shape[1:], layer)]
    args = [x, shift, scale, norm_w, w_in_t, w_dt]
    body = _inproj_kernel
    if prev is not None:
        four, ssd, gate, w_out = prev
        in_specs = [pl.BlockSpec((None, tm, four.shape[2]), row_map),
                    pl.BlockSpec((None, tm, ssd.shape[2]), row_map),
                    in_specs[0], per_batch(gate), _resident(w_out.shape[1:], layer - 1)] + in_specs[1:]
        args = [four, ssd, x, gate, w_out] + args[1:]
        out_shape = [jax.ShapeDtypeStruct((bsz, length, d), F32)] + out_shape
        out_specs = [pl.BlockSpec((None, tm, d), row_map)] + out_specs
        body = _outproj_inproj_kernel
    return pl.pallas_call(
        functools.partial(body, splits=splits, col_tile=PROJ_COL_TILE),
        out_shape=out_shape,
        grid=(bsz, length // tm),
        in_specs=in_specs,
        out_specs=out_specs,
        compiler_params=_cparams("parallel", "parallel"),
        name="inproj" if prev is None else "outproj_inproj",
    )(*args)


def _reverse_into(src_ref, src_end, dst_ref, dst0, count):
    blk = LANES
    ri = lax.broadcasted_iota(jnp.int32, (blk, 2 * blk), 0)
    ci = lax.broadcasted_iota(jnp.int32, (blk, 2 * blk), 1)
    rev = ((ri + ci == blk) | ((ri == 0) & (ci == blk))).astype(BF16)
    for i in range(count // blk):
        lo = src_ref[src_end - (i + 1) * blk:src_end - i * blk, :]
        hi = (src_ref[src_end - i * blk:src_end - (i - 1) * blk, :] if i > 0
              else jnp.zeros_like(lo))
        dst_ref[dst0 + i * blk:dst0 + (i + 1) * blk, :] = jnp.dot(
            rev, jnp.concatenate([lo, hi], axis=0), preferred_element_type=F32).astype(BF16)


def _fourier_kernel(u_ref, zf_ref, fseq_ref, fch_ref, wf_ref, bf_ref, o_ref,
                    urev_ref, ucs_ref, mix_ref, mirror_ref, *, row_tile):
    length, width = u_ref.shape
    half = length // 2
    scale = 1.0 / math.sqrt(length)

    _reverse_into(u_ref, length, urev_ref, 0, half)
    mids = []
    for c0 in range(0, width, MXU_WIDTH):
        cols = slice(c0, c0 + MXU_WIDTH)
        uh = u_ref[0:half, cols]
        ur = urev_ref[:, cols]
        cos_ch = fch_ref[cols, cols]
        sin_ch = fch_ref[cols, width + c0:width + c0 + MXU_WIDTH]
        ucs_ref[0:half, cols] = (jnp.dot(uh, cos_ch, preferred_element_type=F32)
                                 + jnp.dot(ur, cos_ch, preferred_element_type=F32)).astype(BF16)
        ucs_ref[half:length, cols] = (jnp.dot(uh, sin_ch, preferred_element_type=F32)
                                      - jnp.dot(ur, sin_ch, preferred_element_type=F32)).astype(BF16)
        mids.append(jnp.dot(u_ref[half:half + BF16_SUBLANES, cols], cos_ch,
                            preferred_element_type=F32)[0:1, :] * scale)
    mid = jnp.concatenate(mids, axis=1)

    for r in range(half // row_tile):
        rows = slice(r * row_tile, (r + 1) * row_tile)
        c_part = jnp.dot(fseq_ref[rows, 0:half], ucs_ref[0:half, :], preferred_element_type=F32)
        s_part = jnp.dot(fseq_ref[rows, half:length], ucs_ref[half:length, :],
                         preferred_element_type=F32)
        k = r * row_tile + lax.broadcasted_iota(jnp.int32, (row_tile, 1), 0)
        base = c_part + (1 - 2 * (k & 1)).astype(F32) * mid
        mix_ref[rows, :] = (base + s_part).astype(BF16)
        mirror_ref[rows, :] = (base - s_part).astype(BF16)
    _reverse_into(mirror_ref, half, mix_ref, half, half)
    li = lax.broadcasted_iota(jnp.int32, (BF16_SUBLANES, half), 1)
    alt = (1 - 2 * (li & 1)).astype(BF16)
    row_half = (jnp.dot(alt, ucs_ref[0:half, :], preferred_element_type=F32)[0:1, :] * scale
                + (1.0 if half % 2 == 0 else -1.0) * mid)
    first = mix_ref[half:half + BF16_SUBLANES, :]
    sub = lax.broadcasted_iota(jnp.int32, first.shape, 0)
    mix_ref[half:half + BF16_SUBLANES, :] = jnp.where(sub == 0, row_half.astype(BF16), first)

    for r in range(length // row_tile):
        rows = slice(r * row_tile, (r + 1) * row_tile)
        lin = jnp.dot(mix_ref[rows, :], wf_ref[...], preferred_element_type=F32) + bf_ref[...]
        o_ref[rows, :] = (lin * _silu(zf_ref[rows, :].astype(F32))).astype(BF16)


def _fourier(u, zf, fseq, fch, w_f, layer, b_f):
    bsz, length, width = u.shape
    half = length // 2
    assert half % LANES == 0, "the block-wise reversal needs L/2 to be a multiple of 128"
    assert width % MXU_WIDTH == 0 and MXU_WIDTH % FOURIER_GROUP == 0
    batch_block = pl.BlockSpec((None, length, width), lambda b: (b, 0, 0))
    return pl.pallas_call(
        functools.partial(_fourier_kernel, row_tile=min(half, FOURIER_ROW_TILE)),
        out_shape=jax.ShapeDtypeStruct((bsz, length, width), BF16),
        grid=(bsz,),
        in_specs=[batch_block, batch_block, _resident(fseq.shape), _resident(fch.shape),
                  _resident(w_f.shape[1:], layer), _resident(b_f.shape)],
        out_specs=batch_block,
        scratch_shapes=[pltpu.VMEM((half, width), BF16),
                        pltpu.VMEM((length, width), BF16),
                        pltpu.VMEM((length, width), BF16),
                        pltpu.VMEM((half, width), BF16)],
        compiler_params=_cparams("parallel"),
        name="fourier",
    )(u, zf, fseq, fch, w_f, b_f)


@functools.lru_cache(maxsize=None)
def _dft_tables(length, width):
    half = length // 2
    kl = (np.arange(half)[:, None] * np.arange(half)[None, :]) % length
    ang = kl * (2.0 * np.pi / length)
    fseq = np.concatenate([np.cos(ang), -np.sin(ang)], axis=1) / np.sqrt(length)
    ch = np.arange(width)
    same = (ch[:, None] // FOURIER_GROUP) == (ch[None, :] // FOURIER_GROUP)
    angc = (((ch[:, None] % FOURIER_GROUP) * (ch[None, :] % FOURIER_GROUP)) % FOURIER_GROUP) * (
        2.0 * np.pi / FOURIER_GROUP)
    fch = np.concatenate([np.where(same, np.cos(angc), 0.0),
                          np.where(same, np.sin(angc), 0.0)], axis=1) / np.sqrt(FOURIER_GROUP)
    return fseq.astype(np.float32), fch.astype(np.float32)


HP = BF16_SUBLANES
CONV_SUB = 128


def _split2(v):
    hi = v.astype(BF16)
    return hi, (v - hi.astype(F32)).astype(BF16)


def _split3(v):
    hi, _ = _split2(v)
    r1 = v - hi.astype(F32)
    mid, lo = _split2(r1)
    return hi, mid, lo


def _expand_rows(cols, n_tiles, lane_lo):
    tiles = [jnp.where(lane_lo, cols[:, 2 * t:2 * t + 1], cols[:, 2 * t + 1:2 * t + 2])
             for t in range(n_tiles)]
    return jnp.concatenate(tiles, axis=1)


def _ssd_kernel(*refs, segments, hpg):
    n_seg = len(segments)
    seg_refs = [refs[6 * i:6 * i + 6] for i in range(n_seg)]
    pos = 6 * n_seg
    cwx_ref, cwb_ref, cwc_ref, cbx_ref, cbb_ref, cbc_ref, dsk_ref, nw_ref = refs[pos:pos + 8]
    pos += 8
    y_refs = []
    for seg in segments:
        y_refs.append(refs[pos] if seg.emit_y else None)
        pos += 1 if seg.emit_y else 0
    (sm_ref, s0_ref, sp_ref, cc_ref, rowq_ref, tblk_ref, tb_ref, yd_ref, contrib_ref, dec_ref,
     sball_ref, sf_ref, sb_ref) = refs[pos:]

    gw = hpg * HEAD_DIM
    n_tiles = gw // LANES
    conv_w = s0_ref.shape[1]
    shifted = (sm_ref, s0_ref, sp_ref)
    ri = lax.broadcasted_iota(jnp.int32, (CHUNK, CHUNK), 0)
    ci = lax.broadcasted_iota(jnp.int32, (CHUNK, CHUNK), 1)
    tri_t = (ri <= ci).astype(BF16)
    below = ci < ri
    on_diag = ci == ri
    lane_lo = lax.broadcasted_iota(jnp.int32, (1, LANES), 1) < HEAD_DIM
    erow = lax.broadcasted_iota(jnp.int32, (CHUNK, 2 * gw), 0)
    ehead = lax.broadcasted_iota(jnp.int32, (CHUNK, 2 * gw), 1) // HEAD_DIM
    expand_w = (erow == ehead + 2 * HP).astype(BF16)
    expand_off = (erow == ehead + 3 * HP).astype(BF16)

    def chunk_rows(c):
        return pl.ds(pl.multiple_of(c * CHUNK, CHUNK), CHUNK)

    def run_segment(x_ref, b_ref, c_ref, z_ref, dtr_ref, pr_ref, y_ref, seg):
        length, grid_rows, grid_cols = seg.length, seg.grid_rows, seg.grid_cols
        nc = length // CHUNK
        nrow = nc * HP

        halo = grid_cols if grid_rows > 1 else 0
        if halo:
            zeros = jnp.zeros((halo, conv_w), BF16)
            for s_ref in shifted:
                s_ref[0:halo, :] = zeros
                s_ref[halo + length:2 * halo + length, :] = zeros
        sources = ((x_ref, 0, gw), (b_ref, gw, D_STATE), (c_ref, gw + D_STATE, D_STATE))

        def shift_row(r, carry):
            src_rows = pl.ds(pl.multiple_of(r * grid_cols, grid_cols), grid_cols)
            dst_rows = pl.ds(pl.multiple_of(halo + r * grid_cols, grid_cols), grid_cols)
            for ref, lane0, width in sources:
                wpos = lax.broadcasted_iota(jnp.int32, (grid_cols, width), 0)
                xb = ref[src_rows, :]
                xr = xb.astype(F32)
                dst = slice(lane0, lane0 + width)
                s0_ref[dst_rows, dst] = xb
                sm_ref[dst_rows, dst] = jnp.where(
                    wpos == 0, 0.0, pltpu.roll(xr, 1, axis=0)).astype(BF16)
                sp_ref[dst_rows, dst] = jnp.where(
                    wpos == grid_cols - 1, 0.0, pltpu.roll(xr, grid_cols - 1, axis=0)).astype(BF16)
            return carry

        lax.fori_loop(0, grid_rows, shift_row, 0)
        dhs = (-1, 0, 1) if grid_rows > 1 else (0,)

        def conv_tile(c, lane0, w_ref, b_ref_, t):
            wl = slice(t * LANES, (t + 1) * LANES)
            sl = slice(lane0 + t * LANES, lane0 + (t + 1) * LANES)
            bias = b_ref_[:, wl]
            outs = []
            for s in range(CHUNK // CONV_SUB):
                base = pl.multiple_of(c * CHUNK, CHUNK) + s * CONV_SUB
                acc = bias
                for dh in dhs:
                    off = halo + base + dh * grid_cols
                    part = None
                    for kw, s_ref in enumerate(shifted):
                        k = 3 * (dh + 1) + kw
                        term = w_ref[k:k + 1, wl].astype(BF16) * s_ref[pl.ds(off, CONV_SUB), sl]
                        part = term if part is None else part + term
                    acc = acc + part.astype(F32)
                outs.append(_silu(acc))
            return jnp.concatenate(outs, axis=0)

        row_fwd = (lax.broadcasted_iota(jnp.int32, (nrow, 1), 0) & (HP - 1)) < hpg
        dt_r = _softplus(dtr_ref[...].reshape(nrow, CHUNK) + pr_ref[:, 0:1])
        a_r = dt_r * (-jnp.exp(pr_ref[:, 1:2]))
        cum_r = sum(jnp.dot(p, tri_t, preferred_element_type=F32) for p in _split3(a_r))
        excl_r = cum_r - a_r
        tot_r = jnp.broadcast_to(cum_r[:, CHUNK - 1:CHUNK], cum_r.shape)
        log_dt = jnp.log(dt_r)
        rowq_ref[0, 0:nrow, :] = dt_r
        rowq_ref[1, 0:nrow, :] = cum_r - log_dt
        rowq_ref[2, 0:nrow, :] = excl_r + log_dt
        to_cols = (
            jnp.where(row_fwd, cum_r, excl_r),
            tot_r,
            jnp.where(row_fwd, jnp.exp(tot_r - cum_r), jnp.exp(excl_r)) * dt_r,
            jnp.where(row_fwd, jnp.exp(cum_r), jnp.exp(tot_r - excl_r)),
        )
        for c in range(nc):
            for q, val in enumerate(to_cols):
                tblk_ref[c, q * HP:(q + 1) * HP, :] = val[c * HP:(c + 1) * HP, :]
            tblk_ref[c, len(to_cols) * HP:, :] = jnp.zeros((CHUNK - len(to_cols) * HP, CHUNK), F32)

        def phase_a(c, carry):
            rows = chunk_rows(c)
            rrows = pl.ds(pl.multiple_of(c * HP, HP), HP)
            blk_t = tblk_ref[c].T
            blk_tb = blk_t.astype(BF16)
            tb_ref[c] = blk_tb
            dt_row = rowq_ref[0, rrows, :]
            cf_row = rowq_ref[1, rrows, :]
            eb_row = rowq_ref[2, rrows, :]

            bmat = conv_tile(c, gw, cwb_ref, cbb_ref, 0).astype(BF16)
            cmat = conv_tile(c, gw + D_STATE, cwc_ref, cbc_ref, 0).astype(BF16)
            cc_ref[rows, :] = cmat
            cb = lax.dot_general(cmat, bmat, (((1,), (1,)), ((), ())), preferred_element_type=F32)

            def mix(h):
                arg = jnp.where(below, blk_t[:, h:h + 1] - cf_row[h:h + 1, :],
                                eb_row[hpg + h:hpg + h + 1, :] - blk_t[:, hpg + h:hpg + h + 1])
                e = jnp.exp(arg) + jnp.where(on_diag, dt_row[h:h + 1, :], 0.0)
                return (cb * e).astype(BF16)

            w_exp = jnp.dot(blk_tb, expand_w, preferred_element_type=F32)

            xs_f, xs_b = [], []
            for t in range(n_tiles):
                lanes = slice(t * LANES, (t + 1) * LANES)
                xf = conv_tile(c, 0, cwx_ref, cbx_ref, t)
                xt = xf.astype(BF16)
                zero = jnp.zeros_like(xt)
                rhs = jnp.concatenate([jnp.where(lane_lo, xt, zero), jnp.where(lane_lo, zero, xt)],
                                      axis=0)
                lhs = jnp.concatenate([mix(2 * t), mix(2 * t + 1)], axis=1)
                yd_ref[rows, lanes] = (jnp.dot(lhs, rhs, preferred_element_type=F32)
                                       + dsk_ref[:, lanes] * xf)
                xs_f.append((xf * w_exp[:, t * LANES:(t + 1) * LANES]).astype(BF16))
                xs_b.append((xf * w_exp[:, gw + t * LANES:gw + (t + 1) * LANES]).astype(BF16))
            contrib_ref[c] = lax.dot_general(bmat, jnp.concatenate(xs_f + xs_b, axis=1),
                                             (((0,), (0,)), ((), ())), preferred_element_type=F32)
            dec_ref[c] = _expand_rows(jnp.exp(blk_t[0:1, HP:2 * HP]), 2 * n_tiles, lane_lo)
            return carry

        lax.fori_loop(0, nc, phase_a, 0, unroll=min(nc, 8))

        def phase_b(k, carry):
            c = nc - 1 - k
            sb = sb_ref[...]
            sball_ref[c] = sb.astype(BF16)
            sb_ref[...] = sb * dec_ref[c][:, gw:] + contrib_ref[c, :, gw:]
            return carry

        lax.fori_loop(0, nc, phase_b, 0, unroll=min(nc, 4))

        def phase_c(c, carry):
            rows = chunk_rows(c)
            sf = sf_ref[...]
            if y_ref is not None:
                s_cat = jnp.concatenate([sf.astype(BF16), sball_ref[c]], axis=1)
                cs = jnp.dot(cc_ref[rows, :], s_cat, preferred_element_type=F32)
                off = jnp.dot(tb_ref[c], expand_off, preferred_element_type=F32)
                g_tiles = []
                ssq = jnp.zeros((CHUNK, 1), F32)
                for t in range(n_tiles):
                    lanes = slice(t * LANES, (t + 1) * LANES)
                    blanes = slice(gw + t * LANES, gw + (t + 1) * LANES)
                    y = (yd_ref[rows, lanes] + cs[:, lanes] * off[:, lanes]
                         + cs[:, blanes] * off[:, blanes])
                    g = y * _silu(z_ref[rows, lanes].astype(F32))
                    ssq = ssq + jnp.sum(g * g, axis=-1, keepdims=True)
                    g_tiles.append(g)
                inv = lax.rsqrt(ssq * (1.0 / gw) + EPS)
                for t in range(n_tiles):
                    lanes = slice(t * LANES, (t + 1) * LANES)
                    y_ref[rows, lanes] = ((g_tiles[t] * inv) * nw_ref[:, lanes]).astype(BF16)
            sf_ref[...] = sf * dec_ref[c][:, :gw] + contrib_ref[c, :, :gw]
            return carry

        lax.fori_loop(0, nc, phase_c, 0, unroll=min(nc, 8))

    sf_ref[...] = jnp.zeros(sf_ref.shape, F32)
    sb_ref[...] = jnp.zeros(sb_ref.shape, F32)
    for (x_ref, b_ref, c_ref, z_ref, dtr_ref, pr_ref), y_ref, seg in zip(seg_refs, y_refs, segments):
        run_segment(x_ref, b_ref, c_ref, z_ref, dtr_ref, pr_ref, y_ref, seg)


class _Segment(NamedTuple):
    length: int
    grid_rows: int
    grid_cols: int
    emit_y: bool


def _ssd(token_sets, p_col, conv_w9, conv_b, d_exp, norm_w, d_ssd):
    assert HEADS_PER_TILE == 2, "the kernel pairs heads 2t, 2t + 1 on one 128-lane tile"
    g = N_BC_GROUPS
    gw = d_ssd // g
    hpg = gw // HEAD_DIM
    bc_first = d_ssd // D_STATE
    pad = HP - 2 * hpg
    p_col = jnp.pad(p_col, ((0, 0), (0, 0), (0, pad)))
    bsz = token_sets[0][0].shape[0]
    segments, args, in_specs, out_shape, out_specs = [], [], [], [], []
    for xbc, zs, dt, grid_rows, grid_cols, emit_y in token_sets:
        length = xbc.shape[1]
        nc = length // CHUNK
        segments.append(_Segment(length, grid_rows, grid_cols, emit_y))
        p_row = jnp.tile(p_col.transpose(0, 2, 1), (1, nc, 1))
        args += [xbc, xbc, xbc, zs, dt, p_row]
        in_specs += [
            pl.BlockSpec((None, length, gw), lambda b, j: (b, 0, j)),
            pl.BlockSpec((None, length, D_STATE), lambda b, j: (b, 0, bc_first + j)),
            pl.BlockSpec((None, length, D_STATE), lambda b, j: (b, 0, bc_first + g + j)),
            pl.BlockSpec((None, length, gw), lambda b, j: (b, 0, j)),
            pl.BlockSpec((None, nc, HP, CHUNK), lambda b, j: (b, 0, j, 0)),
            pl.BlockSpec((None, nc * HP, 2), lambda b, j: (j, 0, 0)),
        ]
        if emit_y:
            out_shape.append(jax.ShapeDtypeStruct((bsz, length, d_ssd), BF16))
            out_specs.append(pl.BlockSpec((None, length, gw), lambda b, j: (b, 0, j)))
    args += [conv_w9, conv_w9, conv_w9, conv_b, conv_b, conv_b, d_exp, norm_w]
    in_specs += [
        pl.BlockSpec((9, gw), lambda b, j: (0, j)),
        pl.BlockSpec((9, D_STATE), lambda b, j: (0, bc_first + j)),
        pl.BlockSpec((9, D_STATE), lambda b, j: (0, bc_first + g + j)),
        pl.BlockSpec((1, gw), lambda b, j: (0, j)),
        pl.BlockSpec((1, D_STATE), lambda b, j: (0, bc_first + j)),
        pl.BlockSpec((1, D_STATE), lambda b, j: (0, bc_first + g + j)),
        pl.BlockSpec((1, gw), lambda b, j: (0, j)),
        pl.BlockSpec((1, gw), lambda b, j: (0, j)),
    ]
    max_len = max(seg.length for seg in segments)
    max_nc = max_len // CHUNK
    conv_rows = max(seg.length + (2 * seg.grid_cols if seg.grid_rows > 1 else 0) for seg in segments)
    conv_w = gw + 2 * D_STATE
    outs = pl.pallas_call(
        functools.partial(_ssd_kernel, segments=tuple(segments), hpg=hpg),
        out_shape=out_shape,
        grid=(bsz, g),
        in_specs=in_specs,
        out_specs=out_specs,
        scratch_shapes=[pltpu.VMEM((conv_rows, conv_w), BF16),
                        pltpu.VMEM((conv_rows, conv_w), BF16),
                        pltpu.VMEM((conv_rows, conv_w), BF16),
                        pltpu.VMEM((max_len, D_STATE), BF16),
                        pltpu.VMEM((3, max_nc * HP, CHUNK), F32),
                        pltpu.VMEM((max_nc, CHUNK, CHUNK), F32),
                        pltpu.VMEM((max_nc, CHUNK, CHUNK), BF16),
                        pltpu.VMEM((max_len, gw), F32),
                        pltpu.VMEM((max_nc, D_STATE, 2 * gw), F32),
                        pltpu.VMEM((max_nc, 1, 2 * gw), F32),
                        pltpu.VMEM((max_nc, D_STATE, gw), BF16),
                        pltpu.VMEM((D_STATE, gw), F32),
                        pltpu.VMEM((D_STATE, gw), F32)],
        compiler_params=_cparams("parallel", "parallel"),
        name="ssd",
    )(*args)
    outs = list(outs)
    return [outs.pop(0) if seg.emit_y else None for seg in segments]


def _final_kernel(four_ref, ssd_ref, x_ref, gate_ref, w_ref, fnw_ref, o_ref):
    y = _residual(four_ref, ssd_ref, x_ref, gate_ref, w_ref)
    ms = jnp.mean(y * y, axis=-1, keepdims=True)
    o_ref[...] = (y * lax.rsqrt(ms + EPS)) * fnw_ref[...]


def _final(four, ssd, x, gate, w_out, layer, final_norm_w):
    bsz, length, d = x.shape
    tm = min(length, FINAL_ROW_TILE)
    row_map = lambda b, t: (b, t, 0)
    return pl.pallas_call(
        _final_kernel,
        out_shape=jax.ShapeDtypeStruct((bsz, length, d), F32),
        grid=(bsz, length // tm),
        in_specs=[
            pl.BlockSpec((None, tm, four.shape[2]), row_map),
            pl.BlockSpec((None, tm, ssd.shape[2]), row_map),
            pl.BlockSpec((None, tm, d), row_map),
            pl.BlockSpec((None, 1, d), lambda b, t: (b, 0, 0)),
            _resident(w_out.shape[1:], layer),
            _resident((1, d)),
        ],
        out_specs=pl.BlockSpec((None, tm, d), row_map),
        compiler_params=_cparams("parallel", "parallel"),
        name="final",
    )(four, ssd, x, gate, w_out, final_norm_w)


def kernel(x, c, ctx, c_ctx, norm_w, w_ada, b_ada, w_in, conv_w, conv_b, dt_bias, a_log, d_skip,
           ssd_norm_w, w_fourier, b_fourier, w_out, final_norm_w):
    bsz, seq_len, d = x.shape
    ctx_len = ctx.shape[1]
    depth = w_in.shape[0]
    d_f = w_fourier.shape[1]
    d_ssd = ssd_norm_w.shape[1]
    conv_ch = conv_w.shape[-1]
    n_heads = d_skip.shape[1]
    g = N_BC_GROUPS
    hpg = n_heads // g
    splits = (d_f, d_f, d_ssd, conv_ch)

    pad_rows = (-(bsz + 1)) % 8
    cc = jnp.concatenate([c, c_ctx[None, :], jnp.zeros((pad_rows, d), F32)], axis=0)
    mod = _adaln(cc, w_ada, b_ada)

    w_in_t = jnp.swapaxes(w_in, 1, 2)
    w_dt = w_in[:, :, sum(splits):].reshape(depth, d, 2, g, hpg).transpose(0, 3, 2, 4, 1)
    w_dt = jnp.pad(w_dt.reshape(depth, g, 2 * hpg, d), ((0, 0), (0, 0), (0, HP - 2 * hpg), (0, 0)))
    w_dt = w_dt.reshape(depth, g * HP, d)
    w_f_b = w_fourier.astype(BF16)
    w9 = conv_w.reshape(depth, 9, conv_ch)
    par = jnp.stack([dt_bias, a_log], axis=1)
    p_col = par.reshape(depth, 2, 2, g, hpg).transpose(0, 3, 1, 2, 4).reshape(depth, g, 2, 2 * hpg)
    d_exp = jnp.repeat(d_skip, HEAD_DIM, axis=1).reshape(depth, 1, d_ssd)

    fseq_x, fch = (jnp.asarray(t).astype(BF16) for t in _dft_tables(seq_len, d_f))
    fseq_c = jnp.asarray(_dft_tables(ctx_len, d_f)[0]).astype(BF16)
    rows = seq_len // GRID_W

    prev_c = prev_x = None
    for i in range(depth):
        last = i == depth - 1
        m = mod[i]
        shift, scale, gate = (m[:bsz, None, k * d:(k + 1) * d] for k in range(3))
        shift_c, scale_c, gate_c = (m[bsz:bsz + 1, None, k * d:(k + 1) * d] for k in range(3))
        nw_i = norm_w[i].reshape(1, d)
        outs_c = _inproj(ctx, shift_c, scale_c, nw_i, w_in_t, w_dt, i, splits, prev_c)
        outs_x = _inproj(x, shift, scale, nw_i, w_in_t, w_dt, i, splits, prev_x)
        if i > 0:
            ctx, outs_c = outs_c[0], outs_c[1:]
            x, outs_x = outs_x[0], outs_x[1:]
        (u_c, zf_c, zs_c, xbc_c, dt_c), (u_x, zf_x, zs_x, xbc_x, dt_x) = outs_c, outs_x
        ssd_c, ssd_x = _ssd(
            [(xbc_c, zs_c, dt_c, 1, ctx_len, not last), (xbc_x, zs_x, dt_x, rows, GRID_W, True)],
            p_col[i], w9[i], conv_b[i].reshape(1, conv_ch), d_exp[i],
            ssd_norm_w[i].reshape(1, d_ssd), d_ssd)
        b_f = b_fourier[i].reshape(1, d_f)
        four_x = _fourier(u_x, zf_x, fseq_x, fch, w_f_b, i, b_f)
        prev_x = (four_x, ssd_x, gate, w_out)
        if not last:
            four_c = _fourier(u_c, zf_c, fseq_c, fch, w_f_b, i, b_f)
            prev_c = (four_c, ssd_c, gate_c, w_out)
    four_x, ssd_x, gate, w_o = prev_x
    return _final(four_x, ssd_x, x, gate, w_o, depth - 1, final_norm_w.reshape(1, d))
```

```python
import functools
import math
from typing import NamedTuple

import jax
import jax.numpy as jnp
import numpy as np
from jax import lax
from jax.experimental import pallas as pl
from jax.experimental.pallas import tpu as pltpu

F32 = jnp.float32
BF16 = jnp.bfloat16

GRID_W = 64
FOURIER_GROUP = 64
HEAD_DIM = 64
N_BC_GROUPS = 4
D_STATE = 128
CHUNK = 128
EPS = 1e-6
LOG2E = 1.4426950408889634

LANES = 128
BF16_SUBLANES = 16
MXU_WIDTH = 256
VMEM_LIMIT_BYTES = 56 * 1024 * 1024

PROJ_ROW_TILE = 512
PROJ_COL_TILE = 512
FINAL_ROW_TILE = 1024
ADALN_COL_TILE = 1024
FOURIER_ROW_TILE = 256

HEADS_PER_TILE = LANES // HEAD_DIM


def _cparams(*sem):
    return pltpu.CompilerParams(dimension_semantics=sem, vmem_limit_bytes=VMEM_LIMIT_BYTES)


def _silu(v):
    return v * (1.0 / (1.0 + jnp.exp2(v * (-LOG2E))))


def _softplus(v):
    return jnp.maximum(v, 0.0) + jnp.log1p(jnp.exp(-jnp.abs(v)))


def _resident(shape, layer=None):
    zeros = (0,) * len(shape)
    if layer is None:
        return pl.BlockSpec(shape, lambda *_: zeros, pipeline_mode=pl.Buffered(1))
    return pl.BlockSpec((None,) + tuple(shape), lambda *_: (layer,) + zeros,
                        pipeline_mode=pl.Buffered(1))


def _adaln_kernel(c_ref, w_ref, b_ref, o_ref):
    s = _silu(c_ref[...])
    o_ref[...] = jnp.dot(s, w_ref[...], preferred_element_type=F32,
                         precision=lax.Precision.HIGHEST) + b_ref[...]


def _adaln(cc, w_ada, b_ada):
    depth, d, d3 = w_ada.shape
    rows = cc.shape[0]
    tn = ADALN_COL_TILE
    return pl.pallas_call(
        _adaln_kernel,
        out_shape=jax.ShapeDtypeStruct((depth, rows, d3), F32),
        grid=(depth, d3 // tn),
        in_specs=[
            pl.BlockSpec((rows, d), lambda i, j: (0, 0)),
            pl.BlockSpec((None, d, tn), lambda i, j: (i, 0, j)),
            pl.BlockSpec((None, 1, tn), lambda i, j: (i, 0, j)),
        ],
        out_specs=pl.BlockSpec((None, rows, tn), lambda i, j: (i, 0, j)),
        compiler_params=_cparams("parallel", "parallel"),
        name="adaln",
    )(cc, w_ada, b_ada.reshape(depth, 1, d3))


def _project(x, sh_ref, sc_ref, nw_ref, wt_ref, wdt_ref, out_refs, dt_ref, splits, col_tile):
    ms = jnp.mean(x * x, axis=-1, keepdims=True)
    h = (x * lax.rsqrt(ms + EPS)) * nw_ref[...]
    h = h * (1.0 + sc_ref[...]) + sh_ref[...]
    col = 0
    for out_ref, width in zip(out_refs, splits):
        for c0 in range(0, width, col_tile):
            out_ref[:, c0:c0 + col_tile] = lax.dot_general(
                h, wt_ref[col + c0:col + c0 + col_tile, :], (((1,), (1,)), ((), ())),
                preferred_element_type=F32).astype(BF16)
        col += width
    dt_t = lax.dot_general(wdt_ref[...], h, (((1,), (1,)), ((), ())), preferred_element_type=F32)
    for k in range(dt_ref.shape[0]):
        dt_ref[k] = dt_t[:, k * CHUNK:(k + 1) * CHUNK]


def _residual(four_ref, ssd_ref, x_ref, gate_ref, wo_ref):
    d_f = four_ref.shape[1]
    acc = jnp.dot(four_ref[...].astype(F32), wo_ref[0:d_f, :], preferred_element_type=F32)
    acc = acc + jnp.dot(ssd_ref[...].astype(F32), wo_ref[d_f:, :], preferred_element_type=F32)
    return x_ref[...] + gate_ref[...] * acc


def _inproj_kernel(x_ref, sh_ref, sc_ref, nw_ref, wt_ref, wdt_ref,
                   u_ref, zf_ref, zs_ref, xbc_ref, dt_ref, *, splits, col_tile):
    _project(x_ref[...], sh_ref, sc_ref, nw_ref, wt_ref, wdt_ref, (u_ref, zf_ref, zs_ref, xbc_ref),
             dt_ref, splits, col_tile)


def _outproj_inproj_kernel(four_ref, ssd_ref, x_ref, gate_ref, wo_ref, sh_ref, sc_ref, nw_ref, wt_ref,
                           wdt_ref, xo_ref, u_ref, zf_ref, zs_ref, xbc_ref, dt_ref, *, splits, col_tile):
    x_new = _residual(four_ref, ssd_ref, x_ref, gate_ref, wo_ref)
    xo_ref[...] = x_new
    _project(x_new, sh_ref, sc_ref, nw_ref, wt_ref, wdt_ref, (u_ref, zf_ref, zs_ref, xbc_ref),
             dt_ref, splits, col_tile)


def _inproj(x, shift, scale, norm_w, w_in_t, w_dt, layer, splits, prev=None):
    bsz, length, d = x.shape
    tm = min(length, PROJ_ROW_TILE)
    n_dt = w_dt.shape[1]
    row_map = lambda b, t: (b, t, 0)

    def per_batch(arr):
        return pl.BlockSpec((None, 1, d), (lambda b, t: (b, 0, 0)) if arr.shape[0] > 1
                            else (lambda b, t: (0, 0, 0)))

    out_shape = [jax.ShapeDtypeStruct((bsz, length, w), BF16) for w in splits]
    out_shape.append(jax.ShapeDtypeStruct((bsz, length // CHUNK, n_dt, CHUNK), F32))
    out_specs = [pl.BlockSpec((None, tm, w), row_map) for w in splits]
    out_specs.append(pl.BlockSpec((None, tm // CHUNK, n_dt, CHUNK), lambda b, t: (b, t, 0, 0)))
    in_specs = [pl.BlockSpec((None, tm, d), row_map), per_batch(shift), per_batch(scale),
                _resident((1, d)), _resident(w_in_t.shape[1:], layer), _resident(w_dt.shape[1:], layer)]
    args = [x, shift, scale, norm_w, w_in_t, w_dt]
    body = _inproj_kernel
    if prev is not None:
        four, ssd, gate, w_out = prev
        in_specs = [pl.BlockSpec((None, tm, four.shape[2]), row_map),
                    pl.BlockSpec((None, tm, ssd.shape[2]), row_map),
                    in_specs[0], per_batch(gate), _resident(w_out.shape[1:], layer - 1)] + in_specs[1:]
        args = [four, ssd, x, gate, w_out] + args[1:]
        out_shape = [jax.ShapeDtypeStruct((bsz, length, d), F32)] + out_shape
        out_specs = [pl.BlockSpec((None, tm, d), row_map)] + out_specs
        body = _outproj_inproj_kernel
    return pl.pallas_call(
        functools.partial(body, splits=splits, col_tile=PROJ_COL_TILE),
        out_shape=out_shape,
        grid=(bsz, length // tm),
        in_specs=in_specs,
        out_specs=out_specs,
        compiler_params=_cparams("parallel", "parallel"),
        name="inproj" if prev is None else "outproj_inproj",
    )(*args)


def _reverse_into(src_ref, src_end, dst_ref, dst0, count):
    blk = LANES
    ri = lax.broadcasted_iota(jnp.int32, (blk, 2 * blk), 0)
    ci = lax.broadcasted_iota(jnp.int32, (blk, 2 * blk), 1)
    rev = ((ri + ci == blk) | ((ri == 0) & (ci == blk))).astype(BF16)
    for i in range(count // blk):
        lo = src_ref[src_end - (i + 1) * blk:src_end - i * blk, :]
        hi = (src_ref[src_end - i * blk:src_end - (i - 1) * blk, :] if i > 0
              else jnp.zeros_like(lo))
        dst_ref[dst0 + i * blk:dst0 + (i + 1) * blk, :] = jnp.dot(
            rev, jnp.concatenate([lo, hi], axis=0), preferred_element_type=F32).astype(BF16)


def _fourier_kernel(u_ref, zf_ref, fseq_ref, fch_ref, wf_ref, bf_ref, o_ref,
                    urev_ref, ucs_ref, mix_ref, mirror_ref, *, row_tile):
    length, width = u_ref.shape
    half = length // 2
    scale = 1.0 / math.sqrt(length)

    _reverse_into(u_ref, length, urev_ref, 0, half)
    mids = []
    for c0 in range(0, width, MXU_WIDTH):
        cols = slice(c0, c0 + MXU_WIDTH)
        uh = u_ref[0:half, cols]
        ur = urev_ref[:, cols]
        cos_ch = fch_ref[cols, cols]
        sin_ch = fch_ref[cols, width + c0:width + c0 + MXU_WIDTH]
        ucs_ref[0:half, cols] = (jnp.dot(uh, cos_ch, preferred_element_type=F32)
                                 + jnp.dot(ur, cos_ch, preferred_element_type=F32)).astype(BF16)
        ucs_ref[half:length, cols] = (jnp.dot(uh, sin_ch, preferred_element_type=F32)
                                      - jnp.dot(ur, sin_ch, preferred_element_type=F32)).astype(BF16)
        mids.append(jnp.dot(u_ref[half:half + BF16_SUBLANES, cols], cos_ch,
                            preferred_element_type=F32)[0:1, :] * scale)
    mid = jnp.concatenate(mids, axis=1)

    for r in range(half // row_tile):
        rows = slice(r * row_tile, (r + 1) * row_tile)
        c_part = jnp.dot(fseq_ref[rows, 0:half], ucs_ref[0:half, :], preferred_element_type=F32)
        s_part = jnp.dot(fseq_ref[rows, half:length], ucs_ref[half:length, :],
                         preferred_element_type=F32)
        k = r * row_tile + lax.broadcasted_iota(jnp.int32, (row_tile, 1), 0)
        base = c_part + (1 - 2 * (k & 1)).astype(F32) * mid
        mix_ref[rows, :] = (base + s_part).astype(BF16)
        mirror_ref[rows, :] = (base - s_part).astype(BF16)
    _reverse_into(mirror_ref, half, mix_ref, half, half)
    li = lax.broadcasted_iota(jnp.int32, (BF16_SUBLANES, half), 1)
    alt = (1 - 2 * (li & 1)).astype(BF16)
    row_half = (jnp.dot(alt, ucs_ref[0:half, :], preferred_element_type=F32)[0:1, :] * scale
                + (1.0 if half % 2 == 0 else -1.0) * mid)
    first = mix_ref[half:half + BF16_SUBLANES, :]
    sub = lax.broadcasted_iota(jnp.int32, first.shape, 0)
    mix_ref[half:half + BF16_SUBLANES, :] = jnp.where(sub == 0, row_half.astype(BF16), first)

    for r in range(length // row_tile):
        rows = slice(r * row_tile, (r + 1) * row_tile)
        lin = jnp.dot(mix_ref[rows, :], wf_ref[...], preferred_element_type=F32) + bf_ref[...]
        o_ref[rows, :] = (lin * _silu(zf_ref[rows, :].astype(F32))).astype(BF16)


def _fourier(u, zf, fseq, fch, w_f, layer, b_f):
    bsz, length, width = u.shape
    half = length // 2
    assert half % LANES == 0, "the block-wise reversal needs L/2 to be a multiple of 128"
    assert width % MXU_WIDTH == 0 and MXU_WIDTH % FOURIER_GROUP == 0
    batch_block = pl.BlockSpec((None, length, width), lambda b: (b, 0, 0))
    return pl.pallas_call(
        functools.partial(_fourier_kernel, row_tile=min(half, FOURIER_ROW_TILE)),
        out_shape=jax.ShapeDtypeStruct((bsz, length, width), BF16),
        grid=(bsz,),
        in_specs=[batch_block, batch_block, _resident(fseq.shape), _resident(fch.shape),
                  _resident(w_f.shape[1:], layer), _resident(b_f.shape)],
        out_specs=batch_block,
        scratch_shapes=[pltpu.VMEM((half, width), BF16),
                        pltpu.VMEM((length, width), BF16),
                        pltpu.VMEM((length, width), BF16),
                        pltpu.VMEM((half, width), BF16)],
        compiler_params=_cparams("parallel"),
        name="fourier",
    )(u, zf, fseq, fch, w_f, b_f)


@functools.lru_cache(maxsize=None)
def _dft_tables(length, width):
    half = length // 2
    kl = (np.arange(half)[:, None] * np.arange(half)[None, :]) % length
    ang = kl * (2.0 * np.pi / length)
    fseq = np.concatenate([np.cos(ang), -np.sin(ang)], axis=1) / np.sqrt(length)
    ch = np.arange(width)
    same = (ch[:, None] // FOURIER_GROUP) == (ch[None, :] // FOURIER_GROUP)
    angc = (((ch[:, None] % FOURIER_GROUP) * (ch[None, :] % FOURIER_GROUP)) % FOURIER_GROUP) * (
        2.0 * np.pi / FOURIER_GROUP)
    fch = np.concatenate([np.where(same, np.cos(angc), 0.0),
                          np.where(same, np.sin(angc), 0.0)], axis=1) / np.sqrt(FOURIER_GROUP)
    return fseq.astype(np.float32), fch.astype(np.float32)


HP = BF16_SUBLANES
CONV_SUB = 128


def _split2(v):
    hi = v.astype(BF16)
    return hi, (v - hi.astype(F32)).astype(BF16)


def _split3(v):
    hi, _ = _split2(v)
    r1 = v - hi.astype(F32)
    mid, lo = _split2(r1)
    return hi, mid, lo


def _expand_rows(cols, n_tiles, lane_lo):
    tiles = [jnp.where(lane_lo, cols[:, 2 * t:2 * t + 1], cols[:, 2 * t + 1:2 * t + 2])
             for t in range(n_tiles)]
    return jnp.concatenate(tiles, axis=1)


def _ssd_kernel(*refs, segments, hpg):
    n_seg = len(segments)
    seg_refs = [refs[6 * i:6 * i + 6] for i in range(n_seg)]
    pos = 6 * n_seg
    cwx_ref, cwb_ref, cwc_ref, cbx_ref, cbb_ref, cbc_ref, dsk_ref, nw_ref = refs[pos:pos + 8]
    pos += 8
    y_refs = []
    for seg in segments:
        y_refs.append(refs[pos] if seg.emit_y else None)
        pos += 1 if seg.emit_y else 0
    (sm_ref, s0_ref, sp_ref, cc_ref, rowq_ref, tblk_ref, tb_ref, yd_ref, contrib_ref, dec_ref,
     sball_ref, sf_ref, sb_ref) = refs[pos:]

    gw = hpg * HEAD_DIM
    n_tiles = gw // LANES
    conv_w = s0_ref.shape[1]
    shifted = (sm_ref, s0_ref, sp_ref)
    ri = lax.broadcasted_iota(jnp.int32, (CHUNK, CHUNK), 0)
    ci = lax.broadcasted_iota(jnp.int32, (CHUNK, CHUNK), 1)
    tri_t = (ri <= ci).astype(BF16)
    below = ci < ri
    on_diag = ci == ri
    lane_lo = lax.broadcasted_iota(jnp.int32, (1, LANES), 1) < HEAD_DIM
    erow = lax.broadcasted_iota(jnp.int32, (CHUNK, 2 * gw), 0)
    ehead = lax.broadcasted_iota(jnp.int32, (CHUNK, 2 * gw), 1) // HEAD_DIM
    expand_w = (erow == ehead + 2 * HP).astype(BF16)
    expand_off = (erow == ehead + 3 * HP).astype(BF16)

    def chunk_rows(c):
        return pl.ds(pl.multiple_of(c * CHUNK, CHUNK), CHUNK)

    def run_segment(x_ref, b_ref, c_ref, z_ref, dtr_ref, pr_ref, y_ref, seg):
        length, grid_rows, grid_cols = seg.length, seg.grid_rows, seg.grid_cols
        nc = length // CHUNK
        nrow = nc * HP

        halo = grid_cols if grid_rows > 1 else 0
        if halo:
            zeros = jnp.zeros((halo, conv_w), BF16)
            for s_ref in shifted:
                s_ref[0:halo, :] = zeros
                s_ref[halo + length:2 * halo + length, :] = zeros
        sources = ((x_ref, 0, gw), (b_ref, gw, D_STATE), (c_ref, gw + D_STATE, D_STATE))

        def shift_row(r, carry):
            src_rows = pl.ds(pl.multiple_of(r * grid_cols, grid_cols), grid_cols)
            dst_rows = pl.ds(pl.multiple_of(halo + r * grid_cols, grid_cols), grid_cols)
            for ref, lane0, width in sources:
                wpos = lax.broadcasted_iota(jnp.int32, (grid_cols, width), 0)
                xb = ref[src_rows, :]
                xr = xb.astype(F32)
                dst = slice(lane0, lane0 + width)
                s0_ref[dst_rows, dst] = xb
                sm_ref[dst_rows, dst] = jnp.where(
                    wpos == 0, 0.0, pltpu.roll(xr, 1, axis=0)).astype(BF16)
                sp_ref[dst_rows, dst] = jnp.where(
                    wpos == grid_cols - 1, 0.0, pltpu.roll(xr, grid_cols - 1, axis=0)).astype(BF16)
            return carry

        lax.fori_loop(0, grid_rows, shift_row, 0)
        dhs = (-1, 0, 1) if grid_rows > 1 else (0,)

        def conv_tile(c, lane0, w_ref, b_ref_, t):
            wl = slice(t * LANES, (t + 1) * LANES)
            sl = slice(lane0 + t * LANES, lane0 + (t + 1) * LANES)
            bias = b_ref_[:, wl]
            outs = []
            for s in range(CHUNK // CONV_SUB):
                base = pl.multiple_of(c * CHUNK, CHUNK) + s * CONV_SUB
                acc = bias
                for dh in dhs:
                    off = halo + base + dh * grid_cols
                    part = None
                    for kw, s_ref in enumerate(shifted):
                        k = 3 * (dh + 1) + kw
                        term = w_ref[k:k + 1, wl].astype(BF16) * s_ref[pl.ds(off, CONV_SUB), sl]
                        part = term if part is None else part + term
                    acc = acc + part.astype(F32)
                outs.append(_silu(acc))
            return jnp.concatenate(outs, axis=0)

        row_fwd = (lax.broadcasted_iota(jnp.int32, (nrow, 1), 0) & (HP - 1)) < hpg
        dt_r = _softplus(dtr_ref[...].reshape(nrow, CHUNK) + pr_ref[:, 0:1])
        a_r = dt_r * (-jnp.exp(pr_ref[:, 1:2]))
        cum_r = sum(jnp.dot(p, tri_t, preferred_element_type=F32) for p in _split3(a_r))
        excl_r = cum_r - a_r
        tot_r = jnp.broadcast_to(cum_r[:, CHUNK - 1:CHUNK], cum_r.shape)
        log_dt = jnp.log(dt_r)
        rowq_ref[0, 0:nrow, :] = dt_r
        rowq_ref[1, 0:nrow, :] = cum_r - log_dt
        rowq_ref[2, 0:nrow, :] = excl_r + log_dt
        to_cols = (
            jnp.where(row_fwd, cum_r, excl_r),
            tot_r,
            jnp.where(row_fwd, jnp.exp(tot_r - cum_r), jnp.exp(excl_r)) * dt_r,
            jnp.where(row_fwd, jnp.exp(cum_r), jnp.exp(tot_r - excl_r)),
        )
        for c in range(nc):
            for q, val in enumerate(to_cols):
                tblk_ref[c, q * HP:(q + 1) * HP, :] = val[c * HP:(c + 1) * HP, :]
            tblk_ref[c, len(to_cols) * HP:, :] = jnp.zeros((CHUNK - len(to_cols) * HP, CHUNK), F32)

        def phase_a(c, carry):
            rows = chunk_rows(c)
            rrows = pl.ds(pl.multiple_of(c * HP, HP), HP)
            blk_t = tblk_ref[c].T
            blk_tb = blk_t.astype(BF16)
            tb_ref[c] = blk_tb
            dt_row = rowq_ref[0, rrows, :]
            cf_row = rowq_ref[1, rrows, :]
            eb_row = rowq_ref[2, rrows, :]

            bmat = conv_tile(c, gw, cwb_ref, cbb_ref, 0).astype(BF16)
            cmat = conv_tile(c, gw + D_STATE, cwc_ref, cbc_ref, 0).astype(BF16)
            cc_ref[rows, :] = cmat
            cb = lax.dot_general(cmat, bmat, (((1,), (1,)), ((), ())), preferred_element_type=F32)

            def mix(h):
                arg = jnp.where(below, blk_t[:, h:h + 1] - cf_row[h:h + 1, :],
                                eb_row[hpg + h:hpg + h + 1, :] - blk_t[:, hpg + h:hpg + h + 1])
                e = jnp.exp(arg) + jnp.where(on_diag, dt_row[h:h + 1, :], 0.0)
                return (cb * e).astype(BF16)

            w_exp = jnp.dot(blk_tb, expand_w, preferred_element_type=F32)

            xs_f, xs_b = [], []
            for t in range(n_tiles):
                lanes = slice(t * LANES, (t + 1) * LANES)
                xf = conv_tile(c, 0, cwx_ref, cbx_ref, t)
                xt = xf.astype(BF16)
                zero = jnp.zeros_like(xt)
                rhs = jnp.concatenate([jnp.where(lane_lo, xt, zero), jnp.where(lane_lo, zero, xt)],
                                      axis=0)
                lhs = jnp.concatenate([mix(2 * t), mix(2 * t + 1)], axis=1)
                yd_ref[rows, lanes] = (jnp.dot(lhs, rhs, preferred_element_type=F32)
                                       + dsk_ref[:, lanes] * xf)
                xs_f.append((xf * w_exp[:, t * LANES:(t + 1) * LANES]).astype(BF16))
                xs_b.append((xf * w_exp[:, gw + t * LANES:gw + (t + 1) * LANES]).astype(BF16))
            contrib_ref[c] = lax.dot_general(bmat, jnp.concatenate(xs_f + xs_b, axis=1),
                                             (((0,), (0,)), ((), ())), preferred_element_type=F32)
            dec_ref[c] = _expand_rows(jnp.exp(blk_t[0:1, HP:2 * HP]), 2 * n_tiles, lane_lo)
            return carry

        lax.fori_loop(0, nc, phase_a, 0, unroll=min(nc, 8))

        def phase_b(k, carry):
            c = nc - 1 - k
            sb = sb_ref[...]
            sball_ref[c] = sb.astype(BF16)
            sb_ref[...] = sb * dec_ref[c][:, gw:] + contrib_ref[c, :, gw:]
            return carry

        lax.fori_loop(0, nc, phase_b, 0, unroll=min(nc, 4))

        def phase_c(c, carry):
            rows = chunk_rows(c)
            sf = sf_ref[...]
            if y_ref is not None:
                s_cat = jnp.concatenate([sf.astype(BF16), sball_ref[c]], axis=1)
                cs = jnp.dot(cc_ref[rows, :], s_cat, preferred_element_type=F32)
                off = jnp.dot(tb_ref[c], expand_off, preferred_element_type=F32)
                g_tiles = []
                ssq = jnp.zeros((CHUNK, 1), F32)
                for t in range(n_tiles):
                    lanes = slice(t * LANES, (t + 1) * LANES)
                    blanes = slice(gw + t * LANES, gw + (t + 1) * LANES)
                    y = (yd_ref[rows, lanes] + cs[:, lanes] * off[:, lanes]
                         + cs[:, blanes] * off[:, blanes])
                    g = y * _silu(z_ref[rows, lanes].astype(F32))
                    ssq = ssq + jnp.sum(g * g, axis=-1, keepdims=True)
                    g_tiles.append(g)
                inv = lax.rsqrt(ssq * (1.0 / gw) + EPS)
                for t in range(n_tiles):
                    lanes = slice(t * LANES, (t + 1) * LANES)
                    y_ref[rows, lanes] = ((g_tiles[t] * inv) * nw_ref[:, lanes]).astype(BF16)
            sf_ref[...] = sf * dec_ref[c][:, :gw] + contrib_ref[c, :, :gw]
            return carry

        lax.fori_loop(0, nc, phase_c, 0, unroll=min(nc, 8))

    sf_ref[...] = jnp.zeros(sf_ref.shape, F32)
    sb_ref[...] = jnp.zeros(sb_ref.shape, F32)
    for (x_ref, b_ref, c_ref, z_ref, dtr_ref, pr_ref), y_ref, seg in zip(seg_refs, y_refs, segments):
        run_segment(x_ref, b_ref, c_ref, z_ref, dtr_ref, pr_ref, y_ref, seg)


class _Segment(NamedTuple):
    length: int
    grid_rows: int
    grid_cols: int
    emit_y: bool


def _ssd(token_sets, p_col, conv_w9, conv_b, d_exp, norm_w, d_ssd):
    assert HEADS_PER_TILE == 2, "the kernel pairs heads 2t, 2t + 1 on one 128-lane tile"
    g = N_BC_GROUPS
    gw = d_ssd // g
    hpg = gw // HEAD_DIM
    bc_first = d_ssd // D_STATE
    pad = HP - 2 * hpg
    p_col = jnp.pad(p_col, ((0, 0), (0, 0), (0, pad)))
    bsz = token_sets[0][0].shape[0]
    segments, args, in_specs, out_shape, out_specs = [], [], [], [], []
    for xbc, zs, dt, grid_rows, grid_cols, emit_y in token_sets:
        length = xbc.shape[1]
        nc = length // CHUNK
        segments.append(_Segment(length, grid_rows, grid_cols, emit_y))
        p_row = jnp.tile(p_col.transpose(0, 2, 1), (1, nc, 1))
        args += [xbc, xbc, xbc, zs, dt, p_row]
        in_specs += [
            pl.BlockSpec((None, length, gw), lambda b, j: (b, 0, j)),
            pl.BlockSpec((None, length, D_STATE), lambda b, j: (b, 0, bc_first + j)),
            pl.BlockSpec((None, length, D_STATE), lambda b, j: (b, 0, bc_first + g + j)),
            pl.BlockSpec((None, length, gw), lambda b, j: (b, 0, j)),
            pl.BlockSpec((None, nc, HP, CHUNK), lambda b, j: (b, 0, j, 0)),
            pl.BlockSpec((None, nc * HP, 2), lambda b, j: (j, 0, 0)),
        ]
        if emit_y:
            out_shape.append(jax.ShapeDtypeStruct((bsz, length, d_ssd), BF16))
            out_specs.append(pl.BlockSpec((None, length, gw), lambda b, j: (b, 0, j)))
    args += [conv_w9, conv_w9, conv_w9, conv_b, conv_b, conv_b, d_exp, norm_w]
    in_specs += [
        pl.BlockSpec((9, gw), lambda b, j: (0, j)),
        pl.BlockSpec((9, D_STATE), lambda b, j: (0, bc_first + j)),
        pl.BlockSpec((9, D_STATE), lambda b, j: (0, bc_first + g + j)),
        pl.BlockSpec((1, gw), lambda b, j: (0, j)),
        pl.BlockSpec((1, D_STATE), lambda b, j: (0, bc_first + j)),
        pl.BlockSpec((1, D_STATE), lambda b, j: (0, bc_first + g + j)),
        pl.BlockSpec((1, gw), lambda b, j: (0, j)),
        pl.BlockSpec((1, gw), lambda b, j: (0, j)),
    ]
    max_len = max(seg.length for seg in segments)
    max_nc = max_len // CHUNK
    conv_rows = max(seg.length + (2 * seg.grid_cols if seg.grid_rows > 1 else 0) for seg in segments)
    conv_w = gw + 2 * D_STATE
    outs = pl.pallas_call(
        functools.partial(_ssd_kernel, segments=tuple(segments), hpg=hpg),
        out_shape=out_shape,
        grid=(bsz, g),
        in_specs=in_specs,
        out_specs=out_specs,
        scratch_shapes=[pltpu.VMEM((conv_rows, conv_w), BF16),
                        pltpu.VMEM((conv_rows, conv_w), BF16),
                        pltpu.VMEM((conv_rows, conv_w), BF16),
                        pltpu.VMEM((max_len, D_STATE), BF16),
                        pltpu.VMEM((3, max_nc * HP, CHUNK), F32),
                        pltpu.VMEM((max_nc, CHUNK, CHUNK), F32),
                        pltpu.VMEM((max_nc, CHUNK, CHUNK), BF16),
                        pltpu.VMEM((max_len, gw), F32),
                        pltpu.VMEM((max_nc, D_STATE, 2 * gw), F32),
                        pltpu.VMEM((max_nc, 1, 2 * gw), F32),
                        pltpu.VMEM((max_nc, D_STATE, gw), BF16),
                        pltpu.VMEM((D_STATE, gw), F32),
                        pltpu.VMEM((D_STATE, gw), F32)],
        compiler_params=_cparams("parallel", "parallel"),
        name="ssd",
    )(*args)
    outs = list(outs)
    return [outs.pop(0) if seg.emit_y else None for seg in segments]


def _final_kernel(four_ref, ssd_ref, x_ref, gate_ref, w_ref, fnw_ref, o_ref):
    y = _residual(four_ref, ssd_ref, x_ref, gate_ref, w_ref)
    ms = jnp.mean(y * y, axis=-1, keepdims=True)
    o_ref[...] = (y * lax.rsqrt(ms + EPS)) * fnw_ref[...]


def _final(four, ssd, x, gate, w_out, layer, final_norm_w):
    bsz, length, d = x.shape
    tm = min(length, FINAL_ROW_TILE)
    row_map = lambda b, t: (b, t, 0)
    return pl.pallas_call(
        _final_kernel,
        out_shape=jax.ShapeDtypeStruct((bsz, length, d), F32),
        grid=(bsz, length // tm),
        in_specs=[
            pl.BlockSpec((None, tm, four.shape[2]), row_map),
            pl.BlockSpec((None, tm, ssd.shape[2]), row_map),
            pl.BlockSpec((None, tm, d), row_map),
            pl.BlockSpec((None, 1, d), lambda b, t: (b, 0, 0)),
            _resident(w_out.shape[1:], layer),
            _resident((1, d)),
        ],
        out_specs=pl.BlockSpec((None, tm, d), row_map),
        compiler_params=_cparams("parallel", "parallel"),
        name="final",
    )(four, ssd, x, gate, w_out, final_norm_w)


def kernel(x, c, ctx, c_ctx, norm_w, w_ada, b_ada, w_in, conv_w, conv_b, dt_bias, a_log, d_skip,
           ssd_norm_w, w_fourier, b_fourier, w_out, final_norm_w):
    bsz, seq_len, d = x.shape
    ctx_len = ctx.shape[1]
    depth = w_in.shape[0]
    d_f = w_fourier.shape[1]
    d_ssd = ssd_norm_w.shape[1]
    conv_ch = conv_w.shape[-1]
    n_heads = d_skip.shape[1]
    g = N_BC_GROUPS
    hpg = n_heads // g
    splits = (d_f, d_f, d_ssd, conv_ch)

    pad_rows = (-(bsz + 1)) % 8
    cc = jnp.concatenate([c, c_ctx[None, :], jnp.zeros((pad_rows, d), F32)], axis=0)
    mod = _adaln(cc, w_ada, b_ada)

    w_in_t = jnp.swapaxes(w_in, 1, 2)
    w_dt = w_in[:, :, sum(splits):].reshape(depth, d, 2, g, hpg).transpose(0, 3, 2, 4, 1)
    w_dt = jnp.pad(w_dt.reshape(depth, g, 2 * hpg, d), ((0, 0), (0, 0), (0, HP - 2 * hpg), (0, 0)))
    w_dt = w_dt.reshape(depth, g * HP, d)
    w_f_b = w_fourier.astype(BF16)
    w9 = conv_w.reshape(depth, 9, conv_ch)
    par = jnp.stack([dt_bias, a_log], axis=1)
    p_col = par.reshape(depth, 2, 2, g, hpg).transpose(0, 3, 1, 2, 4).reshape(depth, g, 2, 2 * hpg)
    d_exp = jnp.repeat(d_skip, HEAD_DIM, axis=1).reshape(depth, 1, d_ssd)

    fseq_x, fch = (jnp.asarray(t).astype(BF16) for t in _dft_tables(seq_len, d_f))
    fseq_c = jnp.asarray(_dft_tables(ctx_len, d_f)[0]).astype(BF16)
    rows = seq_len // GRID_W

    ctx = ctx.reshape(1, bsz * ctx_len, d)
    prev_c = prev_x = None
    for i in range(depth):
        last = i == depth - 1
        m = mod[i]
        shift, scale, gate = (m[:bsz, None, k * d:(k + 1) * d] for k in range(3))
        shift_c, scale_c, gate_c = (m[bsz:bsz + 1, None, k * d:(k + 1) * d] for k in range(3))
        nw_i = norm_w[i].reshape(1, d)
        outs_c = _inproj(ctx, shift_c, scale_c, nw_i, w_in_t, w_dt, i, splits, prev_c)
        outs_x = _inproj(x, shift, scale, nw_i, w_in_t, w_dt, i, splits, prev_x)
        if i > 0:
            ctx, outs_c = outs_c[0], outs_c[1:]
            x, outs_x = outs_x[0], outs_x[1:]
        u_c, zf_c, zs_c, xbc_c = (t.reshape(bsz, ctx_len, t.shape[-1]) for t in outs_c[:4])
        dt_c = outs_c[4].reshape(bsz, ctx_len // CHUNK, *outs_c[4].shape[2:])
        u_x, zf_x, zs_x, xbc_x, dt_x = outs_x
        ssd_c, ssd_x = _ssd(
            [(xbc_c, zs_c, dt_c, 1, ctx_len, not last), (xbc_x, zs_x, dt_x, rows, GRID_W, True)],
            p_col[i], w9[i], conv_b[i].reshape(1, conv_ch), d_exp[i],
            ssd_norm_w[i].reshape(1, d_ssd), d_ssd)
        b_f = b_fourier[i].reshape(1, d_f)
        four_x = _fourier(u_x, zf_x, fseq_x, fch, w_f_b, i, b_f)
        prev_x = (four_x, ssd_x, gate, w_out)
        if not last:
            four_c = _fourier(u_c, zf_c, fseq_c, fch, w_f_b, i, b_f)
            prev_c = (four_c.reshape(1, bsz * ctx_len, d_f), ssd_c.reshape(1, bsz * ctx_len, d_ssd),
                      gate_c, w_out)
    four_x, ssd_x, gate, w_o = prev_x
    return _final(four_x, ssd_x, x, gate, w_o, depth - 1, final_norm_w.reshape(1, d))
```

```python
import functools
import math
from typing import NamedTuple

import jax
import jax.numpy as jnp
import numpy as np
from jax import lax
from jax.experimental import pallas as pl
from jax.experimental.pallas import tpu as pltpu

F32 = jnp.float32
BF16 = jnp.bfloat16

GRID_W = 64
FOURIER_GROUP = 64
HEAD_DIM = 64
N_BC_GROUPS = 4
D_STATE = 128
CHUNK = 128
EPS = 1e-6
LOG2E = 1.4426950408889634

LANES = 128
BF16_SUBLANES = 16
MXU_WIDTH = 256
VMEM_LIMIT_BYTES = 56 * 1024 * 1024

PROJ_ROW_TILE = 512
PROJ_COL_TILE = 512
FINAL_ROW_TILE = 1024
ADALN_COL_TILE = 1024
FOURIER_ROW_TILE = 256

HEADS_PER_TILE = LANES // HEAD_DIM


def _cparams(*sem):
    return pltpu.CompilerParams(dimension_semantics=sem, vmem_limit_bytes=VMEM_LIMIT_BYTES)


def _silu(v):
    return v * (1.0 / (1.0 + jnp.exp2(v * (-LOG2E))))


def _softplus(v):
    return jnp.maximum(v, 0.0) + jnp.log1p(jnp.exp(-jnp.abs(v)))


def _resident(shape, layer=None):
    zeros = (0,) * len(shape)
    if layer is None:
        return pl.BlockSpec(shape, lambda *_: zeros, pipeline_mode=pl.Buffered(1))
    return pl.BlockSpec((None,) + tuple(shape), lambda *_: (layer,) + zeros,
                        pipeline_mode=pl.Buffered(1))


def _adaln_kernel(c_ref, w_ref, b_ref, o_ref):
    s = _silu(c_ref[...])
    o_ref[...] = jnp.dot(s, w_ref[...], preferred_element_type=F32) + b_ref[...]


def _adaln(cc, w_ada, b_ada):
    depth, d, d3 = w_ada.shape
    rows = cc.shape[0]
    tn = ADALN_COL_TILE
    return pl.pallas_call(
        _adaln_kernel,
        out_shape=jax.ShapeDtypeStruct((depth, rows, d3), F32),
        grid=(depth, d3 // tn),
        in_specs=[
            pl.BlockSpec((rows, d), lambda i, j: (0, 0)),
            pl.BlockSpec((None, d, tn), lambda i, j: (i, 0, j)),
            pl.BlockSpec((None, 1, tn), lambda i, j: (i, 0, j)),
        ],
        out_specs=pl.BlockSpec((None, rows, tn), lambda i, j: (i, 0, j)),
        compiler_params=_cparams("parallel", "parallel"),
        name="adaln",
    )(cc, w_ada, b_ada.reshape(depth, 1, d3))


def _project(x, sh_ref, sc_ref, nw_ref, wt_ref, wdt_ref, out_refs, dt_ref, splits, col_tile):
    ms = jnp.mean(x * x, axis=-1, keepdims=True)
    h = (x * lax.rsqrt(ms + EPS)) * nw_ref[...]
    h = h * (1.0 + sc_ref[...]) + sh_ref[...]
    col = 0
    for out_ref, width in zip(out_refs, splits):
        for c0 in range(0, width, col_tile):
            out_ref[:, c0:c0 + col_tile] = lax.dot_general(
                h, wt_ref[col + c0:col + c0 + col_tile, :], (((1,), (1,)), ((), ())),
                preferred_element_type=F32).astype(BF16)
        col += width
    dt_t = lax.dot_general(wdt_ref[...], h, (((1,), (1,)), ((), ())), preferred_element_type=F32)
    for k in range(dt_ref.shape[0]):
        dt_ref[k] = dt_t[:, k * CHUNK:(k + 1) * CHUNK]


def _residual(four_ref, ssd_ref, x_ref, gate_ref, wo_ref):
    d_f = four_ref.shape[1]
    acc = jnp.dot(four_ref[...].astype(F32), wo_ref[0:d_f, :], preferred_element_type=F32)
    acc = acc + jnp.dot(ssd_ref[...].astype(F32), wo_ref[d_f:, :], preferred_element_type=F32)
    return x_ref[...] + gate_ref[...] * acc


def _inproj_kernel(x_ref, sh_ref, sc_ref, nw_ref, wt_ref, wdt_ref,
                   u_ref, zf_ref, zs_ref, xbc_ref, dt_ref, *, splits, col_tile):
    _project(x_ref[...], sh_ref, sc_ref, nw_ref, wt_ref, wdt_ref, (u_ref, zf_ref, zs_ref, xbc_ref),
             dt_ref, splits, col_tile)


def _outproj_inproj_kernel(four_ref, ssd_ref, x_ref, gate_ref, wo_ref, sh_ref, sc_ref, nw_ref, wt_ref,
                           wdt_ref, xo_ref, u_ref, zf_ref, zs_ref, xbc_ref, dt_ref, *, splits, col_tile):
    x_new = _residual(four_ref, ssd_ref, x_ref, gate_ref, wo_ref)
    xo_ref[...] = x_new
    _project(x_new, sh_ref, sc_ref, nw_ref, wt_ref, wdt_ref, (u_ref, zf_ref, zs_ref, xbc_ref),
             dt_ref, splits, col_tile)


def _inproj(x, shift, scale, norm_w, w_in_t, w_dt, layer, splits, prev=None):
    bsz, length, d = x.shape
    tm = min(length, PROJ_ROW_TILE)
    n_dt = w_dt.shape[1]
    row_map = lambda b, t: (b, t, 0)

    def per_batch(arr):
        return pl.BlockSpec((None, 1, d), (lambda b, t: (b, 0, 0)) if arr.shape[0] > 1
                            else (lambda b, t: (0, 0, 0)))

    out_shape = [jax.ShapeDtypeStruct((bsz, length, w), BF16) for w in splits]
    out_shape.append(jax.ShapeDtypeStruct((bsz, length // CHUNK, n_dt, CHUNK), F32))
    out_specs = [pl.BlockSpec((None, tm, w), row_map) for w in splits]
    out_specs.append(pl.BlockSpec((None, tm // CHUNK, n_dt, CHUNK), lambda b, t: (b, t, 0, 0)))
    in_specs = [pl.BlockSpec((None, tm, d), row_map), per_batch(shift), per_batch(scale),
                _resident((1, d)), _resident(w_in_t.shape[1:], layer), _resident(w_dt.shape[1:], layer)]
    args = [x, shift, scale, norm_w, w_in_t, w_dt]
    body = _inproj_kernel
    if prev is not None:
        four, ssd, gate, w_out = prev
        in_specs = [pl.BlockSpec((None, tm, four.shape[2]), row_map),
                    pl.BlockSpec((None, tm, ssd.shape[2]), row_map),
                    in_specs[0], per_batch(gate), _resident(w_out.shape[1:], layer - 1)] + in_specs[1:]
        args = [four, ssd, x, gate, w_out] + args[1:]
        out_shape = [jax.ShapeDtypeStruct((bsz, length, d), F32)] + out_shape
        out_specs = [pl.BlockSpec((None, tm, d), row_map)] + out_specs
        body = _outproj_inproj_kernel
    return pl.pallas_call(
        functools.partial(body, splits=splits, col_tile=PROJ_COL_TILE),
        out_shape=out_shape,
        grid=(bsz, length // tm),
        in_specs=in_specs,
        out_specs=out_specs,
        compiler_params=_cparams("parallel", "parallel"),
        name="inproj" if prev is None else "outproj_inproj",
    )(*args)


def _reverse_into(src_ref, src_end, dst_ref, dst0, count):
    blk = LANES
    ri = lax.broadcasted_iota(jnp.int32, (blk, 2 * blk), 0)
    ci = lax.broadcasted_iota(jnp.int32, (blk, 2 * blk), 1)
    rev = ((ri + ci == blk) | ((ri == 0) & (ci == blk))).astype(BF16)
    for i in range(count // blk):
        lo = src_ref[src_end - (i + 1) * blk:src_end - i * blk, :]
        hi = (src_ref[src_end - i * blk:src_end - (i - 1) * blk, :] if i > 0
              else jnp.zeros_like(lo))
        dst_ref[dst0 + i * blk:dst0 + (i + 1) * blk, :] = jnp.dot(
            rev, jnp.concatenate([lo, hi], axis=0), preferred_element_type=F32).astype(BF16)


def _fourier_kernel(u_ref, zf_ref, fseq_ref, fch_ref, wf_ref, bf_ref, o_ref,
                    urev_ref, ucs_ref, mix_ref, mirror_ref, *, row_tile):
    length, width = u_ref.shape
    half = length // 2
    scale = 1.0 / math.sqrt(length)

    _reverse_into(u_ref, length, urev_ref, 0, half)
    mids = []
    for c0 in range(0, width, MXU_WIDTH):
        cols = slice(c0, c0 + MXU_WIDTH)
        uh = u_ref[0:half, cols]
        ur = urev_ref[:, cols]
        cos_ch = fch_ref[cols, cols]
        sin_ch = fch_ref[cols, width + c0:width + c0 + MXU_WIDTH]
        ucs_ref[0:half, cols] = (jnp.dot(uh, cos_ch, preferred_element_type=F32)
                                 + jnp.dot(ur, cos_ch, preferred_element_type=F32)).astype(BF16)
        ucs_ref[half:length, cols] = (jnp.dot(uh, sin_ch, preferred_element_type=F32)
                                      - jnp.dot(ur, sin_ch, preferred_element_type=F32)).astype(BF16)
        mids.append(jnp.dot(u_ref[half:half + BF16_SUBLANES, cols], cos_ch,
                            preferred_element_type=F32)[0:1, :] * scale)
    mid = jnp.concatenate(mids, axis=1)

    for r in range(half // row_tile):
        rows = slice(r * row_tile, (r + 1) * row_tile)
        c_part = jnp.dot(fseq_ref[rows, 0:half], ucs_ref[0:half, :], preferred_element_type=F32)
        s_part = jnp.dot(fseq_ref[rows, half:length], ucs_ref[half:length, :],
                         preferred_element_type=F32)
        k = r * row_tile + lax.broadcasted_iota(jnp.int32, (row_tile, 1), 0)
        base = c_part + (1 - 2 * (k & 1)).astype(F32) * mid
        mix_ref[rows, :] = (base + s_part).astype(BF16)
        mirror_ref[rows, :] = (base - s_part).astype(BF16)
    _reverse_into(mirror_ref, half, mix_ref, half, half)
    li = lax.broadcasted_iota(jnp.int32, (BF16_SUBLANES, half), 1)
    alt = (1 - 2 * (li & 1)).astype(BF16)
    row_half = (jnp.dot(alt, ucs_ref[0:half, :], preferred_element_type=F32)[0:1, :] * scale
                + (1.0 if half % 2 == 0 else -1.0) * mid)
    first = mix_ref[half:half + BF16_SUBLANES, :]
    sub = lax.broadcasted_iota(jnp.int32, first.shape, 0)
    mix_ref[half:half + BF16_SUBLANES, :] = jnp.where(sub == 0, row_half.astype(BF16), first)

    for r in range(length // row_tile):
        rows = slice(r * row_tile, (r + 1) * row_tile)
        lin = jnp.dot(mix_ref[rows, :], wf_ref[...], preferred_element_type=F32) + bf_ref[...]
        o_ref[rows, :] = (lin * _silu(zf_ref[rows, :].astype(F32))).astype(BF16)


def _fourier(u, zf, fseq, fch, w_f, layer, b_f):
    bsz, length, width = u.shape
    half = length // 2
    assert half % LANES == 0, "the block-wise reversal needs L/2 to be a multiple of 128"
    assert width % MXU_WIDTH == 0 and MXU_WIDTH % FOURIER_GROUP == 0
    batch_block = pl.BlockSpec((None, length, width), lambda b: (b, 0, 0))
    return pl.pallas_call(
        functools.partial(_fourier_kernel, row_tile=min(half, FOURIER_ROW_TILE)),
        out_shape=jax.ShapeDtypeStruct((bsz, length, width), BF16),
        grid=(bsz,),
        in_specs=[batch_block, batch_block, _resident(fseq.shape), _resident(fch.shape),
                  _resident(w_f.shape[1:], layer), _resident(b_f.shape)],
        out_specs=batch_block,
        scratch_shapes=[pltpu.VMEM((half, width), BF16),
                        pltpu.VMEM((length, width), BF16),
                        pltpu.VMEM((length, width), BF16),
                        pltpu.VMEM((half, width), BF16)],
        compiler_params=_cparams("parallel"),
        name="fourier",
    )(u, zf, fseq, fch, w_f, b_f)


@functools.lru_cache(maxsize=None)
def _dft_tables(length, width):
    half = length // 2
    kl = (np.arange(half)[:, None] * np.arange(half)[None, :]) % length
    ang = kl * (2.0 * np.pi / length)
    fseq = np.concatenate([np.cos(ang), -np.sin(ang)], axis=1) / np.sqrt(length)
    ch = np.arange(width)
    same = (ch[:, None] // FOURIER_GROUP) == (ch[None, :] // FOURIER_GROUP)
    angc = (((ch[:, None] % FOURIER_GROUP) * (ch[None, :] % FOURIER_GROUP)) % FOURIER_GROUP) * (
        2.0 * np.pi / FOURIER_GROUP)
    fch = np.concatenate([np.where(same, np.cos(angc), 0.0),
                          np.where(same, np.sin(angc), 0.0)], axis=1) / np.sqrt(FOURIER_GROUP)
    return fseq.astype(np.float32), fch.astype(np.float32)


HP = BF16_SUBLANES
CONV_SUB = 128


def _split2(v):
    hi = v.astype(BF16)
    return hi, (v - hi.astype(F32)).astype(BF16)


def _split3(v):
    hi, _ = _split2(v)
    r1 = v - hi.astype(F32)
    mid, lo = _split2(r1)
    return hi, mid, lo


def _expand_rows(cols, n_tiles, lane_lo):
    tiles = [jnp.where(lane_lo, cols[:, 2 * t:2 * t + 1], cols[:, 2 * t + 1:2 * t + 2])
             for t in range(n_tiles)]
    return jnp.concatenate(tiles, axis=1)


def _ssd_kernel(*refs, segments, hpg):
    n_seg = len(segments)
    seg_refs = [refs[6 * i:6 * i + 6] for i in range(n_seg)]
    pos = 6 * n_seg
    cwx_ref, cwb_ref, cwc_ref, cbx_ref, cbb_ref, cbc_ref, dsk_ref, nw_ref = refs[pos:pos + 8]
    pos += 8
    y_refs = []
    for seg in segments:
        y_refs.append(refs[pos] if seg.emit_y else None)
        pos += 1 if seg.emit_y else 0
    (sm_ref, s0_ref, sp_ref, cc_ref, rowq_ref, tblk_ref, tb_ref, yd_ref, contrib_ref, dec_ref,
     sball_ref, sf_ref, sb_ref) = refs[pos:]

    gw = hpg * HEAD_DIM
    n_tiles = gw // LANES
    conv_w = s0_ref.shape[1]
    shifted = (sm_ref, s0_ref, sp_ref)
    ri = lax.broadcasted_iota(jnp.int32, (CHUNK, CHUNK), 0)
    ci = lax.broadcasted_iota(jnp.int32, (CHUNK, CHUNK), 1)
    tri_t = (ri <= ci).astype(BF16)
    below = ci < ri
    on_diag = ci == ri
    lane_lo = lax.broadcasted_iota(jnp.int32, (1, LANES), 1) < HEAD_DIM
    erow = lax.broadcasted_iota(jnp.int32, (CHUNK, 2 * gw), 0)
    ehead = lax.broadcasted_iota(jnp.int32, (CHUNK, 2 * gw), 1) // HEAD_DIM
    expand_w = (erow == ehead + 2 * HP).astype(BF16)
    expand_off = (erow == ehead + 3 * HP).astype(BF16)

    def chunk_rows(c):
        return pl.ds(pl.multiple_of(c * CHUNK, CHUNK), CHUNK)

    def run_segment(x_ref, b_ref, c_ref, z_ref, dtr_ref, pr_ref, y_ref, seg):
        length, grid_rows, grid_cols = seg.length, seg.grid_rows, seg.grid_cols
        nc = length // CHUNK
        nrow = nc * HP

        halo = grid_cols if grid_rows > 1 else 0
        if halo:
            zeros = jnp.zeros((halo, conv_w), BF16)
            for s_ref in shifted:
                s_ref[0:halo, :] = zeros
                s_ref[halo + length:2 * halo + length, :] = zeros
        sources = ((x_ref, 0, gw), (b_ref, gw, D_STATE), (c_ref, gw + D_STATE, D_STATE))

        def shift_row(r, carry):
            src_rows = pl.ds(pl.multiple_of(r * grid_cols, grid_cols), grid_cols)
            dst_rows = pl.ds(pl.multiple_of(halo + r * grid_cols, grid_cols), grid_cols)
            for ref, lane0, width in sources:
                wpos = lax.broadcasted_iota(jnp.int32, (grid_cols, width), 0)
                xb = ref[src_rows, :]
                xr = xb.astype(F32)
                dst = slice(lane0, lane0 + width)
                s0_ref[dst_rows, dst] = xb
                sm_ref[dst_rows, dst] = jnp.where(
                    wpos == 0, 0.0, pltpu.roll(xr, 1, axis=0)).astype(BF16)
                sp_ref[dst_rows, dst] = jnp.where(
                    wpos == grid_cols - 1, 0.0, pltpu.roll(xr, grid_cols - 1, axis=0)).astype(BF16)
            return carry

        lax.fori_loop(0, grid_rows, shift_row, 0)
        dhs = (-1, 0, 1) if grid_rows > 1 else (0,)

        def conv_tile(c, lane0, w_ref, b_ref_, t):
            wl = slice(t * LANES, (t + 1) * LANES)
            sl = slice(lane0 + t * LANES, lane0 + (t + 1) * LANES)
            bias = b_ref_[:, wl]
            outs = []
            for s in range(CHUNK // CONV_SUB):
                base = pl.multiple_of(c * CHUNK, CHUNK) + s * CONV_SUB
                acc = bias
                for dh in dhs:
                    off = halo + base + dh * grid_cols
                    part = None
                    for kw, s_ref in enumerate(shifted):
                        k = 3 * (dh + 1) + kw
                        term = w_ref[k:k + 1, wl].astype(BF16) * s_ref[pl.ds(off, CONV_SUB), sl]
                        part = term if part is None else part + term
                    acc = acc + part.astype(F32)
                outs.append(_silu(acc))
            return jnp.concatenate(outs, axis=0)

        row_fwd = (lax.broadcasted_iota(jnp.int32, (nrow, 1), 0) & (HP - 1)) < hpg
        dt_r = _softplus(dtr_ref[...].reshape(nrow, CHUNK) + pr_ref[:, 0:1])
        a_r = dt_r * (-jnp.exp(pr_ref[:, 1:2]))
        cum_r = sum(jnp.dot(p, tri_t, preferred_element_type=F32) for p in _split3(a_r))
        excl_r = cum_r - a_r
        tot_r = jnp.broadcast_to(cum_r[:, CHUNK - 1:CHUNK], cum_r.shape)
        log_dt = jnp.log(dt_r)
        rowq_ref[0, 0:nrow, :] = dt_r
        rowq_ref[1, 0:nrow, :] = cum_r - log_dt
        rowq_ref[2, 0:nrow, :] = excl_r + log_dt
        to_cols = (
            jnp.where(row_fwd, cum_r, excl_r),
            tot_r,
            jnp.where(row_fwd, jnp.exp(tot_r - cum_r), jnp.exp(excl_r)) * dt_r,
            jnp.where(row_fwd, jnp.exp(cum_r), jnp.exp(tot_r - excl_r)),
        )
        for c in range(nc):
            for q, val in enumerate(to_cols):
                tblk_ref[c, q * HP:(q + 1) * HP, :] = val[c * HP:(c + 1) * HP, :]
            tblk_ref[c, len(to_cols) * HP:, :] = jnp.zeros((CHUNK - len(to_cols) * HP, CHUNK), F32)

        def phase_a(c, carry):
            rows = chunk_rows(c)
            rrows = pl.ds(pl.multiple_of(c * HP, HP), HP)
            blk_t = tblk_ref[c].T
            blk_tb = blk_t.astype(BF16)
            tb_ref[c] = blk_tb
            dt_row = rowq_ref[0, rrows, :]
            cf_row = rowq_ref[1, rrows, :]
            eb_row = rowq_ref[2, rrows, :]

            bmat = conv_tile(c, gw, cwb_ref, cbb_ref, 0).astype(BF16)
            cmat = conv_tile(c, gw + D_STATE, cwc_ref, cbc_ref, 0).astype(BF16)
            cc_ref[rows, :] = cmat
            cb = lax.dot_general(cmat, bmat, (((1,), (1,)), ((), ())), preferred_element_type=F32)

            def mix(h):
                arg = jnp.where(below, blk_t[:, h:h + 1] - cf_row[h:h + 1, :],
                                eb_row[hpg + h:hpg + h + 1, :] - blk_t[:, hpg + h:hpg + h + 1])
                e = jnp.exp(arg) + jnp.where(on_diag, dt_row[h:h + 1, :], 0.0)
                return (cb * e).astype(BF16)

            w_exp = jnp.dot(blk_tb, expand_w, preferred_element_type=F32)

            xs_f, xs_b = [], []
            for t in range(n_tiles):
                lanes = slice(t * LANES, (t + 1) * LANES)
                xf = conv_tile(c, 0, cwx_ref, cbx_ref, t)
                xt = xf.astype(BF16)
                zero = jnp.zeros_like(xt)
                rhs = jnp.concatenate([jnp.where(lane_lo, xt, zero), jnp.where(lane_lo, zero, xt)],
                                      axis=0)
                lhs = jnp.concatenate([mix(2 * t), mix(2 * t + 1)], axis=1)
                yd_ref[rows, lanes] = (jnp.dot(lhs, rhs, preferred_element_type=F32)
                                       + dsk_ref[:, lanes] * xf)
                xs_f.append(xt * w_exp[:, t * LANES:(t + 1) * LANES].astype(BF16))
                xs_b.append(xt * w_exp[:, gw + t * LANES:gw + (t + 1) * LANES].astype(BF16))
            contrib_ref[c] = lax.dot_general(bmat, jnp.concatenate(xs_f + xs_b, axis=1),
                                             (((0,), (0,)), ((), ())), preferred_element_type=F32)
            dec_ref[c] = _expand_rows(jnp.exp(blk_t[0:1, HP:2 * HP]), 2 * n_tiles, lane_lo)
            return carry

        lax.fori_loop(0, nc, phase_a, 0, unroll=min(nc, 8))

        def phase_b(k, carry):
            c = nc - 1 - k
            sb = sb_ref[...]
            sball_ref[c] = sb.astype(BF16)
            sb_ref[...] = sb * dec_ref[c][:, gw:] + contrib_ref[c, :, gw:]
            return carry

        lax.fori_loop(0, nc, phase_b, 0, unroll=min(nc, 4))

        def phase_c(c, carry):
            rows = chunk_rows(c)
            sf = sf_ref[...]
            if y_ref is not None:
                s_cat = jnp.concatenate([sf.astype(BF16), sball_ref[c]], axis=1)
                cs = jnp.dot(cc_ref[rows, :], s_cat, preferred_element_type=F32)
                off = jnp.dot(tb_ref[c], expand_off, preferred_element_type=F32)
                g_tiles = []
                ssq = jnp.zeros((CHUNK, 1), F32)
                for t in range(n_tiles):
                    lanes = slice(t * LANES, (t + 1) * LANES)
                    blanes = slice(gw + t * LANES, gw + (t + 1) * LANES)
                    y = (yd_ref[rows, lanes] + cs[:, lanes] * off[:, lanes]
                         + cs[:, blanes] * off[:, blanes])
                    g = y * _silu(z_ref[rows, lanes].astype(F32))
                    ssq = ssq + jnp.sum(g * g, axis=-1, keepdims=True)
                    g_tiles.append(g)
                inv = lax.rsqrt(ssq * (1.0 / gw) + EPS)
                for t in range(n_tiles):
                    lanes = slice(t * LANES, (t + 1) * LANES)
                    y_ref[rows, lanes] = ((g_tiles[t] * inv) * nw_ref[:, lanes]).astype(BF16)
            sf_ref[...] = sf * dec_ref[c][:, :gw] + contrib_ref[c, :, :gw]
            return carry

        lax.fori_loop(0, nc, phase_c, 0, unroll=min(nc, 16))

    sf_ref[...] = jnp.zeros(sf_ref.shape, F32)
    sb_ref[...] = jnp.zeros(sb_ref.shape, F32)
    for (x_ref, b_ref, c_ref, z_ref, dtr_ref, pr_ref), y_ref, seg in zip(seg_refs, y_refs, segments):
        run_segment(x_ref, b_ref, c_ref, z_ref, dtr_ref, pr_ref, y_ref, seg)


class _Segment(NamedTuple):
    length: int
    grid_rows: int
    grid_cols: int
    emit_y: bool


def _ssd(token_sets, p_col, conv_w9, conv_b, d_exp, norm_w, d_ssd):
    assert HEADS_PER_TILE == 2, "the kernel pairs heads 2t, 2t + 1 on one 128-lane tile"
    g = N_BC_GROUPS
    gw = d_ssd // g
    hpg = gw // HEAD_DIM
    bc_first = d_ssd // D_STATE
    pad = HP - 2 * hpg
    p_col = jnp.pad(p_col, ((0, 0), (0, 0), (0, pad)))
    bsz = token_sets[0][0].shape[0]
    segments, args, in_specs, out_shape, out_specs = [], [], [], [], []
    for xbc, zs, dt, grid_rows, grid_cols, emit_y in token_sets:
        length = xbc.shape[1]
        nc = length // CHUNK
        segments.append(_Segment(length, grid_rows, grid_cols, emit_y))
        p_row = jnp.tile(p_col.transpose(0, 2, 1), (1, nc, 1))
        args += [xbc, xbc, xbc, zs, dt, p_row]
        in_specs += [
            pl.BlockSpec((None, length, gw), lambda b, j: (b, 0, j)),
            pl.BlockSpec((None, length, D_STATE), lambda b, j: (b, 0, bc_first + j)),
            pl.BlockSpec((None, length, D_STATE), lambda b, j: (b, 0, bc_first + g + j)),
            pl.BlockSpec((None, length, gw), lambda b, j: (b, 0, j)),
            pl.BlockSpec((None, nc, HP, CHUNK), lambda b, j: (b, 0, j, 0)),
            pl.BlockSpec((None, nc * HP, 2), lambda b, j: (j, 0, 0)),
        ]
        if emit_y:
            out_shape.append(jax.ShapeDtypeStruct((bsz, length, d_ssd), BF16))
            out_specs.append(pl.BlockSpec((None, length, gw), lambda b, j: (b, 0, j)))
    args += [conv_w9, conv_w9, conv_w9, conv_b, conv_b, conv_b, d_exp, norm_w]
    in_specs += [
        pl.BlockSpec((9, gw), lambda b, j: (0, j)),
        pl.BlockSpec((9, D_STATE), lambda b, j: (0, bc_first + j)),
        pl.BlockSpec((9, D_STATE), lambda b, j: (0, bc_first + g + j)),
        pl.BlockSpec((1, gw), lambda b, j: (0, j)),
        pl.BlockSpec((1, D_STATE), lambda b, j: (0, bc_first + j)),
        pl.BlockSpec((1, D_STATE), lambda b, j: (0, bc_first + g + j)),
        pl.BlockSpec((1, gw), lambda b, j: (0, j)),
        pl.BlockSpec((1, gw), lambda b, j: (0, j)),
    ]
    max_len = max(seg.length for seg in segments)
    max_nc = max_len // CHUNK
    conv_rows = max(seg.length + (2 * seg.grid_cols if seg.grid_rows > 1 else 0) for seg in segments)
    conv_w = gw + 2 * D_STATE
    outs = pl.pallas_call(
        functools.partial(_ssd_kernel, segments=tuple(segments), hpg=hpg),
        out_shape=out_shape,
        grid=(bsz, g),
        in_specs=in_specs,
        out_specs=out_specs,
        scratch_shapes=[pltpu.VMEM((conv_rows, conv_w), BF16),
                        pltpu.VMEM((conv_rows, conv_w), BF16),
                        pltpu.VMEM((conv_rows, conv_w), BF16),
                        pltpu.VMEM((max_len, D_STATE), BF16),
                        pltpu.VMEM((3, max_nc * HP, CHUNK), F32),
                        pltpu.VMEM((max_nc, CHUNK, CHUNK), F32),
                        pltpu.VMEM((max_nc, CHUNK, CHUNK), BF16),
                        pltpu.VMEM((max_len, gw), F32),
                        pltpu.VMEM((max_nc, D_STATE, 2 * gw), F32),
                        pltpu.VMEM((max_nc, 1, 2 * gw), F32),
                        pltpu.VMEM((max_nc, D_STATE, gw), BF16),
                        pltpu.VMEM((D_STATE, gw), F32),
                        pltpu.VMEM((D_STATE, gw), F32)],
        compiler_params=_cparams("parallel", "parallel"),
        name="ssd",
    )(*args)
    outs = list(outs)
    return [outs.pop(0) if seg.emit_y else None for seg in segments]


def _final_kernel(four_ref, ssd_ref, x_ref, gate_ref, w_ref, fnw_ref, o_ref):
    y = _residual(four_ref, ssd_ref, x_ref, gate_ref, w_ref)
    ms = jnp.mean(y * y, axis=-1, keepdims=True)
    o_ref[...] = (y * lax.rsqrt(ms + EPS)) * fnw_ref[...]


def _final(four, ssd, x, gate, w_out, layer, final_norm_w):
    bsz, length, d = x.shape
    tm = min(length, FINAL_ROW_TILE)
    row_map = lambda b, t: (b, t, 0)
    return pl.pallas_call(
        _final_kernel,
        out_shape=jax.ShapeDtypeStruct((bsz, length, d), F32),
        grid=(bsz, length // tm),
        in_specs=[
            pl.BlockSpec((None, tm, four.shape[2]), row_map),
            pl.BlockSpec((None, tm, ssd.shape[2]), row_map),
            pl.BlockSpec((None, tm, d), row_map),
            pl.BlockSpec((None, 1, d), lambda b, t: (b, 0, 0)),
            _resident(w_out.shape[1:], layer),
            _resident((1, d)),
        ],
        out_specs=pl.BlockSpec((None, tm, d), row_map),
        compiler_params=_cparams("parallel", "parallel"),
        name="final",
    )(four, ssd, x, gate, w_out, final_norm_w)


def kernel(x, c, ctx, c_ctx, norm_w, w_ada, b_ada, w_in, conv_w, conv_b, dt_bias, a_log, d_skip,
           ssd_norm_w, w_fourier, b_fourier, w_out, final_norm_w):
    bsz, seq_len, d = x.shape
    ctx_len = ctx.shape[1]
    depth = w_in.shape[0]
    d_f = w_fourier.shape[1]
    d_ssd = ssd_norm_w.shape[1]
    conv_ch = conv_w.shape[-1]
    n_heads = d_skip.shape[1]
    g = N_BC_GROUPS
    hpg = n_heads // g
    splits = (d_f, d_f, d_ssd, conv_ch)

    pad_rows = (-(bsz + 1)) % 8
    cc = jnp.concatenate([c, c_ctx[None, :], jnp.zeros((pad_rows, d), F32)], axis=0)
    mod = _adaln(cc, w_ada, b_ada)

    w_in_t = jnp.swapaxes(w_in, 1, 2)
    w_dt = w_in[:, :, sum(splits):].reshape(depth, d, 2, g, hpg).transpose(0, 3, 2, 4, 1)
    w_dt = jnp.pad(w_dt.reshape(depth, g, 2 * hpg, d), ((0, 0), (0, 0), (0, HP - 2 * hpg), (0, 0)))
    w_dt = w_dt.reshape(depth, g * HP, d)
    w_f_b = w_fourier.astype(BF16)
    w9 = conv_w.reshape(depth, 9, conv_ch)
    par = jnp.stack([dt_bias, a_log], axis=1)
    p_col = par.reshape(depth, 2, 2, g, hpg).transpose(0, 3, 1, 2, 4).reshape(depth, g, 2, 2 * hpg)
    d_exp = jnp.repeat(d_skip, HEAD_DIM, axis=1).reshape(depth, 1, d_ssd)

    fseq_x, fch = (jnp.asarray(t).astype(BF16) for t in _dft_tables(seq_len, d_f))
    fseq_c = jnp.asarray(_dft_tables(ctx_len, d_f)[0]).astype(BF16)
    rows = seq_len // GRID_W

    prev_c = prev_x = None
    for i in range(depth):
        last = i == depth - 1
        m = mod[i]
        shift, scale, gate = (m[:bsz, None, k * d:(k + 1) * d] for k in range(3))
        shift_c, scale_c, gate_c = (m[bsz:bsz + 1, None, k * d:(k + 1) * d] for k in range(3))
        nw_i = norm_w[i].reshape(1, d)
        outs_c = _inproj(ctx, shift_c, scale_c, nw_i, w_in_t, w_dt, i, splits, prev_c)
        outs_x = _inproj(x, shift, scale, nw_i, w_in_t, w_dt, i, splits, prev_x)
        if i > 0:
            ctx, outs_c = outs_c[0], outs_c[1:]
            x, outs_x = outs_x[0], outs_x[1:]
        (u_c, zf_c, zs_c, xbc_c, dt_c), (u_x, zf_x, zs_x, xbc_x, dt_x) = outs_c, outs_x
        ssd_c, ssd_x = _ssd(
            [(xbc_c, zs_c, dt_c, 1, ctx_len, not last), (xbc_x, zs_x, dt_x, rows, GRID_W, True)],
            p_col[i], w9[i], conv_b[i].reshape(1, conv_ch), d_exp[i],
            ssd_norm_w[i].reshape(1, d_ssd), d_ssd)
        b_f = b_fourier[i].reshape(1, d_f)
        four_x = _fourier(u_x, zf_x, fseq_x, fch, w_f_b, i, b_f)
        prev_x = (four_x, ssd_x, gate, w_out)
        if not last:
            four_c = _fourier(u_c, zf_c, fseq_c, fch, w_f_b, i, b_f)
            prev_c = (four_c, ssd_c, gate_c, w_out)
    four_x, ssd_x, gate, w_o = prev_x
    return _final(four_x, ssd_x, x, gate, w_o, depth - 1, final_norm_w.reshape(1, d))
```
